```python
import math
import jax, jax.numpy as jnp
from jax import lax
import numpy as np

D_MODEL = 1024
BATCH = 16
SEQ = 4096
DEPTH = 2
DEC_BATCH = 16
DEC_SEQ = 16
PAST_LEN = 4096

CHUNK = 64
N_A_LAYERS = DEPTH // 2
N_B_LAYERS = DEPTH - N_A_LAYERS
GLA_HEADS = 4
GLA_DK = D_MODEL // (2 * GLA_HEADS)
GLA_DV = D_MODEL // GLA_HEADS
GLA_RANK = 16
GLA_TAU = 16.0
ATT_HEADS = 16
ATT_HD = D_MODEL // ATT_HEADS
BAND_CHUNKS = 8
BAND_PAST = BAND_CHUNKS * CHUNK
REL_CLIP = 128
N_EXPERTS = 32
TOP_K = 4
D_FF = D_MODEL
SWIGLU_LIMIT = 7.0
SWIGLU_ALPHA = 1.702
MOE_BLOCK = 256
DEEPNORM_ALPHA = (2.0 * DEPTH) ** 0.25
DEEPNORM_BETA = (8.0 * DEPTH) ** -0.25
LN_EPS = 1e-5
NEG_INF = -1e30

kernel_name = 'yoco_gla_chunkband_moe_stream_step'


def layer_norm(x, g, b):
    xf = x.astype(jnp.float32)
    mu = jnp.mean(xf, -1, keepdims=True)
    var = jnp.mean(jnp.square(xf - mu), -1, keepdims=True)
    return ((xf - mu) * lax.rsqrt(var + LN_EPS) * g + b).astype(x.dtype)


def ada_modulation(c, w, b):
    m = jax.nn.silu(c) @ w + b
    return jnp.split(m[:, None, :], 6, axis=-1)


def gla_recurrence(q, k, v, log_a, s0):
    B, L, H, DK = q.shape
    DV = v.shape[-1]
    C = min(CHUNK, L)
    NC = L // C

    def chunks(a):
        return a.astype(jnp.float32).reshape(B, NC, C, H, a.shape[-1]).swapaxes(0, 1)

    qc = chunks(q) * (DK ** -0.5)
    kc = chunks(k)
    vc = chunks(v)
    bc = jnp.cumsum(chunks(log_a), axis=2)
    causal = (jnp.arange(C)[:, None] >= jnp.arange(C)[None, :])[None, :, :, None, None]

    def step(S, xs):
        qn, kn, vn, bn = xs
        diff = bn[:, :, None] - bn[:, None, :]
        w = jnp.exp(jnp.where(causal, diff, -jnp.inf))
        a = jnp.einsum('bthd,bshd,btshd->bhts', qn, kn, w)
        o = jnp.einsum('bhts,bshv->bthv', a, vn) + jnp.einsum('bthd,bhdv->bthv', qn * jnp.exp(bn), S)
        bl = bn[:, -1]
        S = jnp.exp(bl)[..., None] * S + jnp.einsum('bshd,bshv->bhdv', kn * jnp.exp(bl[:, None] - bn), vn)
        return S, o

    s_fin, o = lax.scan(step, s0.astype(jnp.float32), (qc, kc, vc, bc))
    return o.swapaxes(0, 1).reshape(B, L, H, DV), s_fin


def gla_mixer(h, s0, wq, wk, wv, wa1, wa2, ba, wr, norm_g, wo):
    B, L, _ = h.shape
    q = (h @ wq).reshape(B, L, GLA_HEADS, GLA_DK)
    k = (h @ wk).reshape(B, L, GLA_HEADS, GLA_DK)
    v = (h @ wv).reshape(B, L, GLA_HEADS, GLA_DV)
    log_a = jax.nn.log_sigmoid(((h @ wa1) @ wa2 + ba).astype(jnp.float32)) / GLA_TAU
    o, s_fin = gla_recurrence(q, k, v, log_a.reshape(B, L, GLA_HEADS, GLA_DK), s0)
    o = o * lax.rsqrt(jnp.mean(jnp.square(o), -1, keepdims=True) + LN_EPS) * norm_g
    o = o.reshape(B, L, GLA_HEADS * GLA_DV).astype(h.dtype) * jax.nn.silu(h @ wr)
    return o @ wo, s_fin.astype(s0.dtype)


def band_attention(q, k_ext, v_ext, valid, rel_bias):
    B, L, H, HD = q.shape
    C = min(CHUNK, L)
    NC = L // C
    W = BAND_PAST + C
    dist = jnp.arange(C)[:, None] - jnp.arange(W)[None, :] + BAND_PAST
    bias = rel_bias[:, jnp.clip(dist, -REL_CLIP, REL_CLIP) + REL_CLIP].astype(jnp.float32)
    qc = (q * (HD ** -0.5)).reshape(B, NC, C, H, HD).swapaxes(0, 1)

    def one_chunk(args):
        qb, n = args
        start = n * C
        kb = lax.dynamic_slice_in_dim(k_ext, start, W, axis=1)
        vb = lax.dynamic_slice_in_dim(v_ext, start, W, axis=1)
        mb = lax.dynamic_slice_in_dim(valid, start, W, axis=0)
        s = jnp.einsum('bthd,buhd->bhtu', qb, kb).astype(jnp.float32) + bias
        s = jnp.where(mb, s, NEG_INF)
        p = jax.nn.softmax(s, axis=-1).astype(vb.dtype)
        return jnp.einsum('bhtu,buhd->bthd', p, vb)

    o = lax.map(one_chunk, (qc, jnp.arange(NC)))
    return o.swapaxes(0, 1).reshape(B, L, H, HD)


def moe(h, wr, br, wg, bg, wu, bu, wd, bd):
    B, L, D = h.shape
    T = B * L
    x = h.reshape(T, D)
    logits = (x @ wr).astype(jnp.float32) + br.astype(jnp.float32)
    top_v, top_i = lax.top_k(logits, TOP_K)
    gates = jax.nn.softmax(top_v, axis=-1)
    flat_e = top_i.reshape(-1).astype(jnp.int32)
    flat_t = jnp.repeat(jnp.arange(T, dtype=jnp.int32), TOP_K)
    order = jnp.argsort(flat_e)
    se = flat_e[order]
    st = flat_t[order]
    sg = gates.reshape(-1)[order]
    counts = jnp.zeros((N_EXPERTS,), jnp.int32).at[flat_e].add(1)
    padded = (counts + MOE_BLOCK - 1) // MOE_BLOCK * MOE_BLOCK
    pend = jnp.cumsum(padded)
    dest = (pend - padded)[se] + jnp.arange(T * TOP_K, dtype=jnp.int32) - (jnp.cumsum(counts) - counts)[se]
    n_blocks = -(-(T * TOP_K) // MOE_BLOCK) + N_EXPERTS
    n_slots = n_blocks * MOE_BLOCK
    slot_t = jnp.zeros((n_slots,), jnp.int32).at[dest].set(st)
    slot_g = jnp.zeros((n_slots,), jnp.float32).at[dest].set(sg)
    block_e = jnp.minimum(jnp.searchsorted(pend, jnp.arange(n_blocks, dtype=jnp.int32) * MOE_BLOCK, side='right'), N_EXPERTS - 1)

    def expert_block(args):
        tok, g, e = args
        xb = x[tok]
        a = jnp.minimum(xb @ wg[e] + bg[e], SWIGLU_LIMIT)
        u = jnp.clip(xb @ wu[e] + bu[e], -SWIGLU_LIMIT, SWIGLU_LIMIT)
        y = ((u + 1) * (a * jax.nn.sigmoid(SWIGLU_ALPHA * a))) @ wd[e] + bd[e]
        return y * g[:, None].astype(y.dtype)

    yb = lax.map(expert_block, (slot_t.reshape(n_blocks, MOE_BLOCK), slot_g.reshape(n_blocks, MOE_BLOCK), block_e))
    out = jnp.zeros((T, D), h.dtype).at[slot_t].add(yb.reshape(n_slots, D).astype(h.dtype))
    return out.reshape(B, L, D)


def trunk(x, c, gla_s0, past_k, past_v, past_valid, weights):
    (ada_w, ada_b, ln_g, ln_b, gla_wq, gla_wk, gla_wv, gla_wa1, gla_wa2, gla_ba, gla_wr, gla_norm_g,
     gla_wo, kv_wk, kv_wv, att_wq, att_wo, att_rel_bias, moe_wr, moe_br, moe_wg, moe_bg, moe_wu,
     moe_bu, moe_wd, moe_bd) = weights
    B, L, _ = x.shape
    new_s = []
    for layer in range(DEPTH):
        sh_m, sc_m, g_m, sh_f, sc_f, g_f = ada_modulation(c, ada_w[layer], ada_b[layer])
        if layer == N_A_LAYERS:
            k_new = (x @ kv_wk).reshape(B, L, ATT_HEADS, ATT_HD)
            v_new = (x @ kv_wv).reshape(B, L, ATT_HEADS, ATT_HD)
            k_ext = jnp.concatenate([past_k.astype(k_new.dtype), k_new], axis=1)
            v_ext = jnp.concatenate([past_v.astype(v_new.dtype), v_new], axis=1)
            valid = jnp.concatenate([past_valid, jnp.ones((L,), bool)])
        h = x * (1 + sc_m) + sh_m
        if layer < N_A_LAYERS:
            mix, s_fin = gla_mixer(h, gla_s0[layer], gla_wq[layer], gla_wk[layer], gla_wv[layer],
                                   gla_wa1[layer], gla_wa2[layer], gla_ba[layer], gla_wr[layer],
                                   gla_norm_g[layer], gla_wo[layer])
            new_s.append(s_fin)
        else:
            j = layer - N_A_LAYERS
            q = (h @ att_wq[j]).reshape(B, L, ATT_HEADS, ATT_HD)
            mix = band_attention(q, k_ext, v_ext, valid, att_rel_bias[j]).reshape(B, L, D_MODEL) @ att_wo[j]
        x = layer_norm(DEEPNORM_ALPHA * x + g_m * mix, ln_g[layer, 0], ln_b[layer, 0])
        h = x * (1 + sc_f) + sh_f
        ff = moe(h, moe_wr[layer], moe_br[layer], moe_wg[layer], moe_bg[layer], moe_wu[layer],
                 moe_bu[layer], moe_wd[layer], moe_bd[layer])
        x = layer_norm(DEEPNORM_ALPHA * x + g_f * ff, ln_g[layer, 1], ln_b[layer, 1])
    return x, jnp.stack(new_s), k_new, v_new


def setup_inputs(seed: int = 0) -> dict:
    key = jax.random.key(seed)
    ks = iter(jax.random.split(key, 40))

    def nrm(shape, scale):
        return jax.random.normal(next(ks), shape, jnp.float32) * scale

    D = D_MODEL
    wc = min(BAND_PAST, PAST_LEN)
    beta = DEEPNORM_BETA
    return {
        'x_prompt': nrm((BATCH, SEQ, D), 1.0),
        'x_sample': nrm((DEC_BATCH, DEC_SEQ, D), 1.0),
        'c_prompt': nrm((BATCH, D), 1.0),
        'c_sample': nrm((DEC_BATCH, D), 1.0),
        'state_gla': nrm((N_A_LAYERS, DEC_BATCH, GLA_HEADS, GLA_DK, GLA_DV), 0.5),
        'cache_k': nrm((DEC_BATCH, wc, ATT_HEADS, ATT_HD), 1.0),
        'cache_v': nrm((DEC_BATCH, wc, ATT_HEADS, ATT_HD), beta),
        'ada_w': nrm((DEPTH, D, 6 * D), D ** -0.5),
        'ada_b': nrm((DEPTH, 6 * D), 0.02),
        'ln_g': 1.0 + nrm((DEPTH, 2, D), 0.02),
        'ln_b': nrm((DEPTH, 2, D), 0.02),
        'gla_wq': nrm((N_A_LAYERS, D, GLA_HEADS * GLA_DK), D ** -0.5),
        'gla_wk': nrm((N_A_LAYERS, D, GLA_HEADS * GLA_DK), D ** -0.5),
        'gla_wv': nrm((N_A_LAYERS, D, GLA_HEADS * GLA_DV), beta * D ** -0.5),
        'gla_wa1': nrm((N_A_LAYERS, D, GLA_RANK), D ** -0.5),
        'gla_wa2': nrm((N_A_LAYERS, GLA_RANK, GLA_HEADS * GLA_DK), GLA_RANK ** -0.5),
        'gla_ba': nrm((N_A_LAYERS, GLA_HEADS * GLA_DK), 0.1),
        'gla_wr': nrm((N_A_LAYERS, D, GLA_HEADS * GLA_DV), D ** -0.5),
        'gla_norm_g': 1.0 + nrm((N_A_LAYERS, GLA_DV), 0.02),
        'gla_wo': nrm((N_A_LAYERS, GLA_HEADS * GLA_DV, D), beta * (GLA_HEADS * GLA_DV) ** -0.5),
        'kv_wk': nrm((D, ATT_HEADS * ATT_HD), D ** -0.5),
        'kv_wv': nrm((D, ATT_HEADS * ATT_HD), beta * D ** -0.5),
        'att_wq': nrm((N_B_LAYERS, D, ATT_HEADS * ATT_HD), D ** -0.5),
        'att_wo': nrm((N_B_LAYERS, ATT_HEADS * ATT_HD, D), beta * (ATT_HEADS * ATT_HD) ** -0.5),
        'att_rel_bias': nrm((N_B_LAYERS, ATT_HEADS, 2 * REL_CLIP + 1), 0.2),
        'moe_wr': nrm((DEPTH, D, N_EXPERTS), D ** -0.5),
        'moe_br': nrm((DEPTH, N_EXPERTS), 0.01),
        'moe_wg': nrm((DEPTH, N_EXPERTS, D, D_FF), D ** -0.5),
        'moe_bg': nrm((DEPTH, N_EXPERTS, D_FF), 0.02),
        'moe_wu': nrm((DEPTH, N_EXPERTS, D, D_FF), D ** -0.5),
        'moe_bu': nrm((DEPTH, N_EXPERTS, D_FF), 0.02),
        'moe_wd': nrm((DEPTH, N_EXPERTS, D_FF, D), beta * D_FF ** -0.5),
        'moe_bd': nrm((DEPTH, N_EXPERTS, D), 0.02),
    }


def reference(x_prompt, x_sample, c_prompt, c_sample, state_gla, cache_k, cache_v, ada_w, ada_b,
              ln_g, ln_b, gla_wq, gla_wk, gla_wv, gla_wa1, gla_wa2, gla_ba, gla_wr, gla_norm_g, gla_wo,
              kv_wk, kv_wv, att_wq, att_wo, att_rel_bias, moe_wr, moe_br, moe_wg, moe_bg, moe_wu,
              moe_bu, moe_wd, moe_bd):
    weights = (ada_w, ada_b, ln_g, ln_b, gla_wq, gla_wk, gla_wv, gla_wa1, gla_wa2, gla_ba, gla_wr,
               gla_norm_g, gla_wo, kv_wk, kv_wv, att_wq, att_wo, att_rel_bias, moe_wr, moe_br,
               moe_wg, moe_bg, moe_wu, moe_bu, moe_wd, moe_bd)
    P = BAND_PAST
    b_p, l_p, _ = x_prompt.shape
    s0_p = jnp.zeros((N_A_LAYERS, b_p, GLA_HEADS, GLA_DK, GLA_DV), state_gla.dtype)
    pk_p = jnp.zeros((b_p, P, ATT_HEADS, ATT_HD), x_prompt.dtype)
    valid_p = jnp.zeros((P,), bool)
    y_p, s_p, k_p, v_p = trunk(x_prompt, c_prompt, s0_p, pk_p, pk_p, valid_p, weights)
    keep = min(P, l_p)
    wc = cache_k.shape[1]
    pad = ((0, 0), (P - wc, 0), (0, 0), (0, 0))
    pk_s = jnp.pad(cache_k, pad)
    pv_s = jnp.pad(cache_v, pad)
    valid_s = jnp.arange(P) >= P - wc
    y_s, s_s, k_s, v_s = trunk(x_sample, c_sample, state_gla, pk_s, pv_s, valid_s, weights)
    return (y_p, y_s, s_p, k_p[:, l_p - keep:], v_p[:, l_p - keep:], s_s, k_s, v_s)
```

```python
import functools
import math

import jax
import jax.numpy as jnp
from jax import lax
from jax.experimental import pallas as pl
from jax.experimental.pallas import tpu as pltpu

F32 = jnp.float32
BF16 = jnp.bfloat16

CHUNK = 64
GLA_HEADS = 4
GLA_TAU = 16.0
ATT_HEADS = 16
BAND_PAST = 512
REL_CLIP = 128
N_EXPERTS = 32
TOP_K = 4
SWIGLU_LIMIT = 7.0
SWIGLU_ALPHA = 1.702
DEPTH = 2
DEEPNORM_ALPHA = (2.0 * DEPTH) ** 0.25
LN_EPS = 1e-5
NEG_INF = -1e30

V7X_VMEM_LIMIT_BYTES = 56 * 1024 * 1024
TOKEN_TILE = 512
GLA_SUB = 16
MOE_BLOCK_LARGE = 512
MOE_BLOCK_SMALL = 64
LANES = 128


def _dot(a, b):
    return jnp.dot(a, b, preferred_element_type=F32)


def _dot_nt(a, b):
    return lax.dot_general(a, b, (((1,), (1,)), ((), ())), preferred_element_type=F32)


def _dot_tn(a, b):
    return lax.dot_general(a, b, (((0,), (0,)), ((), ())), preferred_element_type=F32)


def _layer_norm(y, g, b):
    mu = jnp.mean(y, axis=-1, keepdims=True)
    yc = y - mu
    var = jnp.mean(yc * yc, axis=-1, keepdims=True)
    return yc * lax.rsqrt(var + LN_EPS) * g + b


def _const_spec(shape):
    nd = len(shape)
    return pl.BlockSpec(shape, lambda *_: (0,) * nd, pipeline_mode=pl.Buffered(1))


def _params(sem):
    return pltpu.CompilerParams(dimension_semantics=sem, vmem_limit_bytes=V7X_VMEM_LIMIT_BYTES)


def _ada_kernel(c_ref, w_ref, b_ref, o_ref):
    c = c_ref[...]
    s = c * jax.nn.sigmoid(c)
    o_ref[0] = jnp.dot(s, w_ref[0], preferred_element_type=F32,
                       precision=lax.Precision.HIGHEST) + b_ref[0]


def _ada_modulation(c, ada_w, ada_b):
    n, d = c.shape
    depth, _, d6 = ada_w.shape
    tn = d6 // 4
    return pl.pallas_call(
        _ada_kernel,
        grid=(depth, d6 // tn),
        in_specs=[pl.BlockSpec((n, d), lambda l, j: (0, 0)),
                  pl.BlockSpec((1, d, tn), lambda l, j: (l, 0, j)),
                  pl.BlockSpec((1, 1, tn), lambda l, j: (l, 0, j))],
        out_specs=pl.BlockSpec((1, n, tn), lambda l, j: (l, 0, j)),
        out_shape=jax.ShapeDtypeStruct((depth, n, d6), F32),
        compiler_params=_params(("arbitrary", "arbitrary")),
        name="ada_modulation",
    )(c, ada_w, ada_b.reshape(depth, 1, d6))


def _post_mixer(x3, mix3, g_m, sh_f, sc_f, lng, lnb, wrt, br):
    bb, tl, d = x3.shape
    rows = bb * tl
    x1 = _layer_norm(DEEPNORM_ALPHA * x3 + g_m * mix3, lng, lnb)
    hf = x1 * (1.0 + sc_f) + sh_f
    logits = lax.dot_general(wrt, hf.reshape(rows, d), (((1,), (1,)), ((), ())),
                             preferred_element_type=F32, precision=lax.Precision.HIGHEST) + br
    eidx = lax.broadcasted_iota(jnp.int32, logits.shape, 0).astype(F32)
    vals, idxs = [], []
    cur = logits
    for _ in range(TOP_K):
        m = jnp.max(cur, axis=0, keepdims=True)
        sel = jnp.min(jnp.where(cur == m, eidx, float(N_EXPERTS)), axis=0, keepdims=True)
        vals.append(m)
        idxs.append(sel)
        cur = jnp.where(eidx == sel, -jnp.inf, cur)
    ex = [jnp.exp(v - vals[0]) for v in vals]
    den = (ex[0] + ex[1]) + (ex[2] + ex[3])
    out_row = lax.broadcasted_iota(jnp.int32, (8, rows), 0)
    ti = jnp.zeros((8, rows), F32)
    tg = jnp.zeros((8, rows), F32)
    for k in range(TOP_K):
        ti = jnp.where(out_row == k, idxs[k], ti)
        tg = jnp.where(out_row == k, ex[k] / den, tg)
    return x1, hf, ti.astype(jnp.int32), tg


def _gla_kernel(x_ref, mod_ref, s0_ref, wq_ref, wk_ref, wv_ref, wr_ref, wa1_ref, wa2_ref, ba_ref,
                ng_ref, wo_ref, lng_ref, lnb_ref, wrt_ref, br_ref,
                x1_ref, hf_ref, ti_ref, tg_ref, sfin_ref,
                st_s, q_s, k_s, v_s, la_s, o_s, *, bb, tl, c):
    i = pl.program_id(1)
    d = x_ref.shape[-1]
    dk = q_s.shape[-1] // GLA_HEADS
    dv = v_s.shape[-1] // GLA_HEADS
    rows = bb * tl
    sub = min(GLA_SUB, c)

    @pl.when(i == 0)
    def _():
        st_s[...] = s0_ref[...]

    x3 = x_ref[...]
    mod = mod_ref[...]
    h = (x3 * (1.0 + mod[:, 1:2, :]) + mod[:, 0:1, :]).reshape(rows, d)
    hb = h.astype(BF16)
    q_s[...] = _dot(hb, wq_ref[...]) * (dk ** -0.5)
    k_s[...] = _dot(hb, wk_ref[...])
    v_s[...] = _dot(hb, wv_ref[...])
    a = _dot(_dot(hb, wa1_ref[...]).astype(BF16), wa2_ref[...]) + ba_ref[...]
    la_s[...] = (jnp.minimum(a, 0.0) - jnp.log1p(jnp.exp(-jnp.abs(a)))) * (1.0 / GLA_TAU)

    tril = (lax.broadcasted_iota(jnp.int32, (c, c), 0) >=
            lax.broadcasted_iota(jnp.int32, (c, c), 1)).astype(BF16)
    lane_c = lax.broadcasted_iota(jnp.int32, (sub, c), 1)
    row_c = lax.broadcasted_iota(jnp.int32, (sub, c), 0)
    krow = lax.broadcasted_iota(jnp.int32, (c, dk), 0)
    ng = ng_ref[...]
    chunks_per_seq = tl // c

    def chunk_body(n, carry):
        r0 = pl.multiple_of(n * c, c)
        bi = n // chunks_per_seq
        la = la_s[pl.ds(r0, c), :]
        la_hi = la.astype(BF16)
        la_lo = (la - la_hi.astype(F32)).astype(BF16)
        bcum = _dot(tril, la_hi) + _dot(tril, la_lo)
        for hd in range(GLA_HEADS):
            qh = q_s[pl.ds(r0, c), hd * dk:(hd + 1) * dk]
            kh = k_s[pl.ds(r0, c), hd * dk:(hd + 1) * dk]
            vh = v_s[pl.ds(r0, c), hd * dv:(hd + 1) * dv].astype(BF16)
            bh = bcum[:, hd * dk:(hd + 1) * dk]
            blocks = []
            for sb in range(c // sub):
                lo = sb * sub
                qi = qh[lo:lo + sub]
                bi_rows = bh[lo:lo + sub]
                if sb > 0:
                    ref_b = bh[lo:lo + 1]
                    qt = (qi * jnp.exp(bi_rows - ref_b)).astype(BF16)
                    kt = jnp.where(krow < lo, kh * jnp.exp(jnp.minimum(ref_b - bh, 0.0)), 0.0)
                    acc = _dot_nt(qt, kt.astype(BF16))
                else:
                    acc = jnp.zeros((sub, c), F32)
                for s in range(sub):
                    g = lo + s
                    e = jnp.exp(jnp.minimum(bi_rows - bh[g:g + 1], 0.0))
                    col = jnp.sum(qi * e * kh[g:g + 1], axis=-1, keepdims=True)
                    acc = jnp.where((lane_c == g) & (row_c >= s), col, acc)
                blocks.append(acc)
            amat = blocks[0] if len(blocks) == 1 else jnp.concatenate(blocks, axis=0)
            st = st_s[bi, hd]
            o = _dot(amat.astype(BF16), vh) + _dot_nt((qh * jnp.exp(bh)).astype(BF16), st.astype(BF16))
            ms = jnp.mean(o * o, axis=-1, keepdims=True)
            o_s[pl.ds(r0, c), hd * dv:(hd + 1) * dv] = o * lax.rsqrt(ms + LN_EPS) * ng
            bl = bh[c - 1:c]
            kd = (kh * jnp.exp(bl - bh)).astype(BF16)
            st_s[bi, hd] = st * jnp.exp(bl) + _dot_tn(vh, kd)
        return carry

    lax.fori_loop(0, rows // c, chunk_body, 0)

    r = _dot(hb, wr_ref[...])
    og = (o_s[...] * (r * jax.nn.sigmoid(r))).astype(BF16)
    mix = _dot(og, wo_ref[...]).reshape(bb, tl, d)
    x1, hf, ti, tg = _post_mixer(x3, mix, mod[:, 2:3, :], mod[:, 3:4, :], mod[:, 4:5, :],
                                 lng_ref[...], lnb_ref[...], wrt_ref[...], br_ref[...])
    x1_ref[...] = x1
    hf_ref[...] = hf
    ti_ref[0] = ti
    tg_ref[0] = tg

    @pl.when(i == pl.num_programs(1) - 1)
    def _():
        sfin_ref[...] = st_s[...]


def _gla_layer(x, mod, s0t, w, bb, tl):
    b, l, d = x.shape
    c = min(CHUNK, l)
    hk = w["wq"].shape[1]
    hv = w["wv"].shape[1]
    nb, nt = b // bb, l // tl
    rows = bb * tl
    dv, dk = s0t.shape[2], s0t.shape[3]
    kern = functools.partial(_gla_kernel, bb=bb, tl=tl, c=c)
    xmap = lambda ib, it: (ib, it, 0)
    bmap = lambda ib, it: (ib, 0, 0)
    smap = lambda ib, it: (ib, 0, 0, 0)
    rmap = lambda ib, it: (ib * nt + it, 0, 0)
    outs = pl.pallas_call(
        kern,
        grid=(nb, nt),
        in_specs=[pl.BlockSpec((bb, tl, d), xmap),
                  pl.BlockSpec((bb, 8, d), bmap),
                  pl.BlockSpec((bb, GLA_HEADS, dv, dk), smap),
                  _const_spec(w["wq"].shape), _const_spec(w["wk"].shape), _const_spec(w["wv"].shape),
                  _const_spec(w["wr"].shape), _const_spec(w["wa1"].shape), _const_spec(w["wa2"].shape),
                  _const_spec(w["ba"].shape), _const_spec(w["ng"].shape), _const_spec(w["wo"].shape),
                  _const_spec(w["lng"].shape), _const_spec(w["lnb"].shape),
                  _const_spec(w["wrt"].shape), _const_spec(w["br"].shape)],
        out_specs=[pl.BlockSpec((bb, tl, d), xmap),
                   pl.BlockSpec((bb, tl, d), xmap),
                   pl.BlockSpec((1, 8, rows), rmap),
                   pl.BlockSpec((1, 8, rows), rmap),
                   pl.BlockSpec((bb, GLA_HEADS, dv, dk), smap)],
        out_shape=[jax.ShapeDtypeStruct((b, l, d), F32),
                   jax.ShapeDtypeStruct((b, l, d), F32),
                   jax.ShapeDtypeStruct((nb * nt, 8, rows), jnp.int32),
                   jax.ShapeDtypeStruct((nb * nt, 8, rows), F32),
                   jax.ShapeDtypeStruct(s0t.shape, F32)],
        scratch_shapes=[pltpu.VMEM((bb, GLA_HEADS, dv, dk), F32),
                        pltpu.VMEM((rows, hk), F32), pltpu.VMEM((rows, hk), F32),
                        pltpu.VMEM((rows, hv), F32), pltpu.VMEM((rows, hk), F32),
                        pltpu.VMEM((rows, hv), F32)],
        compiler_params=_params(("arbitrary", "arbitrary")),
        name="gla_layer",
    )(x, mod, s0t, w["wq"], w["wk"], w["wv"], w["wr"], w["wa1"], w["wa2"], w["ba"], w["ng"],
      w["wo"], w["lng"], w["lnb"], w["wrt"], w["br"])
    return outs


def _attn_kernel(*refs, bb, tl, c, has_past):
    if has_past:
        (x_ref, mod_ref, pk_ref, pv_ref, wkv_k_ref, wkv_v_ref, wq_ref, wo_ref, bias_ref, lng_ref,
         lnb_ref, wrt_ref, br_ref, x1_ref, hf_ref, ti_ref, tg_ref, ko_ref, vo_ref,
         kw_s, vw_s, q_s, o_s) = refs
    else:
        (x_ref, mod_ref, wkv_k_ref, wkv_v_ref, wq_ref, wo_ref, bias_ref, lng_ref,
         lnb_ref, wrt_ref, br_ref, x1_ref, hf_ref, ti_ref, tg_ref, ko_ref, vo_ref,
         kw_s, vw_s, q_s, o_s) = refs
    i = pl.program_id(1)
    d = x_ref.shape[-1]
    p = BAND_PAST
    w = p + c
    rows = bb * tl
    hd2 = 2 * (d // ATT_HEADS)

    @pl.when(i == 0)
    def _():
        if has_past:
            kw_s[:, 0:p, :] = pk_ref[...]
            vw_s[:, 0:p, :] = pv_ref[...]
        else:
            kw_s[:, 0:p, :] = jnp.zeros((bb, p, d), BF16)
            vw_s[:, 0:p, :] = jnp.zeros((bb, p, d), BF16)

    x3 = x_ref[...]
    mod = mod_ref[...]
    xb = x3.reshape(rows, d).astype(BF16)
    hb = (x3 * (1.0 + mod[:, 1:2, :]) + mod[:, 0:1, :]).reshape(rows, d).astype(BF16)
    kn = _dot(xb, wkv_k_ref[...])
    vn = _dot(xb, wkv_v_ref[...])
    ko_ref[...] = kn.reshape(bb, tl, d)
    vo_ref[...] = vn.reshape(bb, tl, d)
    kw_s[:, p:p + tl, :] = kn.astype(BF16).reshape(bb, tl, d)
    vw_s[:, p:p + tl, :] = vn.astype(BF16).reshape(bb, tl, d)
    q_s[...] = (_dot(hb, wq_ref[...]) * ((d // ATT_HEADS) ** -0.5)).astype(BF16)

    lane = lax.broadcasted_iota(jnp.int32, (c, hd2), 1)
    low = lane < (hd2 // 2)
    kpos = lax.broadcasted_iota(jnp.int32, (1, w), 1)
    chunks_per_seq = tl // c

    def chunk_body(n, carry):
        r0 = pl.multiple_of(n * c, c)
        bi = n // chunks_per_seq
        ci = n - bi * chunks_per_seq
        w0 = pl.multiple_of(ci * c, c)
        if not has_past:
            valid = (kpos + (i * tl + ci * c)) >= p
        for hp in range(ATT_HEADS // 2):
            qp = q_s[pl.ds(r0, c), hp * hd2:(hp + 1) * hd2]
            kwin = kw_s[bi, pl.ds(w0, w), hp * hd2:(hp + 1) * hd2]
            vwin = vw_s[bi, pl.ds(w0, w), hp * hd2:(hp + 1) * hd2]
            outs = []
            for half in range(2):
                qm = jnp.where(low if half == 0 else ~low, qp, jnp.zeros_like(qp))
                s = _dot_nt(qm, kwin) + bias_ref[2 * hp + half]
                if not has_past:
                    s = jnp.where(valid, s, NEG_INF)
                m = jnp.max(s, axis=-1, keepdims=True)
                e = jnp.exp(s - m)
                pr = (e / jnp.sum(e, axis=-1, keepdims=True)).astype(BF16)
                outs.append(_dot(pr, vwin))
            o_s[pl.ds(r0, c), hp * hd2:(hp + 1) * hd2] = jnp.where(low, outs[0], outs[1])
        return carry

    lax.fori_loop(0, rows // c, chunk_body, 0)

    if tl >= p:
        kw_s[:, 0:p, :] = kw_s[:, tl:tl + p, :]
        vw_s[:, 0:p, :] = vw_s[:, tl:tl + p, :]

    mix = _dot(o_s[...].astype(BF16), wo_ref[...]).reshape(bb, tl, d)
    x1, hf, ti, tg = _post_mixer(x3, mix, mod[:, 2:3, :], mod[:, 3:4, :], mod[:, 4:5, :],
                                 lng_ref[...], lnb_ref[...], wrt_ref[...], br_ref[...])
    x1_ref[...] = x1
    hf_ref[...] = hf
    ti_ref[0] = ti
    tg_ref[0] = tg


def _attn_layer(x, mod, past_k, past_v, w, bb, tl):
    b, l, d = x.shape
    c = min(CHUNK, l)
    has_past = past_k is not None
    nb, nt = b // bb, l // tl
    assert nt == 1 or tl >= BAND_PAST
    rows = bb * tl
    keep = min(BAND_PAST, l)
    assert keep == tl or (keep == BAND_PAST and tl == BAND_PAST)
    kern = functools.partial(_attn_kernel, bb=bb, tl=tl, c=c, has_past=has_past)
    xmap = lambda ib, it: (ib, it, 0)
    bmap = lambda ib, it: (ib, 0, 0)
    rmap = lambda ib, it: (ib * nt + it, 0, 0)
    in_specs = [pl.BlockSpec((bb, tl, d), xmap), pl.BlockSpec((bb, 8, d), bmap)]
    args = [x, mod]
    if has_past:
        in_specs += [pl.BlockSpec((bb, BAND_PAST, d), bmap)] * 2
        args += [past_k, past_v]
    names = ["wkv_k", "wkv_v", "wq", "wo", "bias", "lng", "lnb", "wrt", "br"]
    in_specs += [_const_spec(w[n].shape) for n in names]
    args += [w[n] for n in names]
    outs = pl.pallas_call(
        kern,
        grid=(nb, nt),
        in_specs=in_specs,
        out_specs=[pl.BlockSpec((bb, tl, d), xmap),
                   pl.BlockSpec((bb, tl, d), xmap),
                   pl.BlockSpec((1, 8, rows), rmap),
                   pl.BlockSpec((1, 8, rows), rmap),
                   pl.BlockSpec((bb, tl, d), bmap),
                   pl.BlockSpec((bb, tl, d), bmap)],
        out_shape=[jax.ShapeDtypeStruct((b, l, d), F32),
                   jax.ShapeDtypeStruct((b, l, d), F32),
                   jax.ShapeDtypeStruct((nb * nt, 8, rows), jnp.int32),
                   jax.ShapeDtypeStruct((nb * nt, 8, rows), F32),
                   jax.ShapeDtypeStruct((b, keep, d), F32),
                   jax.ShapeDtypeStruct((b, keep, d), F32)],
        scratch_shapes=[pltpu.VMEM((bb, BAND_PAST + tl, d), BF16),
                        pltpu.VMEM((bb, BAND_PAST + tl, d), BF16),
                        pltpu.VMEM((rows, d), BF16),
                        pltpu.VMEM((rows, d), F32)],
        compiler_params=_params(("arbitrary", "arbitrary")),
        name="attn_layer",
    )(*args)
    return outs


def _moe_kernel(be_ref, nact_ref, st_ref, sr_ref, sg_ref, hf_hbm, wg_ref, bg_ref, wu_ref, bu_ref,
                wd_ref, bd_ref, y_hbm, xbuf, ybuf, wgb, wub, wdb, gsem, ssem, *, blk):
    i = pl.program_id(0)

    @pl.when(i == 0)
    def _():
        ybuf[...] = jnp.zeros(ybuf.shape, F32)
        spare = pltpu.make_async_copy(ybuf, y_hbm.at[pl.ds(y_hbm.shape[0] - blk, blk)], ssem)
        spare.start()
        spare.wait()

    @pl.when(i < nact_ref[0])
    def _():
        def issue_gather(j, carry):
            t = st_ref[0, 0, j]
            pltpu.make_async_copy(hf_hbm.at[pl.ds(t, 1)], xbuf.at[pl.ds(j, 1)], gsem).start()
            return carry

        lax.fori_loop(0, blk, issue_gather, 0)

        e = be_ref[i]
        prev = be_ref[jnp.maximum(i - 1, 0)]

        @pl.when((i == 0) | (e != prev))
        def _():
            wgb[...] = wg_ref[0].astype(BF16)
            wub[...] = wu_ref[0].astype(BF16)
            wdb[...] = wd_ref[0].astype(BF16)

        pltpu.make_async_copy(hf_hbm.at[pl.ds(0, blk)], xbuf, gsem).wait()
        x = xbuf[...].astype(BF16)
        a = jnp.minimum(_dot(x, wgb[...]) + bg_ref[0], SWIGLU_LIMIT)
        u = jnp.clip(_dot(x, wub[...]) + bu_ref[0], -SWIGLU_LIMIT, SWIGLU_LIMIT)
        hmid = ((u + 1.0) * (a * jax.nn.sigmoid(SWIGLU_ALPHA * a))).astype(BF16)
        ybuf[...] = (_dot(hmid, wdb[...]) + bd_ref[0]) * sg_ref[0]

        def issue_scatter(j, carry):
            r = sr_ref[0, 0, j]
            pltpu.make_async_copy(ybuf.at[pl.ds(j, 1)], y_hbm.at[pl.ds(r, 1)], ssem).start()
            return carry

        lax.fori_loop(0, blk, issue_scatter, 0)
        pltpu.make_async_copy(ybuf, y_hbm.at[pl.ds(0, blk)], ssem).wait()


def _moe_experts(hf2d, block_e, n_active, slot_t, slot_r, slot_g, wg, bg, wu, bu, wd, bd, blk):
    t, d = hf2d.shape
    f = wg.shape[-1]
    n_blocks = block_e.shape[0]
    kern = functools.partial(_moe_kernel, blk=blk)
    wmap = lambda i, be, na: (be[i], 0, 0)
    smap = lambda i, be, na: (i, 0, 0)
    grid_spec = pltpu.PrefetchScalarGridSpec(
        num_scalar_prefetch=2,
        grid=(n_blocks,),
        in_specs=[pl.BlockSpec((1, 1, blk), smap, memory_space=pltpu.SMEM),
                  pl.BlockSpec((1, 1, blk), smap, memory_space=pltpu.SMEM),
                  pl.BlockSpec((1, blk, 1), smap),
                  pl.BlockSpec(memory_space=pl.ANY),
                  pl.BlockSpec((1, d, f), wmap), pl.BlockSpec((1, 1, f), wmap),
                  pl.BlockSpec((1, d, f), wmap), pl.BlockSpec((1, 1, f), wmap),
                  pl.BlockSpec((1, f, d), wmap), pl.BlockSpec((1, 1, d), wmap)],
        out_specs=pl.BlockSpec(memory_space=pl.ANY),
        scratch_shapes=[pltpu.VMEM((blk, d), F32), pltpu.VMEM((blk, d), F32),
                        pltpu.VMEM((d, f), BF16), pltpu.VMEM((d, f), BF16), pltpu.VMEM((f, d), BF16),
                        pltpu.SemaphoreType.DMA, pltpu.SemaphoreType.DMA])
    return pl.pallas_call(
        kern,
        grid_spec=grid_spec,
        out_shape=jax.ShapeDtypeStruct((TOP_K * t + blk, d), F32),
        compiler_params=_params(("arbitrary",)),
        name="moe_experts",
    )(block_e, n_active, slot_t.reshape(n_blocks, 1, blk), slot_r.reshape(n_blocks, 1, blk),
      slot_g.reshape(n_blocks, blk, 1), hf2d, wg, bg.reshape(N_EXPERTS, 1, f), wu,
      bu.reshape(N_EXPERTS, 1, f), wd, bd.reshape(N_EXPERTS, 1, d))


def _route(ti, tg, blk):
    t = ti.shape[0]
    npair = t * TOP_K
    flat_e = ti.reshape(-1)
    order = jnp.argsort(flat_e).astype(jnp.int32)
    se = flat_e[order]
    counts = jnp.zeros((N_EXPERTS,), jnp.int32).at[flat_e].add(1)
    padded = (counts + blk - 1) // blk * blk
    pend = jnp.cumsum(padded)
    dest = (pend - padded)[se] + jnp.arange(npair, dtype=jnp.int32) - (jnp.cumsum(counts) - counts)[se]
    n_blocks = -(-npair // blk) + N_EXPERTS
    n_slots = n_blocks * blk
    slot_pair = jnp.full((n_slots,), -1, jnp.int32).at[dest].set(order)
    real = slot_pair >= 0
    sp = jnp.maximum(slot_pair, 0)
    slot_t = jnp.where(real, sp // TOP_K, 0)
    spare = npair + (jnp.arange(n_slots, dtype=jnp.int32) % blk)
    slot_r = jnp.where(real, (sp % TOP_K) * t + sp // TOP_K, spare)
    slot_g = jnp.where(real, tg.reshape(-1)[sp], 0.0)
    block_e = jnp.minimum(jnp.searchsorted(pend, jnp.arange(n_blocks, dtype=jnp.int32) * blk, side="right"),
                          N_EXPERTS - 1).astype(jnp.int32)
    n_active = (pend[-1] // blk).astype(jnp.int32).reshape(1)
    return block_e, n_active, slot_t, slot_r, slot_g


def _combine_kernel(x_ref, mod_ref, y0, y1, y2, y3, lng_ref, lnb_ref, o_ref):
    ff = (y0[...] + y1[...]) + (y2[...] + y3[...])
    g_f = mod_ref[0, 5:6, :]
    o_ref[0] = _layer_norm(DEEPNORM_ALPHA * x_ref[0] + g_f * ff, lng_ref[...], lnb_ref[...])


def _combine_layer(x1, mod, y, lng, lnb, tl):
    b, l, d = x1.shape
    nt = l // tl
    tb = (b * l) // tl
    ymaps = [functools.partial(lambda ib, it, k: (k * tb + ib * nt + it, 0), k=k) for k in range(TOP_K)]
    return pl.pallas_call(
        _combine_kernel,
        grid=(b, nt),
        in_specs=[pl.BlockSpec((1, tl, d), lambda ib, it: (ib, it, 0)),
                  pl.BlockSpec((1, 8, d), lambda ib, it: (ib, 0, 0))]
                 + [pl.BlockSpec((tl, d), m) for m in ymaps]
                 + [_const_spec(lng.shape), _const_spec(lnb.shape)],
        out_specs=pl.BlockSpec((1, tl, d), lambda ib, it: (ib, it, 0)),
        out_shape=jax.ShapeDtypeStruct((b, l, d), F32),
        compiler_params=_params(("arbitrary", "arbitrary")),
        name="moe_combine",
    )(x1, mod, y, y, y, y, lng, lnb)


def _flatten_router(ti, tg, nb, nt, bb, tl):
    def f(a):
        a = a[:, :TOP_K].reshape(nb, nt, TOP_K, bb, tl)
        return a.transpose(0, 3, 1, 4, 2).reshape(nb * bb * nt * tl, TOP_K)
    return f(ti), f(tg)


def _moe_layer(x1, hf, ti, tg, mod, moe_w, lng, lnb, blk, tl):
    b, l, d = x1.shape
    block_e, n_active, slot_t, slot_r, slot_g = _route(ti, tg, blk)
    y = _moe_experts(hf.reshape(b * l, d), block_e, n_active, slot_t, slot_r, slot_g, *moe_w, blk)
    return _combine_layer(x1, mod, y, lng, lnb, tl)


def _rel_bias_table(rel_bias, c):
    w = BAND_PAST + c
    dist = jnp.arange(c)[:, None] - jnp.arange(w)[None, :] + BAND_PAST
    return rel_bias[:, jnp.clip(dist, -REL_CLIP, REL_CLIP) + REL_CLIP].astype(F32)


def _trunk(x, mods, s0, past_k, past_v, wts, gla_bb, att_bb, blk):
    b, l, d = x.shape
    tl = min(TOKEN_TILE, l)
    c = min(CHUNK, l)
    row = lambda v: v.reshape(1, -1)

    def mod_of(layer):
        m = mods[layer].reshape(b, 6, d)
        return jnp.concatenate([m, jnp.zeros((b, 2, d), F32)], axis=1)

    def router_w(layer):
        return wts["moe_wr"][layer].T, wts["moe_br"][layer].reshape(N_EXPERTS, 1)

    def moe_w(layer):
        return (wts["moe_wg"][layer], wts["moe_bg"][layer], wts["moe_wu"][layer], wts["moe_bu"][layer],
                wts["moe_wd"][layer], wts["moe_bd"][layer])

    wrt, br = router_w(0)
    gw = dict(wq=wts["gla_wq"][0].astype(BF16), wk=wts["gla_wk"][0].astype(BF16),
              wv=wts["gla_wv"][0].astype(BF16), wr=wts["gla_wr"][0].astype(BF16),
              wa1=wts["gla_wa1"][0].astype(BF16), wa2=wts["gla_wa2"][0].astype(BF16),
              ba=row(wts["gla_ba"][0]), ng=row(wts["gla_norm_g"][0]),
              wo=wts["gla_wo"][0].astype(BF16), lng=row(wts["ln_g"][0, 0]), lnb=row(wts["ln_b"][0, 0]),
              wrt=wrt, br=br)
    mod0 = mod_of(0)
    s0t = jnp.swapaxes(s0, -1, -2)
    x1, hf, ti, tg, sfin_t = _gla_layer(x, mod0, s0t, gw, gla_bb, tl)
    ti, tg = _flatten_router(ti, tg, b // gla_bb, l // tl, gla_bb, tl)
    x = _moe_layer(x1, hf, ti, tg, mod0, moe_w(0), row(wts["ln_g"][0, 1]), row(wts["ln_b"][0, 1]), blk, tl)
    s_fin = jnp.swapaxes(sfin_t, -1, -2)[None]

    wrt, br = router_w(1)
    aw = dict(wkv_k=wts["kv_wk"].astype(BF16), wkv_v=wts["kv_wv"].astype(BF16),
              wq=wts["att_wq"][0].astype(BF16), wo=wts["att_wo"][0].astype(BF16),
              bias=_rel_bias_table(wts["att_rel_bias"][0], c),
              lng=row(wts["ln_g"][1, 0]), lnb=row(wts["ln_b"][1, 0]), wrt=wrt, br=br)
    mod1 = mod_of(1)
    x1, hf, ti, tg, k_new, v_new = _attn_layer(x, mod1, past_k, past_v, aw, att_bb, tl)
    ti, tg = _flatten_router(ti, tg, b // att_bb, l // tl, att_bb, tl)
    x = _moe_layer(x1, hf, ti, tg, mod1, moe_w(1), row(wts["ln_g"][1, 1]), row(wts["ln_b"][1, 1]), blk, tl)
    keep = k_new.shape[1]
    hd = d // ATT_HEADS
    return (x, s_fin, k_new.reshape(b, keep, ATT_HEADS, hd), v_new.reshape(b, keep, ATT_HEADS, hd))


def kernel(x_prompt, x_sample, c_prompt, c_sample, state_gla, cache_k, cache_v, ada_w, ada_b, ln_g, ln_b, gla_wq, gla_wk, gla_wv, gla_wa1, gla_wa2, gla_ba, gla_wr, gla_norm_g, gla_wo, kv_wk, kv_wv, att_wq, att_wo, att_rel_bias, moe_wr, moe_br, moe_wg, moe_bg, moe_wu, moe_bu, moe_wd, moe_bd):
    wts = dict(ln_g=ln_g, ln_b=ln_b, gla_wq=gla_wq, gla_wk=gla_wk, gla_wv=gla_wv, gla_wa1=gla_wa1,
               gla_wa2=gla_wa2, gla_ba=gla_ba, gla_wr=gla_wr, gla_norm_g=gla_norm_g, gla_wo=gla_wo,
               kv_wk=kv_wk, kv_wv=kv_wv, att_wq=att_wq, att_wo=att_wo, att_rel_bias=att_rel_bias,
               moe_wr=moe_wr, moe_br=moe_br, moe_wg=moe_wg, moe_bg=moe_bg, moe_wu=moe_wu,
               moe_bu=moe_bu, moe_wd=moe_wd, moe_bd=moe_bd)
    b_p, l_p, d = x_prompt.shape
    b_s, l_s, _ = x_sample.shape
    assert cache_k.shape[1] == BAND_PAST
    mods = _ada_modulation(jnp.concatenate([c_prompt, c_sample], axis=0), ada_w, ada_b)
    mods_p, mods_s = mods[:, :b_p], mods[:, b_p:]

    s0_p = jnp.zeros((b_p,) + state_gla.shape[2:], F32)
    blk_p = MOE_BLOCK_LARGE if b_p * l_p * TOP_K >= 64 * MOE_BLOCK_LARGE else MOE_BLOCK_SMALL
    y_p, s_p, k_p, v_p = _trunk(x_prompt, mods_p, s0_p, None, None, wts, 1, 1, blk_p)

    pk = cache_k.reshape(b_s, BAND_PAST, d).astype(BF16)
    pv = cache_v.reshape(b_s, BAND_PAST, d).astype(BF16)
    blk_s = MOE_BLOCK_LARGE if b_s * l_s * TOP_K >= 64 * MOE_BLOCK_LARGE else MOE_BLOCK_SMALL
    y_s, s_s, k_s, v_s = _trunk(x_sample, mods_s, state_gla[0], pk, pv, wts, b_s, min(4, b_s), blk_s)
    return (y_p, y_s, s_p, k_p, v_p, s_s, k_s, v_s)
```

```python
import functools
import math

import jax
import jax.numpy as jnp
from jax import lax
from jax.experimental import pallas as pl
from jax.experimental.pallas import tpu as pltpu

F32 = jnp.float32
BF16 = jnp.bfloat16

CHUNK = 64
GLA_HEADS = 4
GLA_TAU = 16.0
ATT_HEADS = 16
BAND_PAST = 512
REL_CLIP = 128
N_EXPERTS = 32
TOP_K = 4
SWIGLU_LIMIT = 7.0
SWIGLU_ALPHA = 1.702
DEPTH = 2
DEEPNORM_ALPHA = (2.0 * DEPTH) ** 0.25
LN_EPS = 1e-5
NEG_INF = -1e30

V7X_VMEM_LIMIT_BYTES = 56 * 1024 * 1024
TOKEN_TILE = 512
GLA_SUB = 16
MOE_BLOCK_LARGE = 512
MOE_BLOCK_SMALL = 64
LANES = 128


def _dot(a, b):
    return jnp.dot(a, b, preferred_element_type=F32)


def _dot_nt(a, b):
    return lax.dot_general(a, b, (((1,), (1,)), ((), ())), preferred_element_type=F32)


def _dot_tn(a, b):
    return lax.dot_general(a, b, (((0,), (0,)), ((), ())), preferred_element_type=F32)


def _layer_norm(y, g, b):
    mu = jnp.mean(y, axis=-1, keepdims=True)
    yc = y - mu
    var = jnp.mean(yc * yc, axis=-1, keepdims=True)
    return yc * lax.rsqrt(var + LN_EPS) * g + b


def _const_spec(shape):
    nd = len(shape)
    return pl.BlockSpec(shape, lambda *_: (0,) * nd, pipeline_mode=pl.Buffered(1))


def _params(sem):
    return pltpu.CompilerParams(dimension_semantics=sem, vmem_limit_bytes=V7X_VMEM_LIMIT_BYTES)


def _ada_kernel(c_ref, w_ref, b_ref, o_ref):
    c = c_ref[...]
    s = c * jax.nn.sigmoid(c)
    o_ref[0] = jnp.dot(s, w_ref[0], preferred_element_type=F32,
                       precision=lax.Precision.HIGHEST) + b_ref[0]


def _ada_modulation(c, ada_w, ada_b):
    n, d = c.shape
    depth, _, d6 = ada_w.shape
    tn = d6 // 4
    return pl.pallas_call(
        _ada_kernel,
        grid=(depth, d6 // tn),
        in_specs=[pl.BlockSpec((n, d), lambda l, j: (0, 0)),
                  pl.BlockSpec((1, d, tn), lambda l, j: (l, 0, j)),
                  pl.BlockSpec((1, 1, tn), lambda l, j: (l, 0, j))],
        out_specs=pl.BlockSpec((1, n, tn), lambda l, j: (l, 0, j)),
        out_shape=jax.ShapeDtypeStruct((depth, n, d6), F32),
        compiler_params=_params(("arbitrary", "arbitrary")),
        name="ada_modulation",
    )(c, ada_w, ada_b.reshape(depth, 1, d6))


def _post_mixer(x3, mix3, g_m, sh_f, sc_f, lng, lnb, wrt, br):
    bb, tl, d = x3.shape
    rows = bb * tl
    x1 = _layer_norm(DEEPNORM_ALPHA * x3 + g_m * mix3, lng, lnb)
    hf = x1 * (1.0 + sc_f) + sh_f
    logits = lax.dot_general(wrt, hf.reshape(rows, d), (((1,), (1,)), ((), ())),
                             preferred_element_type=F32, precision=lax.Precision.HIGHEST) + br
    eidx = lax.broadcasted_iota(jnp.int32, logits.shape, 0).astype(F32)
    vals, idxs = [], []
    cur = logits
    for _ in range(TOP_K):
        m = jnp.max(cur, axis=0, keepdims=True)
        sel = jnp.min(jnp.where(cur == m, eidx, float(N_EXPERTS)), axis=0, keepdims=True)
        vals.append(m)
        idxs.append(sel)
        cur = jnp.where(eidx == sel, -jnp.inf, cur)
    ex = [jnp.exp(v - vals[0]) for v in vals]
    den = (ex[0] + ex[1]) + (ex[2] + ex[3])
    out_row = lax.broadcasted_iota(jnp.int32, (8, rows), 0)
    ti = jnp.zeros((8, rows), F32)
    tg = jnp.zeros((8, rows), F32)
    for k in range(TOP_K):
        ti = jnp.where(out_row == k, idxs[k], ti)
        tg = jnp.where(out_row == k, ex[k] / den, tg)
    return x1, hf, ti.astype(jnp.int32), tg


def _gla_kernel(x_ref, mod_ref, s0_ref, wq_ref, wk_ref, wv_ref, wr_ref, wa1_ref, wa2_ref, ba_ref,
                ng_ref, wo_ref, lng_ref, lnb_ref, wrt_ref, br_ref,
                x1_ref, hf_ref, ti_ref, tg_ref, sfin_ref,
                st_s, q_s, k_s, v_s, la_s, o_s, *, bb, tl, c):
    i = pl.program_id(1)
    d = x_ref.shape[-1]
    dk = q_s.shape[-1] // GLA_HEADS
    dv = v_s.shape[-1] // GLA_HEADS
    rows = bb * tl
    sub = min(GLA_SUB, c)

    @pl.when(i == 0)
    def _():
        st_s[...] = s0_ref[...]

    x3 = x_ref[...]
    mod = mod_ref[...]
    h = (x3 * (1.0 + mod[:, 1:2, :]) + mod[:, 0:1, :]).reshape(rows, d)
    hb = h.astype(BF16)
    q_s[...] = _dot(hb, wq_ref[...]) * (dk ** -0.5)
    k_s[...] = _dot(hb, wk_ref[...])
    v_s[...] = _dot(hb, wv_ref[...])
    a = _dot(_dot(hb, wa1_ref[...]).astype(BF16), wa2_ref[...]) + ba_ref[...]
    la_s[...] = (jnp.minimum(a, 0.0) - jnp.log1p(jnp.exp(-jnp.abs(a)))) * (1.0 / GLA_TAU)

    tril = (lax.broadcasted_iota(jnp.int32, (c, c), 0) >=
            lax.broadcasted_iota(jnp.int32, (c, c), 1)).astype(BF16)
    lane_c = lax.broadcasted_iota(jnp.int32, (sub, c), 1)
    row_c = lax.broadcasted_iota(jnp.int32, (sub, c), 0)
    krow = lax.broadcasted_iota(jnp.int32, (c, dk), 0)
    ng = ng_ref[...]
    chunks_per_seq = tl // c

    def chunk_body(n, carry):
        r0 = pl.multiple_of(n * c, c)
        bi = n // chunks_per_seq
        la = la_s[pl.ds(r0, c), :]
        la_hi = la.astype(BF16)
        la_lo = (la - la_hi.astype(F32)).astype(BF16)
        bcum = _dot(tril, la_hi) + _dot(tril, la_lo)
        for hd in range(GLA_HEADS):
            qh = q_s[pl.ds(r0, c), hd * dk:(hd + 1) * dk]
            kh = k_s[pl.ds(r0, c), hd * dk:(hd + 1) * dk]
            vh = v_s[pl.ds(r0, c), hd * dv:(hd + 1) * dv].astype(BF16)
            bh = bcum[:, hd * dk:(hd + 1) * dk]
            blocks = []
            for sb in range(c // sub):
                lo = sb * sub
                qi = qh[lo:lo + sub]
                bi_rows = bh[lo:lo + sub]
                if sb > 0:
                    ref_b = bh[lo:lo + 1]
                    qt = (qi * jnp.exp(bi_rows - ref_b)).astype(BF16)
                    kt = jnp.where(krow < lo, kh * jnp.exp(jnp.minimum(ref_b - bh, 0.0)), 0.0)
                    acc = _dot_nt(qt, kt.astype(BF16))
                else:
                    acc = jnp.zeros((sub, c), F32)
                for s in range(sub):
                    g = lo + s
                    e = jnp.exp(jnp.minimum(bi_rows - bh[g:g + 1], 0.0))
                    col = jnp.sum(qi * e * kh[g:g + 1], axis=-1, keepdims=True)
                    acc = jnp.where((lane_c == g) & (row_c >= s), col, acc)
                blocks.append(acc)
            amat = blocks[0] if len(blocks) == 1 else jnp.concatenate(blocks, axis=0)
            st = st_s[bi, hd]
            o = _dot(amat.astype(BF16), vh) + _dot_nt((qh * jnp.exp(bh)).astype(BF16), st.astype(BF16))
            ms = jnp.mean(o * o, axis=-1, keepdims=True)
            o_s[pl.ds(r0, c), hd * dv:(hd + 1) * dv] = o * lax.rsqrt(ms + LN_EPS) * ng
            bl = bh[c - 1:c]
            kd = (kh * jnp.exp(bl - bh)).astype(BF16)
            st_s[bi, hd] = st * jnp.exp(bl) + _dot_tn(vh, kd)
        return carry

    lax.fori_loop(0, rows // c, chunk_body, 0)

    r = _dot(hb, wr_ref[...])
    og = (o_s[...] * (r * jax.nn.sigmoid(r))).astype(BF16)
    mix = _dot(og, wo_ref[...]).reshape(bb, tl, d)
    x1, hf, ti, tg = _post_mixer(x3, mix, mod[:, 2:3, :], mod[:, 3:4, :], mod[:, 4:5, :],
                                 lng_ref[...], lnb_ref[...], wrt_ref[...], br_ref[...])
    x1_ref[...] = x1
    hf_ref[...] = hf
    ti_ref[0] = ti
    tg_ref[0] = tg

    @pl.when(i == pl.num_programs(1) - 1)
    def _():
        sfin_ref[...] = st_s[...]


def _gla_layer(x, mod, s0t, w, bb, tl):
    b, l, d = x.shape
    c = min(CHUNK, l)
    hk = w["wq"].shape[1]
    hv = w["wv"].shape[1]
    nb, nt = b // bb, l // tl
    rows = bb * tl
    dv, dk = s0t.shape[2], s0t.shape[3]
    kern = functools.partial(_gla_kernel, bb=bb, tl=tl, c=c)
    xmap = lambda ib, it: (ib, it, 0)
    bmap = lambda ib, it: (ib, 0, 0)
    smap = lambda ib, it: (ib, 0, 0, 0)
    rmap = lambda ib, it: (ib * nt + it, 0, 0)
    outs = pl.pallas_call(
        kern,
        grid=(nb, nt),
        in_specs=[pl.BlockSpec((bb, tl, d), xmap),
                  pl.BlockSpec((bb, 8, d), bmap),
                  pl.BlockSpec((bb, GLA_HEADS, dv, dk), smap),
                  _const_spec(w["wq"].shape), _const_spec(w["wk"].shape), _const_spec(w["wv"].shape),
                  _const_spec(w["wr"].shape), _const_spec(w["wa1"].shape), _const_spec(w["wa2"].shape),
                  _const_spec(w["ba"].shape), _const_spec(w["ng"].shape), _const_spec(w["wo"].shape),
                  _const_spec(w["lng"].shape), _const_spec(w["lnb"].shape),
                  _const_spec(w["wrt"].shape), _const_spec(w["br"].shape)],
        out_specs=[pl.BlockSpec((bb, tl, d), xmap),
                   pl.BlockSpec((bb, tl, d), xmap),
                   pl.BlockSpec((1, 8, rows), rmap),
                   pl.BlockSpec((1, 8, rows), rmap),
                   pl.BlockSpec((bb, GLA_HEADS, dv, dk), smap)],
        out_shape=[jax.ShapeDtypeStruct((b, l, d), F32),
                   jax.ShapeDtypeStruct((b, l, d), F32),
                   jax.ShapeDtypeStruct((nb * nt, 8, rows), jnp.int32),
                   jax.ShapeDtypeStruct((nb * nt, 8, rows), F32),
                   jax.ShapeDtypeStruct(s0t.shape, F32)],
        scratch_shapes=[pltpu.VMEM((bb, GLA_HEADS, dv, dk), F32),
                        pltpu.VMEM((rows, hk), F32), pltpu.VMEM((rows, hk), F32),
                        pltpu.VMEM((rows, hv), F32), pltpu.VMEM((rows, hk), F32),
                        pltpu.VMEM((rows, hv), F32)],
        compiler_params=_params(("arbitrary", "arbitrary")),
        name="gla_layer",
    )(x, mod, s0t, w["wq"], w["wk"], w["wv"], w["wr"], w["wa1"], w["wa2"], w["ba"], w["ng"],
      w["wo"], w["lng"], w["lnb"], w["wrt"], w["br"])
    return outs


def _attn_kernel(*refs, bb, tl, c, has_past):
    if has_past:
        (x_ref, mod_ref, pk_ref, pv_ref, wkv_k_ref, wkv_v_ref, wq_ref, wo_ref, bias_ref, lng_ref,
         lnb_ref, wrt_ref, br_ref, x1_ref, hf_ref, ti_ref, tg_ref, ko_ref, vo_ref,
         kw_s, vw_s, q_s, o_s, sc_s, m_s, e_s) = refs
    else:
        (x_ref, mod_ref, wkv_k_ref, wkv_v_ref, wq_ref, wo_ref, bias_ref, lng_ref,
         lnb_ref, wrt_ref, br_ref, x1_ref, hf_ref, ti_ref, tg_ref, ko_ref, vo_ref,
         kw_s, vw_s, q_s, o_s, sc_s, m_s, e_s) = refs
    i = pl.program_id(1)
    d = x_ref.shape[-1]
    p = BAND_PAST
    w = p + c
    rows = bb * tl
    hd2 = 2 * (d // ATT_HEADS)

    @pl.when(i == 0)
    def _():
        if has_past:
            kw_s[:, 0:p, :] = pk_ref[...]
            vw_s[:, 0:p, :] = pv_ref[...]
        else:
            kw_s[:, 0:p, :] = jnp.zeros((bb, p, d), BF16)
            vw_s[:, 0:p, :] = jnp.zeros((bb, p, d), BF16)

    x3 = x_ref[...]
    mod = mod_ref[...]
    xb = x3.reshape(rows, d).astype(BF16)
    hb = (x3 * (1.0 + mod[:, 1:2, :]) + mod[:, 0:1, :]).reshape(rows, d).astype(BF16)
    kn = _dot(xb, wkv_k_ref[...])
    vn = _dot(xb, wkv_v_ref[...])
    ko_ref[...] = kn.reshape(bb, tl, d)
    vo_ref[...] = vn.reshape(bb, tl, d)
    kw_s[:, p:p + tl, :] = kn.astype(BF16).reshape(bb, tl, d)
    vw_s[:, p:p + tl, :] = vn.astype(BF16).reshape(bb, tl, d)
    q_s[...] = (_dot(hb, wq_ref[...]) * ((d // ATT_HEADS) ** -0.5)).astype(BF16)

    lane = lax.broadcasted_iota(jnp.int32, (c, hd2), 1)
    low = lane < (hd2 // 2)
    kpos = lax.broadcasted_iota(jnp.int32, (1, w), 1)
    chunks_per_seq = tl // c

    def chunk_body(n, carry):
        r0 = pl.multiple_of(n * c, c)
        bi = n // chunks_per_seq
        ci = n - bi * chunks_per_seq
        w0 = pl.multiple_of(ci * c, c)
        if not has_past:
            valid = (kpos + (i * tl + ci * c)) >= p
        for hp in range(ATT_HEADS // 2):
            qp = q_s[pl.ds(r0, c), hp * hd2:(hp + 1) * hd2]
            kwin = kw_s[bi, pl.ds(w0, w), hp * hd2:(hp + 1) * hd2]
            for half in range(2):
                qm = jnp.where(low if half == 0 else ~low, qp, jnp.zeros_like(qp))
                s = _dot_nt(qm, kwin) + bias_ref[2 * hp + half]
                if not has_past:
                    s = jnp.where(valid, s, NEG_INF)
                sc_s[2 * hp + half] = s
        for hh in range(ATT_HEADS):
            m_s[hh] = jnp.broadcast_to(jnp.max(sc_s[hh], axis=-1, keepdims=True), (c, LANES))
        for hh in range(ATT_HEADS):
            s = sc_s[hh]
            m = m_s[hh]
            parts = [jnp.exp(s[:, j * LANES:(j + 1) * LANES] - m) for j in range(w // LANES)]
            if w % LANES:
                parts.append(jnp.exp(s[:, w - w % LANES:] - m[:, :w % LANES]))
            e_s[hh] = jnp.concatenate(parts, axis=1).astype(BF16)
        ones = jnp.ones((w, hd2), BF16)
        for hp in range(ATT_HEADS // 2):
            vext = jnp.concatenate([vw_s[bi, pl.ds(w0, w), hp * hd2:(hp + 1) * hd2], ones], axis=1)
            outs = []
            for half in range(2):
                r = _dot(e_s[2 * hp + half], vext)
                outs.append(r[:, :hd2] / r[:, hd2:])
            o_s[pl.ds(r0, c), hp * hd2:(hp + 1) * hd2] = jnp.where(low, outs[0], outs[1])
        return carry

    lax.fori_loop(0, rows // c, chunk_body, 0)

    if tl >= p:
        kw_s[:, 0:p, :] = kw_s[:, tl:tl + p, :]
        vw_s[:, 0:p, :] = vw_s[:, tl:tl + p, :]

    mix = _dot(o_s[...].astype(BF16), wo_ref[...]).reshape(bb, tl, d)
    x1, hf, ti, tg = _post_mixer(x3, mix, mod[:, 2:3, :], mod[:, 3:4, :], mod[:, 4:5, :],
                                 lng_ref[...], lnb_ref[...], wrt_ref[...], br_ref[...])
    x1_ref[...] = x1
    hf_ref[...] = hf
    ti_ref[0] = ti
    tg_ref[0] = tg


def _attn_layer(x, mod, past_k, past_v, w, bb, tl):
    b, l, d = x.shape
    c = min(CHUNK, l)
    has_past = past_k is not None
    nb, nt = b // bb, l // tl
    assert nt == 1 or tl >= BAND_PAST
    rows = bb * tl
    keep = min(BAND_PAST, l)
    assert keep == tl or (keep == BAND_PAST and tl == BAND_PAST)
    kern = functools.partial(_attn_kernel, bb=bb, tl=tl, c=c, has_past=has_past)
    xmap = lambda ib, it: (ib, it, 0)
    bmap = lambda ib, it: (ib, 0, 0)
    rmap = lambda ib, it: (ib * nt + it, 0, 0)
    in_specs = [pl.BlockSpec((bb, tl, d), xmap), pl.BlockSpec((bb, 8, d), bmap)]
    args = [x, mod]
    if has_past:
        in_specs += [pl.BlockSpec((bb, BAND_PAST, d), bmap)] * 2
        args += [past_k, past_v]
    names = ["wkv_k", "wkv_v", "wq", "wo", "bias", "lng", "lnb", "wrt", "br"]
    in_specs += [_const_spec(w[n].shape) for n in names]
    args += [w[n] for n in names]
    outs = pl.pallas_call(
        kern,
        grid=(nb, nt),
        in_specs=in_specs,
        out_specs=[pl.BlockSpec((bb, tl, d), xmap),
                   pl.BlockSpec((bb, tl, d), xmap),
                   pl.BlockSpec((1, 8, rows), rmap),
                   pl.BlockSpec((1, 8, rows), rmap),
                   pl.BlockSpec((bb, tl, d), bmap),
                   pl.BlockSpec((bb, tl, d), bmap)],
        out_shape=[jax.ShapeDtypeStruct((b, l, d), F32),
                   jax.ShapeDtypeStruct((b, l, d), F32),
                   jax.ShapeDtypeStruct((nb * nt, 8, rows), jnp.int32),
                   jax.ShapeDtypeStruct((nb * nt, 8, rows), F32),
                   jax.ShapeDtypeStruct((b, keep, d), F32),
                   jax.ShapeDtypeStruct((b, keep, d), F32)],
        scratch_shapes=[pltpu.VMEM((bb, BAND_PAST + tl, d), BF16),
                        pltpu.VMEM((bb, BAND_PAST + tl, d), BF16),
                        pltpu.VMEM((rows, d), BF16),
                        pltpu.VMEM((rows, d), F32),
                        pltpu.VMEM((ATT_HEADS, c, BAND_PAST + c), F32),
                        pltpu.VMEM((ATT_HEADS, c, LANES), F32),
                        pltpu.VMEM((ATT_HEADS, c, BAND_PAST + c), BF16)],
        compiler_params=_params(("arbitrary", "arbitrary")),
        name="attn_layer",
    )(*args)
    return outs


def _moe_kernel(be_ref, nact_ref, st_ref, stn_ref, sr_ref, sg_ref, hf_hbm, wg_ref, bg_ref, wu_ref, bu_ref,
                wd_ref, bd_ref, y_hbm, xbuf, ybuf, wgb, wub, wdb, gsem, ssem, *, blk):
    i = pl.program_id(0)
    nact = nact_ref[0]

    def gather_row(tok_ref, j, slot):
        return pltpu.make_async_copy(hf_hbm.at[pl.ds(tok_ref[0, 0, j], 1)],
                                     xbuf.at[slot, pl.ds(j, 1)], gsem.at[slot])

    def gather_block(slot):
        return pltpu.make_async_copy(hf_hbm.at[pl.ds(0, blk)], xbuf.at[slot], gsem.at[slot])

    def scatter_block():
        return pltpu.make_async_copy(ybuf, y_hbm.at[pl.ds(y_hbm.shape[0] - blk, blk)], ssem)

    @pl.when(i == 0)
    def _():
        ybuf[...] = jnp.zeros(ybuf.shape, F32)
        scatter_block().start()

        def prime(j, carry):
            gather_row(st_ref, j, 0).start()
            return carry

        lax.fori_loop(0, blk, prime, 0)

    @pl.when(i < nact)
    def _():
        slot = lax.rem(i, 2)
        e = be_ref[i]
        prev = be_ref[jnp.maximum(i - 1, 0)]

        @pl.when((i == 0) | (e != prev))
        def _():
            wgb[...] = wg_ref[0].astype(BF16)
            wub[...] = wu_ref[0].astype(BF16)
            wdb[...] = wd_ref[0].astype(BF16)

        gather_block(slot).wait()
        for j in range(blk):
            gather_row(stn_ref, j, 1 - slot).start()
        x = xbuf[slot].astype(BF16)
        a = jnp.minimum(_dot(x, wgb[...]) + bg_ref[0], SWIGLU_LIMIT)
        u = jnp.clip(_dot(x, wub[...]) + bu_ref[0], -SWIGLU_LIMIT, SWIGLU_LIMIT)
        hmid = ((u + 1.0) * (a * jax.nn.sigmoid(SWIGLU_ALPHA * a))).astype(BF16)
        y = (_dot(hmid, wdb[...]) + bd_ref[0]) * sg_ref[0]
        scatter_block().wait()
        ybuf[...] = y
        for j in range(blk):
            pltpu.make_async_copy(ybuf.at[pl.ds(j, 1)], y_hbm.at[pl.ds(sr_ref[0, 0, j], 1)], ssem).start()

        @pl.when(i == nact - 1)
        def _():
            scatter_block().wait()
            gather_block(1 - slot).wait()


def _moe_experts(hf2d, block_e, n_active, slot_t, slot_r, slot_g, wg, bg, wu, bu, wd, bd, blk):
    t, d = hf2d.shape
    f = wg.shape[-1]
    n_blocks = block_e.shape[0]
    kern = functools.partial(_moe_kernel, blk=blk)
    wmap = lambda i, be, na: (be[i], 0, 0)
    smap = lambda i, be, na: (i, 0, 0)
    nmap = lambda i, be, na: (jnp.minimum(i + 1, n_blocks - 1), 0, 0)
    grid_spec = pltpu.PrefetchScalarGridSpec(
        num_scalar_prefetch=2,
        grid=(n_blocks,),
        in_specs=[pl.BlockSpec((1, 1, blk), smap, memory_space=pltpu.SMEM),
                  pl.BlockSpec((1, 1, blk), nmap, memory_space=pltpu.SMEM),
                  pl.BlockSpec((1, 1, blk), smap, memory_space=pltpu.SMEM),
                  pl.BlockSpec((1, blk, 1), smap),
                  pl.BlockSpec(memory_space=pl.ANY),
                  pl.BlockSpec((1, d, f), wmap), pl.BlockSpec((1, 1, f), wmap),
                  pl.BlockSpec((1, d, f), wmap), pl.BlockSpec((1, 1, f), wmap),
                  pl.BlockSpec((1, f, d), wmap), pl.BlockSpec((1, 1, d), wmap)],
        out_specs=pl.BlockSpec(memory_space=pl.ANY),
        scratch_shapes=[pltpu.VMEM((2, blk, d), F32), pltpu.VMEM((blk, d), F32),
                        pltpu.VMEM((d, f), BF16), pltpu.VMEM((d, f), BF16), pltpu.VMEM((f, d), BF16),
                        pltpu.SemaphoreType.DMA((2,)), pltpu.SemaphoreType.DMA])
    st3 = slot_t.reshape(n_blocks, 1, blk)
    return pl.pallas_call(
        kern,
        grid_spec=grid_spec,
        out_shape=jax.ShapeDtypeStruct((TOP_K * t + blk, d), F32),
        compiler_params=_params(("arbitrary",)),
        name="moe_experts",
    )(block_e, n_active, st3, st3, slot_r.reshape(n_blocks, 1, blk),
      slot_g.reshape(n_blocks, blk, 1), hf2d, wg, bg.reshape(N_EXPERTS, 1, f), wu,
      bu.reshape(N_EXPERTS, 1, f), wd, bd.reshape(N_EXPERTS, 1, d))


def _route(ti, tg, blk):
    t = ti.shape[0]
    npair = t * TOP_K
    flat_e = ti.reshape(-1)
    order = jnp.argsort(flat_e).astype(jnp.int32)
    experts = jnp.arange(N_EXPERTS, dtype=jnp.int32)
    counts = jnp.sum((flat_e[:, None] == experts[None, :]).astype(jnp.int32), axis=0)
    padded = (counts + blk - 1) // blk * blk
    pend = jnp.cumsum(padded)
    pstart = pend - padded
    cstart = jnp.cumsum(counts) - counts
    n_blocks = -(-npair // blk) + N_EXPERTS
    bstart = jnp.arange(n_blocks, dtype=jnp.int32) * blk
    block_e = jnp.minimum(jnp.sum((pend[None, :] <= bstart[:, None]).astype(jnp.int32), axis=1),
                          N_EXPERTS - 1)
    within = jnp.arange(blk, dtype=jnp.int32)[None, :]
    off = (bstart - pstart[block_e])[:, None] + within
    real = off < counts[block_e][:, None]
    spos = jnp.clip(cstart[block_e][:, None] + off, 0, npair - 1)
    sp = order[spos]
    slot_t = jnp.where(real, sp // TOP_K, 0)
    slot_r = jnp.where(real, (sp % TOP_K) * t + sp // TOP_K, npair + within)
    slot_g = jnp.where(real, tg.reshape(-1)[sp], 0.0)
    n_active = (pend[-1] // blk).astype(jnp.int32).reshape(1)
    return block_e, n_active, slot_t, slot_r, slot_g


def _combine_kernel(x_ref, mod_ref, y0, y1, y2, y3, lng_ref, lnb_ref, o_ref):
    ff = (y0[...] + y1[...]) + (y2[...] + y3[...])
    g_f = mod_ref[0, 5:6, :]
    o_ref[0] = _layer_norm(DEEPNORM_ALPHA * x_ref[0] + g_f * ff, lng_ref[...], lnb_ref[...])


def _combine_layer(x1, mod, y, lng, lnb, tl):
    b, l, d = x1.shape
    nt = l // tl
    tb = (b * l) // tl
    ymaps = [functools.partial(lambda ib, it, k: (k * tb + ib * nt + it, 0), k=k) for k in range(TOP_K)]
    return pl.pallas_call(
        _combine_kernel,
        grid=(b, nt),
        in_specs=[pl.BlockSpec((1, tl, d), lambda ib, it: (ib, it, 0)),
                  pl.BlockSpec((1, 8, d), lambda ib, it: (ib, 0, 0))]
                 + [pl.BlockSpec((tl, d), m) for m in ymaps]
                 + [_const_spec(lng.shape), _const_spec(lnb.shape)],
        out_specs=pl.BlockSpec((1, tl, d), lambda ib, it: (ib, it, 0)),
        out_shape=jax.ShapeDtypeStruct((b, l, d), F32),
        compiler_params=_params(("arbitrary", "arbitrary")),
        name="moe_combine",
    )(x1, mod, y, y, y, y, lng, lnb)


def _flatten_router(ti, tg, nb, nt, bb, tl):
    def f(a):
        a = a[:, :TOP_K].reshape(nb, nt, TOP_K, bb, tl)
        return a.transpose(0, 3, 1, 4, 2).reshape(nb * bb * nt * tl, TOP_K)
    return f(ti), f(tg)


def _moe_layer(x1, hf, ti, tg, mod, moe_w, lng, lnb, blk, tl):
    b, l, d = x1.shape
    block_e, n_active, slot_t, slot_r, slot_g = _route(ti, tg, blk)
    y = _moe_experts(hf.reshape(b * l, d), block_e, n_active, slot_t, slot_r, slot_g, *moe_w, blk)
    return _combine_layer(x1, mod, y, lng, lnb, tl)


def _rel_bias_table(rel_bias, c):
    w = BAND_PAST + c
    dist = jnp.arange(c)[:, None] - jnp.arange(w)[None, :] + BAND_PAST
    return rel_bias[:, jnp.clip(dist, -REL_CLIP, REL_CLIP) + REL_CLIP].astype(F32)


def _trunk(x, mods, s0, past_k, past_v, wts, gla_bb, att_bb, blk):
    b, l, d = x.shape
    tl = min(TOKEN_TILE, l)
    c = min(CHUNK, l)
    row = lambda v: v.reshape(1, -1)

    def mod_of(layer):
        m = mods[layer].reshape(b, 6, d)
        return jnp.concatenate([m, jnp.zeros((b, 2, d), F32)], axis=1)

    def router_w(layer):
        return wts["moe_wr"][layer].T, wts["moe_br"][layer].reshape(N_EXPERTS, 1)

    def moe_w(layer):
        return (wts["moe_wg"][layer], wts["moe_bg"][layer], wts["moe_wu"][layer], wts["moe_bu"][layer],
                wts["moe_wd"][layer], wts["moe_bd"][layer])

    wrt, br = router_w(0)
    gw = dict(wq=wts["gla_wq"][0].astype(BF16), wk=wts["gla_wk"][0].astype(BF16),
              wv=wts["gla_wv"][0].astype(BF16), wr=wts["gla_wr"][0].astype(BF16),
              wa1=wts["gla_wa1"][0].astype(BF16), wa2=wts["gla_wa2"][0].astype(BF16),
              ba=row(wts["gla_ba"][0]), ng=row(wts["gla_norm_g"][0]),
              wo=wts["gla_wo"][0].astype(BF16), lng=row(wts["ln_g"][0, 0]), lnb=row(wts["ln_b"][0, 0]),
              wrt=wrt, br=br)
    mod0 = mod_of(0)
    s0t = jnp.swapaxes(s0, -1, -2)
    x1, hf, ti, tg, sfin_t = _gla_layer(x, mod0, s0t, gw, gla_bb, tl)
    ti, tg = _flatten_router(ti, tg, b // gla_bb, l // tl, gla_bb, tl)
    x = _moe_layer(x1, hf, ti, tg, mod0, moe_w(0), row(wts["ln_g"][0, 1]), row(wts["ln_b"][0, 1]), blk, tl)
    s_fin = jnp.swapaxes(sfin_t, -1, -2)[None]

    wrt, br = router_w(1)
    aw = dict(wkv_k=wts["kv_wk"].astype(BF16), wkv_v=wts["kv_wv"].astype(BF16),
              wq=wts["att_wq"][0].astype(BF16), wo=wts["att_wo"][0].astype(BF16),
              bias=_rel_bias_table(wts["att_rel_bias"][0], c),
              lng=row(wts["ln_g"][1, 0]), lnb=row(wts["ln_b"][1, 0]), wrt=wrt, br=br)
    mod1 = mod_of(1)
    x1, hf, ti, tg, k_new, v_new = _attn_layer(x, mod1, past_k, past_v, aw, att_bb, tl)
    ti, tg = _flatten_router(ti, tg, b // att_bb, l // tl, att_bb, tl)
    x = _moe_layer(x1, hf, ti, tg, mod1, moe_w(1), row(wts["ln_g"][1, 1]), row(wts["ln_b"][1, 1]), blk, tl)
    keep = k_new.shape[1]
    hd = d // ATT_HEADS
    return (x, s_fin, k_new.reshape(b, keep, ATT_HEADS, hd), v_new.reshape(b, keep, ATT_HEADS, hd))


def kernel(x_prompt, x_sample, c_prompt, c_sample, state_gla, cache_k, cache_v, ada_w, ada_b, ln_g, ln_b, gla_wq, gla_wk, gla_wv, gla_wa1, gla_wa2, gla_ba, gla_wr, gla_norm_g, gla_wo, kv_wk, kv_wv, att_wq, att_wo, att_rel_bias, moe_wr, moe_br, moe_wg, moe_bg, moe_wu, moe_bu, moe_wd, moe_bd):
    wts = dict(ln_g=ln_g, ln_b=ln_b, gla_wq=gla_wq, gla_wk=gla_wk, gla_wv=gla_wv, gla_wa1=gla_wa1,
               gla_wa2=gla_wa2, gla_ba=gla_ba, gla_wr=gla_wr, gla_norm_g=gla_norm_g, gla_wo=gla_wo,
               kv_wk=kv_wk, kv_wv=kv_wv, att_wq=att_wq, att_wo=att_wo, att_rel_bias=att_rel_bias,
               moe_wr=moe_wr, moe_br=moe_br, moe_wg=moe_wg, moe_bg=moe_bg, moe_wu=moe_wu,
               moe_bu=moe_bu, moe_wd=moe_wd, moe_bd=moe_bd)
    b_p, l_p, d = x_prompt.shape
    b_s, l_s, _ = x_sample.shape
    assert cache_k.shape[1] == BAND_PAST
    mods = _ada_modulation(jnp.concatenate([c_prompt, c_sample], axis=0), ada_w, ada_b)
    mods_p, mods_s = mods[:, :b_p], mods[:, b_p:]

    s0_p = jnp.zeros((b_p,) + state_gla.shape[2:], F32)
    blk_p = MOE_BLOCK_LARGE if b_p * l_p * TOP_K >= 64 * MOE_BLOCK_LARGE else MOE_BLOCK_SMALL
    y_p, s_p, k_p, v_p = _trunk(x_prompt, mods_p, s0_p, None, None, wts, 1, 1, blk_p)

    pk = cache_k.reshape(b_s, BAND_PAST, d).astype(BF16)
    pv = cache_v.reshape(b_s, BAND_PAST, d).astype(BF16)
    blk_s = MOE_BLOCK_LARGE if b_s * l_s * TOP_K >= 64 * MOE_BLOCK_LARGE else MOE_BLOCK_SMALL
    y_s, s_s, k_s, v_s = _trunk(x_sample, mods_s, state_gla[0], pk, pv, wts, b_s, min(4, b_s), blk_s)
    return (y_p, y_s, s_p, k_p, v_p, s_s, k_s, v_s)
```

```python
import functools
import math

import jax
import jax.numpy as jnp
from jax import lax
from jax.experimental import pallas as pl
from jax.experimental.pallas import tpu as pltpu

F32 = jnp.float32
BF16 = jnp.bfloat16

CHUNK = 64
GLA_HEADS = 4
GLA_TAU = 16.0
ATT_HEADS = 16
BAND_PAST = 512
REL_CLIP = 128
N_EXPERTS = 32
TOP_K = 4
SWIGLU_LIMIT = 7.0
SWIGLU_ALPHA = 1.702
DEPTH = 2
DEEPNORM_ALPHA = (2.0 * DEPTH) ** 0.25
LN_EPS = 1e-5
NEG_INF = -1e30

V7X_VMEM_LIMIT_BYTES = 56 * 1024 * 1024
TOKEN_TILE = 512
GLA_SUB = 16
MOE_BLOCK_LARGE = 512
MOE_BLOCK_SMALL = 64
LANES = 128


def _dot(a, b):
    return jnp.dot(a, b, preferred_element_type=F32)


def _dot_nt(a, b):
    return lax.dot_general(a, b, (((1,), (1,)), ((), ())), preferred_element_type=F32)


def _dot_tn(a, b):
    return lax.dot_general(a, b, (((0,), (0,)), ((), ())), preferred_element_type=F32)


def _layer_norm(y, g, b):
    mu = jnp.mean(y, axis=-1, keepdims=True)
    yc = y - mu
    var = jnp.mean(yc * yc, axis=-1, keepdims=True)
    return yc * lax.rsqrt(var + LN_EPS) * g + b


def _store_token_major(ref, val):
    rows, d = val.shape
    g = d // LANES
    for s in range(g):
        ref[pl.ds(s, rows, stride=g), :] = val[:, s * LANES:(s + 1) * LANES]


def _load_token_major(ref, rows, d):
    g = d // LANES
    return jnp.concatenate([ref[pl.ds(s, rows, stride=g), :] for s in range(g)], axis=1)


def _const_spec(shape):
    nd = len(shape)
    return pl.BlockSpec(shape, lambda *_: (0,) * nd, pipeline_mode=pl.Buffered(1))


def _params(sem):
    return pltpu.CompilerParams(dimension_semantics=sem, vmem_limit_bytes=V7X_VMEM_LIMIT_BYTES)


def _ada_kernel(c_ref, w_ref, b_ref, o_ref):
    c = c_ref[...]
    s = c * jax.nn.sigmoid(c)
    o_ref[0] = jnp.dot(s, w_ref[0], preferred_element_type=F32,
                       precision=lax.Precision.HIGHEST) + b_ref[0]


def _ada_modulation(c, ada_w, ada_b):
    n, d = c.shape
    depth, _, d6 = ada_w.shape
    tn = d6 // 4
    return pl.pallas_call(
        _ada_kernel,
        grid=(depth, d6 // tn),
        in_specs=[pl.BlockSpec((n, d), lambda l, j: (0, 0)),
                  pl.BlockSpec((1, d, tn), lambda l, j: (l, 0, j)),
                  pl.BlockSpec((1, 1, tn), lambda l, j: (l, 0, j))],
        out_specs=pl.BlockSpec((1, n, tn), lambda l, j: (l, 0, j)),
        out_shape=jax.ShapeDtypeStruct((depth, n, d6), F32),
        compiler_params=_params(("arbitrary", "arbitrary")),
        name="ada_modulation",
    )(c, ada_w, ada_b.reshape(depth, 1, d6))


def _post_mixer(x3, mix3, g_m, sh_f, sc_f, lng, lnb, wrt, br):
    bb, tl, d = x3.shape
    rows = bb * tl
    x1 = _layer_norm(DEEPNORM_ALPHA * x3 + g_m * mix3, lng, lnb)
    hf = x1 * (1.0 + sc_f) + sh_f
    logits = lax.dot_general(wrt, hf.reshape(rows, d), (((1,), (1,)), ((), ())),
                             preferred_element_type=F32, precision=lax.Precision.HIGHEST) + br
    eidx = lax.broadcasted_iota(jnp.int32, logits.shape, 0).astype(F32)
    vals, idxs = [], []
    cur = logits
    for _ in range(TOP_K):
        m = jnp.max(cur, axis=0, keepdims=True)
        sel = jnp.min(jnp.where(cur == m, eidx, float(N_EXPERTS)), axis=0, keepdims=True)
        vals.append(m)
        idxs.append(sel)
        cur = jnp.where(eidx == sel, -jnp.inf, cur)
    ex = [jnp.exp(v - vals[0]) for v in vals]
    den = (ex[0] + ex[1]) + (ex[2] + ex[3])
    out_row = lax.broadcasted_iota(jnp.int32, (8, rows), 0)
    ti = jnp.zeros((8, rows), F32)
    tg = jnp.zeros((8, rows), F32)
    for k in range(TOP_K):
        ti = jnp.where(out_row == k, idxs[k], ti)
        tg = jnp.where(out_row == k, ex[k] / den, tg)
    return x1, hf, ti.astype(jnp.int32), tg


def _gla_kernel(x_ref, mod_ref, s0_ref, wq_ref, wk_ref, wv_ref, wr_ref, wa1_ref, wa2_ref, ba_ref,
                ng_ref, wo_ref, lng_ref, lnb_ref, wrt_ref, br_ref,
                x1_ref, hf_ref, ti_ref, tg_ref, sfin_ref,
                st_s, q_s, k_s, v_s, la_s, o_s, *, bb, tl, c):
    i = pl.program_id(1)
    d = x_ref.shape[-1]
    dk = q_s.shape[-1] // GLA_HEADS
    dv = v_s.shape[-1] // GLA_HEADS
    rows = bb * tl
    sub = min(GLA_SUB, c)

    @pl.when(i == 0)
    def _():
        st_s[...] = s0_ref[...]

    x3 = x_ref[...]
    mod = mod_ref[...]
    h = (x3 * (1.0 + mod[:, 1:2, :]) + mod[:, 0:1, :]).reshape(rows, d)
    hb = h.astype(BF16)
    q_s[...] = _dot(hb, wq_ref[...]) * (dk ** -0.5)
    k_s[...] = _dot(hb, wk_ref[...])
    v_s[...] = _dot(hb, wv_ref[...])
    a = _dot(_dot(hb, wa1_ref[...]).astype(BF16), wa2_ref[...]) + ba_ref[...]
    la_s[...] = (jnp.minimum(a, 0.0) - jnp.log1p(jnp.exp(-jnp.abs(a)))) * (1.0 / GLA_TAU)

    tril = (lax.broadcasted_iota(jnp.int32, (c, c), 0) >=
            lax.broadcasted_iota(jnp.int32, (c, c), 1)).astype(BF16)
    lane_c = lax.broadcasted_iota(jnp.int32, (sub, c), 1)
    row_c = lax.broadcasted_iota(jnp.int32, (sub, c), 0)
    krow = lax.broadcasted_iota(jnp.int32, (c, dk), 0)
    ng = ng_ref[...]
    chunks_per_seq = tl // c

    def chunk_body(n, carry):
        r0 = pl.multiple_of(n * c, c)
        bi = n // chunks_per_seq
        la = la_s[pl.ds(r0, c), :]
        la_hi = la.astype(BF16)
        la_lo = (la - la_hi.astype(F32)).astype(BF16)
        bcum = _dot(tril, la_hi) + _dot(tril, la_lo)
        for hd in range(GLA_HEADS):
            qh = q_s[pl.ds(r0, c), hd * dk:(hd + 1) * dk]
            kh = k_s[pl.ds(r0, c), hd * dk:(hd + 1) * dk]
            vh = v_s[pl.ds(r0, c), hd * dv:(hd + 1) * dv].astype(BF16)
            bh = bcum[:, hd * dk:(hd + 1) * dk]
            blocks = []
            for sb in range(c // sub):
                lo = sb * sub
                qi = qh[lo:lo + sub]
                bi_rows = bh[lo:lo + sub]
                if sb > 0:
                    ref_b = bh[lo:lo + 1]
                    qt = (qi * jnp.exp(bi_rows - ref_b)).astype(BF16)
                    kt = jnp.where(krow < lo, kh * jnp.exp(jnp.minimum(ref_b - bh, 0.0)), 0.0)
                    acc = _dot_nt(qt, kt.astype(BF16))
                else:
                    acc = jnp.zeros((sub, c), F32)
                for s in range(sub):
                    g = lo + s
                    e = jnp.exp(jnp.minimum(bi_rows - bh[g:g + 1], 0.0))
                    col = jnp.sum(qi * e * kh[g:g + 1], axis=-1, keepdims=True)
                    acc = jnp.where((lane_c == g) & (row_c >= s), col, acc)
                blocks.append(acc)
            amat = blocks[0] if len(blocks) == 1 else jnp.concatenate(blocks, axis=0)
            st = st_s[bi, hd]
            o = _dot(amat.astype(BF16), vh) + _dot_nt((qh * jnp.exp(bh)).astype(BF16), st.astype(BF16))
            ms = jnp.mean(o * o, axis=-1, keepdims=True)
            o_s[pl.ds(r0, c), hd * dv:(hd + 1) * dv] = o * lax.rsqrt(ms + LN_EPS) * ng
            bl = bh[c - 1:c]
            kd = (kh * jnp.exp(bl - bh)).astype(BF16)
            st_s[bi, hd] = st * jnp.exp(bl) + _dot_tn(vh, kd)
        return carry

    lax.fori_loop(0, rows // c, chunk_body, 0)

    r = _dot(hb, wr_ref[...])
    og = (o_s[...] * (r * jax.nn.sigmoid(r))).astype(BF16)
    mix = _dot(og, wo_ref[...]).reshape(bb, tl, d)
    x1, hf, ti, tg = _post_mixer(x3, mix, mod[:, 2:3, :], mod[:, 3:4, :], mod[:, 4:5, :],
                                 lng_ref[...], lnb_ref[...], wrt_ref[...], br_ref[...])
    x1_ref[...] = x1
    _store_token_major(hf_ref, hf.reshape(rows, d))
    ti_ref[0] = ti
    tg_ref[0] = tg

    @pl.when(i == pl.num_programs(1) - 1)
    def _():
        sfin_ref[...] = st_s[...]


def _gla_layer(x, mod, s0t, w, bb, tl):
    b, l, d = x.shape
    c = min(CHUNK, l)
    hk = w["wq"].shape[1]
    hv = w["wv"].shape[1]
    nb, nt = b // bb, l // tl
    assert bb == 1 or nt == 1
    rows = bb * tl
    dv, dk = s0t.shape[2], s0t.shape[3]
    kern = functools.partial(_gla_kernel, bb=bb, tl=tl, c=c)
    xmap = lambda ib, it: (ib, it, 0)
    bmap = lambda ib, it: (ib, 0, 0)
    smap = lambda ib, it: (ib, 0, 0, 0)
    rmap = lambda ib, it: (ib * nt + it, 0, 0)
    outs = pl.pallas_call(
        kern,
        grid=(nb, nt),
        in_specs=[pl.BlockSpec((bb, tl, d), xmap),
                  pl.BlockSpec((bb, 8, d), bmap),
                  pl.BlockSpec((bb, GLA_HEADS, dv, dk), smap),
                  _const_spec(w["wq"].shape), _const_spec(w["wk"].shape), _const_spec(w["wv"].shape),
                  _const_spec(w["wr"].shape), _const_spec(w["wa1"].shape), _const_spec(w["wa2"].shape),
                  _const_spec(w["ba"].shape), _const_spec(w["ng"].shape), _const_spec(w["wo"].shape),
                  _const_spec(w["lng"].shape), _const_spec(w["lnb"].shape),
                  _const_spec(w["wrt"].shape), _const_spec(w["br"].shape)],
        out_specs=[pl.BlockSpec((bb, tl, d), xmap),
                   pl.BlockSpec((rows * (d // LANES), LANES), lambda ib, it: (ib * nt + it, 0)),
                   pl.BlockSpec((1, 8, rows), rmap),
                   pl.BlockSpec((1, 8, rows), rmap),
                   pl.BlockSpec((bb, GLA_HEADS, dv, dk), smap)],
        out_shape=[jax.ShapeDtypeStruct((b, l, d), F32),
                   jax.ShapeDtypeStruct((b * l * (d // LANES), LANES), F32),
                   jax.ShapeDtypeStruct((nb * nt, 8, rows), jnp.int32),
                   jax.ShapeDtypeStruct((nb * nt, 8, rows), F32),
                   jax.ShapeDtypeStruct(s0t.shape, F32)],
        scratch_shapes=[pltpu.VMEM((bb, GLA_HEADS, dv, dk), F32),
                        pltpu.VMEM((rows, hk), F32), pltpu.VMEM((rows, hk), F32),
                        pltpu.VMEM((rows, hv), F32), pltpu.VMEM((rows, hk), F32),
                        pltpu.VMEM((rows, hv), F32)],
        compiler_params=_params(("arbitrary", "arbitrary")),
        name="gla_layer",
    )(x, mod, s0t, w["wq"], w["wk"], w["wv"], w["wr"], w["wa1"], w["wa2"], w["ba"], w["ng"],
      w["wo"], w["lng"], w["lnb"], w["wrt"], w["br"])
    return outs


def _attn_kernel(*refs, bb, tl, c, has_past):
    if has_past:
        (x_ref, mod_ref, pk_ref, pv_ref, wkv_k_ref, wkv_v_ref, wq_ref, wo_ref, bias_ref, lng_ref,
         lnb_ref, wrt_ref, br_ref, x1_ref, hf_ref, ti_ref, tg_ref, ko_ref, vo_ref,
         kw_s, vw_s, q_s, o_s, sc_s, m_s, e_s) = refs
    else:
        (x_ref, mod_ref, wkv_k_ref, wkv_v_ref, wq_ref, wo_ref, bias_ref, lng_ref,
         lnb_ref, wrt_ref, br_ref, x1_ref, hf_ref, ti_ref, tg_ref, ko_ref, vo_ref,
         kw_s, vw_s, q_s, o_s, sc_s, m_s, e_s) = refs
    i = pl.program_id(1)
    d = x_ref.shape[-1]
    p = BAND_PAST
    w = p + c
    rows = bb * tl
    hd2 = 2 * (d // ATT_HEADS)

    @pl.when(i == 0)
    def _():
        if has_past:
            kw_s[:, 0:p, :] = pk_ref[...]
            vw_s[:, 0:p, :] = pv_ref[...]
        else:
            kw_s[:, 0:p, :] = jnp.zeros((bb, p, d), BF16)
            vw_s[:, 0:p, :] = jnp.zeros((bb, p, d), BF16)

    x3 = x_ref[...]
    mod = mod_ref[...]
    xb = x3.reshape(rows, d).astype(BF16)
    hb = (x3 * (1.0 + mod[:, 1:2, :]) + mod[:, 0:1, :]).reshape(rows, d).astype(BF16)
    kn = _dot(xb, wkv_k_ref[...])
    vn = _dot(xb, wkv_v_ref[...])
    ko_ref[...] = kn.reshape(bb, tl, d)
    vo_ref[...] = vn.reshape(bb, tl, d)
    kw_s[:, p:p + tl, :] = kn.astype(BF16).reshape(bb, tl, d)
    vw_s[:, p:p + tl, :] = vn.astype(BF16).reshape(bb, tl, d)
    q_s[...] = (_dot(hb, wq_ref[...]) * ((d // ATT_HEADS) ** -0.5)).astype(BF16)

    lane = lax.broadcasted_iota(jnp.int32, (c, hd2), 1)
    low = lane < (hd2 // 2)
    kpos = lax.broadcasted_iota(jnp.int32, (1, w), 1)
    chunks_per_seq = tl // c

    def chunk_body(n, carry):
        r0 = pl.multiple_of(n * c, c)
        bi = n // chunks_per_seq
        ci = n - bi * chunks_per_seq
        w0 = pl.multiple_of(ci * c, c)
        if not has_past:
            valid = (kpos + (i * tl + ci * c)) >= p
        for hp in range(ATT_HEADS // 2):
            qp = q_s[pl.ds(r0, c), hp * hd2:(hp + 1) * hd2]
            kwin = kw_s[bi, pl.ds(w0, w), hp * hd2:(hp + 1) * hd2]
            for half in range(2):
                qm = jnp.where(low if half == 0 else ~low, qp, jnp.zeros_like(qp))
                s = _dot_nt(qm, kwin) + bias_ref[2 * hp + half]
                if not has_past:
                    s = jnp.where(valid, s, NEG_INF)
                sc_s[2 * hp + half] = s
        for hh in range(ATT_HEADS):
            m_s[hh] = jnp.broadcast_to(jnp.max(sc_s[hh], axis=-1, keepdims=True), (c, LANES))
        for hh in range(ATT_HEADS):
            s = sc_s[hh]
            m = m_s[hh]
            parts = [jnp.exp(s[:, j * LANES:(j + 1) * LANES] - m) for j in range(w // LANES)]
            if w % LANES:
                parts.append(jnp.exp(s[:, w - w % LANES:] - m[:, :w % LANES]))
            e_s[hh] = jnp.concatenate(parts, axis=1).astype(BF16)
        ones = jnp.ones((w, hd2), BF16)
        for hp in range(ATT_HEADS // 2):
            vext = jnp.concatenate([vw_s[bi, pl.ds(w0, w), hp * hd2:(hp + 1) * hd2], ones], axis=1)
            outs = []
            for half in range(2):
                r = _dot(e_s[2 * hp + half], vext)
                outs.append(r[:, :hd2] / r[:, hd2:])
            o_s[pl.ds(r0, c), hp * hd2:(hp + 1) * hd2] = jnp.where(low, outs[0], outs[1])
        return carry

    lax.fori_loop(0, rows // c, chunk_body, 0)

    if tl >= p:
        kw_s[:, 0:p, :] = kw_s[:, tl:tl + p, :]
        vw_s[:, 0:p, :] = vw_s[:, tl:tl + p, :]

    mix = _dot(o_s[...].astype(BF16), wo_ref[...]).reshape(bb, tl, d)
    x1, hf, ti, tg = _post_mixer(x3, mix, mod[:, 2:3, :], mod[:, 3:4, :], mod[:, 4:5, :],
                                 lng_ref[...], lnb_ref[...], wrt_ref[...], br_ref[...])
    x1_ref[...] = x1
    _store_token_major(hf_ref, hf.reshape(rows, d))
    ti_ref[0] = ti
    tg_ref[0] = tg


def _attn_layer(x, mod, past_k, past_v, w, bb, tl):
    b, l, d = x.shape
    c = min(CHUNK, l)
    has_past = past_k is not None
    nb, nt = b // bb, l // tl
    assert nt == 1 or (tl >= BAND_PAST and bb == 1)
    rows = bb * tl
    keep = min(BAND_PAST, l)
    assert keep == tl or (keep == BAND_PAST and tl == BAND_PAST)
    kern = functools.partial(_attn_kernel, bb=bb, tl=tl, c=c, has_past=has_past)
    xmap = lambda ib, it: (ib, it, 0)
    bmap = lambda ib, it: (ib, 0, 0)
    rmap = lambda ib, it: (ib * nt + it, 0, 0)
    in_specs = [pl.BlockSpec((bb, tl, d), xmap), pl.BlockSpec((bb, 8, d), bmap)]
    args = [x, mod]
    if has_past:
        in_specs += [pl.BlockSpec((bb, BAND_PAST, d), bmap)] * 2
        args += [past_k, past_v]
    names = ["wkv_k", "wkv_v", "wq", "wo", "bias", "lng", "lnb", "wrt", "br"]
    in_specs += [_const_spec(w[n].shape) for n in names]
    args += [w[n] for n in names]
    outs = pl.pallas_call(
        kern,
        grid=(nb, nt),
        in_specs=in_specs,
        out_specs=[pl.BlockSpec((bb, tl, d), xmap),
                   pl.BlockSpec((rows * (d // LANES), LANES), lambda ib, it: (ib * nt + it, 0)),
                   pl.BlockSpec((1, 8, rows), rmap),
                   pl.BlockSpec((1, 8, rows), rmap),
                   pl.BlockSpec((bb, tl, d), bmap),
                   pl.BlockSpec((bb, tl, d), bmap)],
        out_shape=[jax.ShapeDtypeStruct((b, l, d), F32),
                   jax.ShapeDtypeStruct((b * l * (d // LANES), LANES), F32),
                   jax.ShapeDtypeStruct((nb * nt, 8, rows), jnp.int32),
                   jax.ShapeDtypeStruct((nb * nt, 8, rows), F32),
                   jax.ShapeDtypeStruct((b, keep, d), F32),
                   jax.ShapeDtypeStruct((b, keep, d), F32)],
        scratch_shapes=[pltpu.VMEM((bb, BAND_PAST + tl, d), BF16),
                        pltpu.VMEM((bb, BAND_PAST + tl, d), BF16),
                        pltpu.VMEM((rows, d), BF16),
                        pltpu.VMEM((rows, d), F32),
                        pltpu.VMEM((ATT_HEADS, c, BAND_PAST + c), F32),
                        pltpu.VMEM((ATT_HEADS, c, LANES), F32),
                        pltpu.VMEM((ATT_HEADS, c, BAND_PAST + c), BF16)],
        compiler_params=_params(("arbitrary", "arbitrary")),
        name="attn_layer",
    )(*args)
    return outs


def _moe_kernel(be_ref, nact_ref, st_ref, stn_ref, sr_ref, sg_ref, hf_hbm, wg_ref, bg_ref, wu_ref, bu_ref,
                wd_ref, bd_ref, y_hbm, xbuf, ybuf, xb_s, hm_s, wgb, wub, wdb, gsem, ssem, *, blk, g):
    i = pl.program_id(0)
    nact = nact_ref[0]
    f = hm_s.shape[1]
    n_col = 4
    fc = f // n_col
    per = blk // n_col
    rc = min(blk, 256)

    def gather_row(tok_ref, j):
        src = pl.multiple_of(tok_ref[0, 0, j], g)
        return pltpu.make_async_copy(hf_hbm.at[pl.ds(src, g)], xbuf.at[pl.ds(j * g, g)], gsem)

    def gather_all():
        return pltpu.make_async_copy(hf_hbm.at[pl.ds(0, blk * g)], xbuf, gsem)

    def scatter_all():
        return pltpu.make_async_copy(ybuf, y_hbm.at[pl.ds(y_hbm.shape[0] - blk * g, blk * g)], ssem)

    @pl.when(i == 0)
    def _():
        ybuf[...] = jnp.zeros(ybuf.shape, F32)
        scatter_all().start()

        def prime(j, carry):
            src = pl.multiple_of(st_ref[0, 0, j], g)
            pltpu.make_async_copy(hf_hbm.at[pl.ds(src, g)],
                                  xbuf.at[pl.ds(pl.multiple_of(j * g, g), g)], gsem).start()
            return carry

        lax.fori_loop(0, blk, prime, 0)

    @pl.when(i < nact)
    def _():
        e = be_ref[i]
        prev = be_ref[jnp.maximum(i - 1, 0)]

        @pl.when((i == 0) | (e != prev))
        def _():
            wgb[...] = wg_ref[0].astype(BF16)
            wub[...] = wu_ref[0].astype(BF16)
            wdb[...] = wd_ref[0].astype(BF16)

        gather_all().wait()
        for s in range(g):
            xb_s[:, s * LANES:(s + 1) * LANES] = xbuf[pl.ds(s, blk, stride=g), :].astype(BF16)
        for c in range(n_col):
            cols = slice(c * fc, (c + 1) * fc)
            a = jnp.minimum(_dot(xb_s[...], wgb[:, cols]) + bg_ref[0][:, cols], SWIGLU_LIMIT)
            u = jnp.clip(_dot(xb_s[...], wub[:, cols]) + bu_ref[0][:, cols], -SWIGLU_LIMIT, SWIGLU_LIMIT)
            hm_s[:, cols] = ((u + 1.0) * (a * jax.nn.sigmoid(SWIGLU_ALPHA * a))).astype(BF16)
            for j in range(c * per, (c + 1) * per):
                gather_row(stn_ref, j).start()
        scatter_all().wait()
        for r0 in range(0, blk, rc):
            y = (_dot(hm_s[r0:r0 + rc, :], wdb[...]) + bd_ref[0]) * sg_ref[0, r0:r0 + rc, :]
            for s in range(g):
                ybuf[pl.ds(r0 * g + s, rc, stride=g), :] = y[:, s * LANES:(s + 1) * LANES]
            for j in range(r0, r0 + rc):
                dst = pl.multiple_of(sr_ref[0, 0, j], g)
                pltpu.make_async_copy(ybuf.at[pl.ds(j * g, g)], y_hbm.at[pl.ds(dst, g)], ssem).start()

        @pl.when(i == nact - 1)
        def _():
            scatter_all().wait()
            gather_all().wait()


def _moe_experts(hf_tm, block_e, n_active, slot_t, slot_r, slot_g, wg, bg, wu, bu, wd, bd, blk):
    d = wg.shape[1]
    g = d // LANES
    t = hf_tm.shape[0] // g
    f = wg.shape[-1]
    n_blocks = block_e.shape[0]
    kern = functools.partial(_moe_kernel, blk=blk, g=g)
    wmap = lambda i, be, na: (be[i], 0, 0)
    smap = lambda i, be, na: (i, 0, 0)
    nmap = lambda i, be, na: (jnp.minimum(i + 1, n_blocks - 1), 0, 0)
    grid_spec = pltpu.PrefetchScalarGridSpec(
        num_scalar_prefetch=2,
        grid=(n_blocks,),
        in_specs=[pl.BlockSpec((1, 1, blk), smap, memory_space=pltpu.SMEM),
                  pl.BlockSpec((1, 1, blk), nmap, memory_space=pltpu.SMEM),
                  pl.BlockSpec((1, 1, blk), smap, memory_space=pltpu.SMEM),
                  pl.BlockSpec((1, blk, 1), smap),
                  pl.BlockSpec(memory_space=pl.ANY),
                  pl.BlockSpec((1, d, f), wmap), pl.BlockSpec((1, 1, f), wmap),
                  pl.BlockSpec((1, d, f), wmap), pl.BlockSpec((1, 1, f), wmap),
                  pl.BlockSpec((1, f, d), wmap), pl.BlockSpec((1, 1, d), wmap)],
        out_specs=pl.BlockSpec(memory_space=pl.ANY),
        scratch_shapes=[pltpu.VMEM((blk * g, LANES), F32), pltpu.VMEM((blk * g, LANES), F32),
                        pltpu.VMEM((blk, d), BF16), pltpu.VMEM((blk, f), BF16),
                        pltpu.VMEM((d, f), BF16), pltpu.VMEM((d, f), BF16), pltpu.VMEM((f, d), BF16),
                        pltpu.SemaphoreType.DMA, pltpu.SemaphoreType.DMA])
    st3 = (slot_t * g).reshape(n_blocks, 1, blk)
    return pl.pallas_call(
        kern,
        grid_spec=grid_spec,
        out_shape=jax.ShapeDtypeStruct(((TOP_K * t + blk) * g, LANES), F32),
        compiler_params=_params(("arbitrary",)),
        name="moe_experts",
    )(block_e, n_active, st3, st3, (slot_r * g).reshape(n_blocks, 1, blk),
      slot_g.reshape(n_blocks, blk, 1), hf_tm, wg, bg.reshape(N_EXPERTS, 1, f), wu,
      bu.reshape(N_EXPERTS, 1, f), wd, bd.reshape(N_EXPERTS, 1, d))


def _route(ti, tg, blk):
    t = ti.shape[0]
    npair = t * TOP_K
    flat_e = ti.reshape(-1)
    order = jnp.argsort(flat_e).astype(jnp.int32)
    experts = jnp.arange(N_EXPERTS, dtype=jnp.int32)
    counts = jnp.sum((flat_e[:, None] == experts[None, :]).astype(jnp.int32), axis=0)
    padded = (counts + blk - 1) // blk * blk
    pend = jnp.cumsum(padded)
    pstart = pend - padded
    cstart = jnp.cumsum(counts) - counts
    n_blocks = -(-npair // blk) + N_EXPERTS
    bstart = jnp.arange(n_blocks, dtype=jnp.int32) * blk
    block_e = jnp.minimum(jnp.sum((pend[None, :] <= bstart[:, None]).astype(jnp.int32), axis=1),
                          N_EXPERTS - 1)
    within = jnp.arange(blk, dtype=jnp.int32)[None, :]
    off = (bstart - pstart[block_e])[:, None] + within
    real = off < counts[block_e][:, None]
    spos = jnp.clip(cstart[block_e][:, None] + off, 0, npair - 1)
    sp = order[spos]
    slot_t = jnp.where(real, sp // TOP_K, 0)
    slot_r = jnp.where(real, (sp % TOP_K) * t + sp // TOP_K, npair + within)
    slot_g = jnp.where(real, tg.reshape(-1)[sp], 0.0)
    n_active = (pend[-1] // blk).astype(jnp.int32).reshape(1)
    return block_e, n_active, slot_t, slot_r, slot_g


def _combine_kernel(x_ref, mod_ref, y0, y1, y2, y3, lng_ref, lnb_ref, o_ref):
    tl, d = x_ref.shape[1], x_ref.shape[2]
    ff = ((_load_token_major(y0, tl, d) + _load_token_major(y1, tl, d)) +
          (_load_token_major(y2, tl, d) + _load_token_major(y3, tl, d)))
    g_f = mod_ref[0, 5:6, :]
    o_ref[0] = _layer_norm(DEEPNORM_ALPHA * x_ref[0] + g_f * ff, lng_ref[...], lnb_ref[...])


def _combine_layer(x1, mod, y, lng, lnb, tl):
    b, l, d = x1.shape
    nt = l // tl
    tb = (b * l) // tl
    ymaps = [functools.partial(lambda ib, it, k: (k * tb + ib * nt + it, 0), k=k) for k in range(TOP_K)]
    return pl.pallas_call(
        _combine_kernel,
        grid=(b, nt),
        in_specs=[pl.BlockSpec((1, tl, d), lambda ib, it: (ib, it, 0)),
                  pl.BlockSpec((1, 8, d), lambda ib, it: (ib, 0, 0))]
                 + [pl.BlockSpec((tl * (d // LANES), LANES), m) for m in ymaps]
                 + [_const_spec(lng.shape), _const_spec(lnb.shape)],
        out_specs=pl.BlockSpec((1, tl, d), lambda ib, it: (ib, it, 0)),
        out_shape=jax.ShapeDtypeStruct((b, l, d), F32),
        compiler_params=_params(("arbitrary", "arbitrary")),
        name="moe_combine",
    )(x1, mod, y, y, y, y, lng, lnb)


def _flatten_router(ti, tg, nb, nt, bb, tl):
    def f(a):
        a = a[:, :TOP_K].reshape(nb, nt, TOP_K, bb, tl)
        return a.transpose(0, 3, 1, 4, 2).reshape(nb * bb * nt * tl, TOP_K)
    return f(ti), f(tg)


def _moe_layer(x1, hf, ti, tg, mod, moe_w, lng, lnb, blk, tl):
    b, l, d = x1.shape
    block_e, n_active, slot_t, slot_r, slot_g = _route(ti, tg, blk)
    y = _moe_experts(hf, block_e, n_active, slot_t, slot_r, slot_g, *moe_w, blk)
    return _combine_layer(x1, mod, y, lng, lnb, tl)


def _rel_bias_table(rel_bias, c):
    w = BAND_PAST + c
    dist = jnp.arange(c)[:, None] - jnp.arange(w)[None, :] + BAND_PAST
    return rel_bias[:, jnp.clip(dist, -REL_CLIP, REL_CLIP) + REL_CLIP].astype(F32)


def _trunk(x, mods, s0, past_k, past_v, wts, gla_bb, att_bb, blk):
    b, l, d = x.shape
    tl = min(TOKEN_TILE, l)
    c = min(CHUNK, l)
    row = lambda v: v.reshape(1, -1)

    def mod_of(layer):
        m = mods[layer].reshape(b, 6, d)
        return jnp.concatenate([m, jnp.zeros((b, 2, d), F32)], axis=1)

    def router_w(layer):
        return wts["moe_wr"][layer].T, wts["moe_br"][layer].reshape(N_EXPERTS, 1)

    def moe_w(layer):
        return (wts["moe_wg"][layer], wts["moe_bg"][layer], wts["moe_wu"][layer], wts["moe_bu"][layer],
                wts["moe_wd"][layer], wts["moe_bd"][layer])

    wrt, br = router_w(0)
    gw = dict(wq=wts["gla_wq"][0].astype(BF16), wk=wts["gla_wk"][0].astype(BF16),
              wv=wts["gla_wv"][0].astype(BF16), wr=wts["gla_wr"][0].astype(BF16),
              wa1=wts["gla_wa1"][0].astype(BF16), wa2=wts["gla_wa2"][0].astype(BF16),
              ba=row(wts["gla_ba"][0]), ng=row(wts["gla_norm_g"][0]),
              wo=wts["gla_wo"][0].astype(BF16), lng=row(wts["ln_g"][0, 0]), lnb=row(wts["ln_b"][0, 0]),
              wrt=wrt, br=br)
    mod0 = mod_of(0)
    s0t = jnp.swapaxes(s0, -1, -2)
    x1, hf, ti, tg, sfin_t = _gla_layer(x, mod0, s0t, gw, gla_bb, tl)
    ti, tg = _flatten_router(ti, tg, b // gla_bb, l // tl, gla_bb, tl)
    x = _moe_layer(x1, hf, ti, tg, mod0, moe_w(0), row(wts["ln_g"][0, 1]), row(wts["ln_b"][0, 1]), blk, tl)
    s_fin = jnp.swapaxes(sfin_t, -1, -2)[None]

    wrt, br = router_w(1)
    aw = dict(wkv_k=wts["kv_wk"].astype(BF16), wkv_v=wts["kv_wv"].astype(BF16),
              wq=wts["att_wq"][0].astype(BF16), wo=wts["att_wo"][0].astype(BF16),
              bias=_rel_bias_table(wts["att_rel_bias"][0], c),
              lng=row(wts["ln_g"][1, 0]), lnb=row(wts["ln_b"][1, 0]), wrt=wrt, br=br)
    mod1 = mod_of(1)
    x1, hf, ti, tg, k_new, v_new = _attn_layer(x, mod1, past_k, past_v, aw, att_bb, tl)
    ti, tg = _flatten_router(ti, tg, b // att_bb, l // tl, att_bb, tl)
    x = _moe_layer(x1, hf, ti, tg, mod1, moe_w(1), row(wts["ln_g"][1, 1]), row(wts["ln_b"][1, 1]), blk, tl)
    keep = k_new.shape[1]
    hd = d // ATT_HEADS
    return (x, s_fin, k_new.reshape(b, keep, ATT_HEADS, hd), v_new.reshape(b, keep, ATT_HEADS, hd))


def kernel(x_prompt, x_sample, c_prompt, c_sample, state_gla, cache_k, cache_v, ada_w, ada_b, ln_g, ln_b, gla_wq, gla_wk, gla_wv, gla_wa1, gla_wa2, gla_ba, gla_wr, gla_norm_g, gla_wo, kv_wk, kv_wv, att_wq, att_wo, att_rel_bias, moe_wr, moe_br, moe_wg, moe_bg, moe_wu, moe_bu, moe_wd, moe_bd):
    wts = dict(ln_g=ln_g, ln_b=ln_b, gla_wq=gla_wq, gla_wk=gla_wk, gla_wv=gla_wv, gla_wa1=gla_wa1,
               gla_wa2=gla_wa2, gla_ba=gla_ba, gla_wr=gla_wr, gla_norm_g=gla_norm_g, gla_wo=gla_wo,
               kv_wk=kv_wk, kv_wv=kv_wv, att_wq=att_wq, att_wo=att_wo, att_rel_bias=att_rel_bias,
               moe_wr=moe_wr, moe_br=moe_br, moe_wg=moe_wg, moe_bg=moe_bg, moe_wu=moe_wu,
               moe_bu=moe_bu, moe_wd=moe_wd, moe_bd=moe_bd)
    b_p, l_p, d = x_prompt.shape
    b_s, l_s, _ = x_sample.shape
    assert cache_k.shape[1] == BAND_PAST
    mods = _ada_modulation(jnp.concatenate([c_prompt, c_sample], axis=0), ada_w, ada_b)
    mods_p, mods_s = mods[:, :b_p], mods[:, b_p:]

    s0_p = jnp.zeros((b_p,) + state_gla.shape[2:], F32)
    blk_p = MOE_BLOCK_LARGE if b_p * l_p * TOP_K >= 64 * MOE_BLOCK_LARGE else MOE_BLOCK_SMALL
    y_p, s_p, k_p, v_p = _trunk(x_prompt, mods_p, s0_p, None, None, wts, 1, 1, blk_p)

    pk = cache_k.reshape(b_s, BAND_PAST, d).astype(BF16)
    pv = cache_v.reshape(b_s, BAND_PAST, d).astype(BF16)
    blk_s = MOE_BLOCK_LARGE if b_s * l_s * TOP_K >= 64 * MOE_BLOCK_LARGE else MOE_BLOCK_SMALL
    y_s, s_s, k_s, v_s = _trunk(x_sample, mods_s, state_gla[0], pk, pv, wts, b_s, min(4, b_s), blk_s)
    return (y_p, y_s, s_p, k_p, v_p, s_s, k_s, v_s)
```

```python
import functools
import math

import jax
import jax.numpy as jnp
from jax import lax
from jax.experimental import pallas as pl
from jax.experimental.pallas import tpu as pltpu

F32 = jnp.float32
BF16 = jnp.bfloat16

CHUNK = 64
GLA_HEADS = 4
GLA_TAU = 16.0
ATT_HEADS = 16
BAND_PAST = 512
REL_CLIP = 128
N_EXPERTS = 32
TOP_K = 4
SWIGLU_LIMIT = 7.0
SWIGLU_ALPHA = 1.702
DEPTH = 2
DEEPNORM_ALPHA = (2.0 * DEPTH) ** 0.25
LN_EPS = 1e-5
NEG_INF = -1e30

V7X_VMEM_LIMIT_BYTES = 56 * 1024 * 1024
TOKEN_TILE = 512
GLA_SUB = 16
MOE_BLOCK_LARGE = 512
MOE_BLOCK_SMALL = 64
LANES = 128


def _dot(a, b):
    return jnp.dot(a, b, preferred_element_type=F32)


def _dot_nt(a, b):
    return lax.dot_general(a, b, (((1,), (1,)), ((), ())), preferred_element_type=F32)


def _dot_tn(a, b):
    return lax.dot_general(a, b, (((0,), (0,)), ((), ())), preferred_element_type=F32)


def _layer_norm(y, g, b):
    mu = jnp.mean(y, axis=-1, keepdims=True)
    yc = y - mu
    var = jnp.mean(yc * yc, axis=-1, keepdims=True)
    return yc * lax.rsqrt(var + LN_EPS) * g + b


def _store_token_major(ref, val):
    rows, d = val.shape
    g = d // LANES
    for s in range(g):
        ref[pl.ds(s, rows, stride=g), :] = val[:, s * LANES:(s + 1) * LANES]


def _load_token_major(ref, rows, d):
    g = d // LANES
    return jnp.concatenate([ref[pl.ds(s, rows, stride=g), :] for s in range(g)], axis=1)


def _const_spec(shape):
    nd = len(shape)
    return pl.BlockSpec(shape, lambda *_: (0,) * nd, pipeline_mode=pl.Buffered(1))


def _params(sem):
    return pltpu.CompilerParams(dimension_semantics=sem, vmem_limit_bytes=V7X_VMEM_LIMIT_BYTES)


def _ada_kernel(c_ref, w_ref, b_ref, o_ref):
    c = c_ref[...]
    s = c * jax.nn.sigmoid(c)
    o_ref[0] = jnp.dot(s, w_ref[0], preferred_element_type=F32,
                       precision=lax.Precision.HIGHEST) + b_ref[0]


def _ada_modulation(c, ada_w, ada_b):
    n, d = c.shape
    depth, _, d6 = ada_w.shape
    tn = d6 // 4
    return pl.pallas_call(
        _ada_kernel,
        grid=(depth, d6 // tn),
        in_specs=[pl.BlockSpec((n, d), lambda l, j: (0, 0)),
                  pl.BlockSpec((1, d, tn), lambda l, j: (l, 0, j)),
                  pl.BlockSpec((1, 1, tn), lambda l, j: (l, 0, j))],
        out_specs=pl.BlockSpec((1, n, tn), lambda l, j: (l, 0, j)),
        out_shape=jax.ShapeDtypeStruct((depth, n, d6), F32),
        compiler_params=_params(("arbitrary", "arbitrary")),
        name="ada_modulation",
    )(c, ada_w, ada_b.reshape(depth, 1, d6))


def _post_mixer(x3, mix3, g_m, sh_f, sc_f, lng, lnb, wrt, br):
    bb, tl, d = x3.shape
    rows = bb * tl
    x1 = _layer_norm(DEEPNORM_ALPHA * x3 + g_m * mix3, lng, lnb)
    hf = x1 * (1.0 + sc_f) + sh_f
    logits = lax.dot_general(wrt, hf.reshape(rows, d), (((1,), (1,)), ((), ())),
                             preferred_element_type=F32, precision=lax.Precision.HIGHEST) + br
    eidx = lax.broadcasted_iota(jnp.int32, logits.shape, 0).astype(F32)
    vals, idxs = [], []
    cur = logits
    for _ in range(TOP_K):
        m = jnp.max(cur, axis=0, keepdims=True)
        sel = jnp.min(jnp.where(cur == m, eidx, float(N_EXPERTS)), axis=0, keepdims=True)
        vals.append(m)
        idxs.append(sel)
        cur = jnp.where(eidx == sel, -jnp.inf, cur)
    ex = [jnp.exp(v - vals[0]) for v in vals]
    den = (ex[0] + ex[1]) + (ex[2] + ex[3])
    out_row = lax.broadcasted_iota(jnp.int32, (8, rows), 0)
    ti = jnp.zeros((8, rows), F32)
    tg = jnp.zeros((8, rows), F32)
    for k in range(TOP_K):
        ti = jnp.where(out_row == k, idxs[k], ti)
        tg = jnp.where(out_row == k, ex[k] / den, tg)
    return x1, hf, ti.astype(jnp.int32), tg


def _gla_kernel(x_ref, mod_ref, s0_ref, wq_ref, wk_ref, wv_ref, wr_ref, wa1_ref, wa2_ref, ba_ref,
                ng_ref, wo_ref, lng_ref, lnb_ref, wrt_ref, br_ref,
                x1_ref, hf_ref, ti_ref, tg_ref, sfin_ref,
                st_s, q_s, k_s, v_s, la_s, o_s, *, bb, tl, c):
    i = pl.program_id(1)
    d = x_ref.shape[-1]
    dk = q_s.shape[-1] // GLA_HEADS
    dv = v_s.shape[-1] // GLA_HEADS
    rows = bb * tl
    sub = min(GLA_SUB, c)

    @pl.when(i == 0)
    def _():
        st_s[...] = s0_ref[...]

    x3 = x_ref[...]
    mod = mod_ref[...]
    h = (x3 * (1.0 + mod[:, 1:2, :]) + mod[:, 0:1, :]).reshape(rows, d)
    hb = h.astype(BF16)
    q_s[...] = _dot(hb, wq_ref[...]) * (dk ** -0.5)
    k_s[...] = _dot(hb, wk_ref[...])
    v_s[...] = _dot(hb, wv_ref[...])
    a = _dot(_dot(hb, wa1_ref[...]).astype(BF16), wa2_ref[...]) + ba_ref[...]
    la_s[...] = (jnp.minimum(a, 0.0) - jnp.log1p(jnp.exp(-jnp.abs(a)))) * (1.0 / GLA_TAU)

    tril = (lax.broadcasted_iota(jnp.int32, (c, c), 0) >=
            lax.broadcasted_iota(jnp.int32, (c, c), 1)).astype(BF16)
    lane_c = lax.broadcasted_iota(jnp.int32, (sub, c), 1)
    row_c = lax.broadcasted_iota(jnp.int32, (sub, c), 0)
    krow = lax.broadcasted_iota(jnp.int32, (c, dk), 0)
    ng = ng_ref[...]
    chunks_per_seq = tl // c

    def chunk_body(n, carry):
        r0 = pl.multiple_of(n * c, c)
        bi = n // chunks_per_seq
        la = la_s[pl.ds(r0, c), :]
        la_hi = la.astype(BF16)
        la_lo = (la - la_hi.astype(F32)).astype(BF16)
        bcum = _dot(tril, la_hi) + _dot(tril, la_lo)
        for hd in range(GLA_HEADS):
            qh = q_s[pl.ds(r0, c), hd * dk:(hd + 1) * dk]
            kh = k_s[pl.ds(r0, c), hd * dk:(hd + 1) * dk]
            vh = v_s[pl.ds(r0, c), hd * dv:(hd + 1) * dv].astype(BF16)
            bh = bcum[:, hd * dk:(hd + 1) * dk]
            blocks = []
            for sb in range(c // sub):
                lo = sb * sub
                qi = qh[lo:lo + sub]
                bi_rows = bh[lo:lo + sub]
                if sb > 0:
                    ref_b = bh[lo:lo + 1]
                    qt = (qi * jnp.exp(bi_rows - ref_b)).astype(BF16)
                    kt = jnp.where(krow < lo, kh * jnp.exp(jnp.minimum(ref_b - bh, 0.0)), 0.0)
                    acc = _dot_nt(qt, kt.astype(BF16))
                else:
                    acc = jnp.zeros((sub, c), F32)
                for s in range(sub):
                    g = lo + s
                    e = jnp.exp(jnp.minimum(bi_rows - bh[g:g + 1], 0.0))
                    col = jnp.sum(qi * e * kh[g:g + 1], axis=-1, keepdims=True)
                    acc = jnp.where((lane_c == g) & (row_c >= s), col, acc)
                blocks.append(acc)
            amat = blocks[0] if len(blocks) == 1 else jnp.concatenate(blocks, axis=0)
            st = st_s[bi, hd]
            o = _dot(amat.astype(BF16), vh) + _dot_nt((qh * jnp.exp(bh)).astype(BF16), st.astype(BF16))
            ms = jnp.mean(o * o, axis=-1, keepdims=True)
            o_s[pl.ds(r0, c), hd * dv:(hd + 1) * dv] = o * lax.rsqrt(ms + LN_EPS) * ng
            bl = bh[c - 1:c]
            kd = (kh * jnp.exp(bl - bh)).astype(BF16)
            st_s[bi, hd] = st * jnp.exp(bl) + _dot_tn(vh, kd)
        return carry

    lax.fori_loop(0, rows // c, chunk_body, 0)

    r = _dot(hb, wr_ref[...])
    og = (o_s[...] * (r * jax.nn.sigmoid(r))).astype(BF16)
    mix = _dot(og, wo_ref[...]).reshape(bb, tl, d)
    x1, hf, ti, tg = _post_mixer(x3, mix, mod[:, 2:3, :], mod[:, 3:4, :], mod[:, 4:5, :],
                                 lng_ref[...], lnb_ref[...], wrt_ref[...], br_ref[...])
    x1_ref[...] = x1
    _store_token_major(hf_ref, hf.reshape(rows, d))
    ti_ref[0] = ti
    tg_ref[0] = tg

    @pl.when(i == pl.num_programs(1) - 1)
    def _():
        sfin_ref[...] = st_s[...]


def _gla_layer(x, mod, s0t, w, bb, tl):
    b, l, d = x.shape
    c = min(CHUNK, l)
    hk = w["wq"].shape[1]
    hv = w["wv"].shape[1]
    nb, nt = b // bb, l // tl
    assert bb == 1 or nt == 1
    rows = bb * tl
    dv, dk = s0t.shape[2], s0t.shape[3]
    kern = functools.partial(_gla_kernel, bb=bb, tl=tl, c=c)
    xmap = lambda ib, it: (ib, it, 0)
    bmap = lambda ib, it: (ib, 0, 0)
    smap = lambda ib, it: (ib, 0, 0, 0)
    rmap = lambda ib, it: (ib * nt + it, 0, 0)
    outs = pl.pallas_call(
        kern,
        grid=(nb, nt),
        in_specs=[pl.BlockSpec((bb, tl, d), xmap),
                  pl.BlockSpec((bb, 8, d), bmap),
                  pl.BlockSpec((bb, GLA_HEADS, dv, dk), smap),
                  _const_spec(w["wq"].shape), _const_spec(w["wk"].shape), _const_spec(w["wv"].shape),
                  _const_spec(w["wr"].shape), _const_spec(w["wa1"].shape), _const_spec(w["wa2"].shape),
                  _const_spec(w["ba"].shape), _const_spec(w["ng"].shape), _const_spec(w["wo"].shape),
                  _const_spec(w["lng"].shape), _const_spec(w["lnb"].shape),
                  _const_spec(w["wrt"].shape), _const_spec(w["br"].shape)],
        out_specs=[pl.BlockSpec((bb, tl, d), xmap),
                   pl.BlockSpec((rows * (d // LANES), LANES), lambda ib, it: (ib * nt + it, 0)),
                   pl.BlockSpec((1, 8, rows), rmap),
                   pl.BlockSpec((1, 8, rows), rmap),
                   pl.BlockSpec((bb, GLA_HEADS, dv, dk), smap)],
        out_shape=[jax.ShapeDtypeStruct((b, l, d), F32),
                   jax.ShapeDtypeStruct((b * l * (d // LANES), LANES), F32),
                   jax.ShapeDtypeStruct((nb * nt, 8, rows), jnp.int32),
                   jax.ShapeDtypeStruct((nb * nt, 8, rows), F32),
                   jax.ShapeDtypeStruct(s0t.shape, F32)],
        scratch_shapes=[pltpu.VMEM((bb, GLA_HEADS, dv, dk), F32),
                        pltpu.VMEM((rows, hk), F32), pltpu.VMEM((rows, hk), F32),
                        pltpu.VMEM((rows, hv), F32), pltpu.VMEM((rows, hk), F32),
                        pltpu.VMEM((rows, hv), F32)],
        compiler_params=_params(("arbitrary", "arbitrary")),
        name="gla_layer",
    )(x, mod, s0t, w["wq"], w["wk"], w["wv"], w["wr"], w["wa1"], w["wa2"], w["ba"], w["ng"],
      w["wo"], w["lng"], w["lnb"], w["wrt"], w["br"])
    return outs


def _attn_kernel(*refs, bb, tl, c, has_past):
    if has_past:
        (x_ref, mod_ref, pk_ref, pv_ref, wkv_k_ref, wkv_v_ref, wq_ref, wo_ref, bias_ref, lng_ref,
         lnb_ref, wrt_ref, br_ref, x1_ref, hf_ref, ti_ref, tg_ref, ko_ref, vo_ref,
         kw_s, vw_s, q_s, o_s, sc_s, m_s, e_s) = refs
    else:
        (x_ref, mod_ref, wkv_k_ref, wkv_v_ref, wq_ref, wo_ref, bias_ref, lng_ref,
         lnb_ref, wrt_ref, br_ref, x1_ref, hf_ref, ti_ref, tg_ref, ko_ref, vo_ref,
         kw_s, vw_s, q_s, o_s, sc_s, m_s, e_s) = refs
    i = pl.program_id(1)
    d = x_ref.shape[-1]
    p = BAND_PAST
    w = p + c
    rows = bb * tl
    hd2 = 2 * (d // ATT_HEADS)

    @pl.when(i == 0)
    def _():
        if has_past:
            kw_s[:, 0:p, :] = pk_ref[...]
            vw_s[:, 0:p, :] = pv_ref[...]
        else:
            kw_s[:, 0:p, :] = jnp.zeros((bb, p, d), BF16)
            vw_s[:, 0:p, :] = jnp.zeros((bb, p, d), BF16)

    x3 = x_ref[...]
    mod = mod_ref[...]
    xb = x3.reshape(rows, d).astype(BF16)
    hb = (x3 * (1.0 + mod[:, 1:2, :]) + mod[:, 0:1, :]).reshape(rows, d).astype(BF16)
    kn = _dot(xb, wkv_k_ref[...])
    vn = _dot(xb, wkv_v_ref[...])
    ko_ref[...] = kn.reshape(bb, tl, d)
    vo_ref[...] = vn.reshape(bb, tl, d)
    kw_s[:, p:p + tl, :] = kn.astype(BF16).reshape(bb, tl, d)
    vw_s[:, p:p + tl, :] = vn.astype(BF16).reshape(bb, tl, d)
    q_s[...] = (_dot(hb, wq_ref[...]) * ((d // ATT_HEADS) ** -0.5)).astype(BF16)

    lane = lax.broadcasted_iota(jnp.int32, (c, hd2), 1)
    low = lane < (hd2 // 2)
    kpos = lax.broadcasted_iota(jnp.int32, (1, w), 1)
    chunks_per_seq = tl // c

    def chunk_body(n, carry):
        r0 = pl.multiple_of(n * c, c)
        bi = n // chunks_per_seq
        ci = n - bi * chunks_per_seq
        w0 = pl.multiple_of(ci * c, c)
        if not has_past:
            valid = (kpos + (i * tl + ci * c)) >= p
        for hp in range(ATT_HEADS // 2):
            qp = q_s[pl.ds(r0, c), hp * hd2:(hp + 1) * hd2]
            kwin = kw_s[bi, pl.ds(w0, w), hp * hd2:(hp + 1) * hd2]
            for half in range(2):
                qm = jnp.where(low if half == 0 else ~low, qp, jnp.zeros_like(qp))
                s = _dot_nt(qm, kwin) + bias_ref[2 * hp + half]
                if not has_past:
                    s = jnp.where(valid, s, NEG_INF)
                sc_s[2 * hp + half] = s
        for hh in range(ATT_HEADS):
            m_s[hh] = jnp.broadcast_to(jnp.max(sc_s[hh], axis=-1, keepdims=True), (c, LANES))
        for hh in range(ATT_HEADS):
            s = sc_s[hh]
            m = m_s[hh]
            parts = [jnp.exp(s[:, j * LANES:(j + 1) * LANES] - m) for j in range(w // LANES)]
            if w % LANES:
                parts.append(jnp.exp(s[:, w - w % LANES:] - m[:, :w % LANES]))
            e_s[hh] = jnp.concatenate(parts, axis=1).astype(BF16)
        ones = jnp.ones((w, hd2), BF16)
        for hp in range(ATT_HEADS // 2):
            vext = jnp.concatenate([vw_s[bi, pl.ds(w0, w), hp * hd2:(hp + 1) * hd2], ones], axis=1)
            outs = []
            for half in range(2):
                r = _dot(e_s[2 * hp + half], vext)
                outs.append(r[:, :hd2] / r[:, hd2:])
            o_s[pl.ds(r0, c), hp * hd2:(hp + 1) * hd2] = jnp.where(low, outs[0], outs[1])
        return carry

    lax.fori_loop(0, rows // c, chunk_body, 0)

    if tl >= p:
        kw_s[:, 0:p, :] = kw_s[:, tl:tl + p, :]
        vw_s[:, 0:p, :] = vw_s[:, tl:tl + p, :]

    mix = _dot(o_s[...].astype(BF16), wo_ref[...]).reshape(bb, tl, d)
    x1, hf, ti, tg = _post_mixer(x3, mix, mod[:, 2:3, :], mod[:, 3:4, :], mod[:, 4:5, :],
                                 lng_ref[...], lnb_ref[...], wrt_ref[...], br_ref[...])
    x1_ref[...] = x1
    _store_token_major(hf_ref, hf.reshape(rows, d))
    ti_ref[0] = ti
    tg_ref[0] = tg


def _attn_layer(x, mod, past_k, past_v, w, bb, tl):
    b, l, d = x.shape
    c = min(CHUNK, l)
    has_past = past_k is not None
    nb, nt = b // bb, l // tl
    assert nt == 1 or (tl >= BAND_PAST and bb == 1)
    rows = bb * tl
    keep = min(BAND_PAST, l)
    assert keep == tl or (keep == BAND_PAST and tl == BAND_PAST)
    kern = functools.partial(_attn_kernel, bb=bb, tl=tl, c=c, has_past=has_past)
    xmap = lambda ib, it: (ib, it, 0)
    bmap = lambda ib, it: (ib, 0, 0)
    rmap = lambda ib, it: (ib * nt + it, 0, 0)
    in_specs = [pl.BlockSpec((bb, tl, d), xmap), pl.BlockSpec((bb, 8, d), bmap)]
    args = [x, mod]
    if has_past:
        in_specs += [pl.BlockSpec((bb, BAND_PAST, d), bmap)] * 2
        args += [past_k, past_v]
    names = ["wkv_k", "wkv_v", "wq", "wo", "bias", "lng", "lnb", "wrt", "br"]
    in_specs += [_const_spec(w[n].shape) for n in names]
    args += [w[n] for n in names]
    outs = pl.pallas_call(
        kern,
        grid=(nb, nt),
        in_specs=in_specs,
        out_specs=[pl.BlockSpec((bb, tl, d), xmap),
                   pl.BlockSpec((rows * (d // LANES), LANES), lambda ib, it: (ib * nt + it, 0)),
                   pl.BlockSpec((1, 8, rows), rmap),
                   pl.BlockSpec((1, 8, rows), rmap),
                   pl.BlockSpec((bb, tl, d), bmap),
                   pl.BlockSpec((bb, tl, d), bmap)],
        out_shape=[jax.ShapeDtypeStruct((b, l, d), F32),
                   jax.ShapeDtypeStruct((b * l * (d // LANES), LANES), F32),
                   jax.ShapeDtypeStruct((nb * nt, 8, rows), jnp.int32),
                   jax.ShapeDtypeStruct((nb * nt, 8, rows), F32),
                   jax.ShapeDtypeStruct((b, keep, d), F32),
                   jax.ShapeDtypeStruct((b, keep, d), F32)],
        scratch_shapes=[pltpu.VMEM((bb, BAND_PAST + tl, d), BF16),
                        pltpu.VMEM((bb, BAND_PAST + tl, d), BF16),
                        pltpu.VMEM((rows, d), BF16),
                        pltpu.VMEM((rows, d), F32),
                        pltpu.VMEM((ATT_HEADS, c, BAND_PAST + c), F32),
                        pltpu.VMEM((ATT_HEADS, c, LANES), F32),
                        pltpu.VMEM((ATT_HEADS, c, BAND_PAST + c), BF16)],
        compiler_params=_params(("arbitrary", "arbitrary")),
        name="attn_layer",
    )(*args)
    return outs


def _moe_kernel(be_ref, nact_ref, st_ref, stn_ref, sr_ref, hf_hbm, wg_ref, bg_ref, wu_ref, bu_ref,
                wd_ref, bd_ref, y_hbm, xbuf, ybuf, xb_s, hm_s, wgb, wub, wdb, gsem, ssem, *, blk, g):
    i = pl.program_id(0)
    nact = nact_ref[0]
    f = hm_s.shape[1]
    n_col = 4
    fc = f // n_col
    per = blk // n_col
    rc = min(blk, 256)

    def gather_row(tok_ref, j):
        src = pl.multiple_of(tok_ref[0, 0, j], g)
        return pltpu.make_async_copy(hf_hbm.at[pl.ds(src, g)], xbuf.at[pl.ds(j * g, g)], gsem)

    def gather_all():
        return pltpu.make_async_copy(hf_hbm.at[pl.ds(0, blk * g)], xbuf, gsem)

    def scatter_all():
        return pltpu.make_async_copy(ybuf, y_hbm.at[pl.ds(y_hbm.shape[0] - blk * g, blk * g)], ssem)

    @pl.when(i == 0)
    def _():
        ybuf[...] = jnp.zeros(ybuf.shape, F32)
        scatter_all().start()

        def prime(j, carry):
            src = pl.multiple_of(st_ref[0, 0, j], g)
            pltpu.make_async_copy(hf_hbm.at[pl.ds(src, g)],
                                  xbuf.at[pl.ds(pl.multiple_of(j * g, g), g)], gsem).start()
            return carry

        lax.fori_loop(0, blk, prime, 0)

    @pl.when(i < nact)
    def _():
        e = be_ref[i]
        prev = be_ref[jnp.maximum(i - 1, 0)]

        @pl.when((i == 0) | (e != prev))
        def _():
            wgb[...] = wg_ref[0, 0].astype(BF16)
            wub[...] = wu_ref[0, 0].astype(BF16)
            wdb[...] = wd_ref[0, 0].astype(BF16)

        gather_all().wait()
        for s in range(g):
            xb_s[:, s * LANES:(s + 1) * LANES] = xbuf[pl.ds(s, blk, stride=g), :].astype(BF16)
        for c in range(n_col):
            cols = slice(c * fc, (c + 1) * fc)
            a = jnp.minimum(_dot(xb_s[...], wgb[:, cols]) + bg_ref[0, 0][:, cols], SWIGLU_LIMIT)
            u = jnp.clip(_dot(xb_s[...], wub[:, cols]) + bu_ref[0, 0][:, cols], -SWIGLU_LIMIT, SWIGLU_LIMIT)
            hm_s[:, cols] = ((u + 1.0) * (a * jax.nn.sigmoid(SWIGLU_ALPHA * a))).astype(BF16)
            for j in range(c * per, (c + 1) * per):
                gather_row(stn_ref, j).start()
        scatter_all().wait()
        for r0 in range(0, blk, rc):
            y = _dot(hm_s[r0:r0 + rc, :], wdb[...]) + bd_ref[0, 0]
            for s in range(g):
                ybuf[pl.ds(r0 * g + s, rc, stride=g), :] = y[:, s * LANES:(s + 1) * LANES]
            for j in range(r0, r0 + rc):
                dst = pl.multiple_of(sr_ref[0, 0, j], g)
                pltpu.make_async_copy(ybuf.at[pl.ds(j * g, g)], y_hbm.at[pl.ds(dst, g)], ssem).start()

        @pl.when(i == nact - 1)
        def _():
            scatter_all().wait()
            gather_all().wait()


def _moe_experts(hf_tm, block_e, n_active, slot_t, slot_r, layer, wg, bg, wu, bu, wd, bd, blk):
    d = wg.shape[2]
    g = d // LANES
    t = hf_tm.shape[0] // g
    f = wg.shape[-1]
    n_blocks = block_e.shape[0]
    kern = functools.partial(_moe_kernel, blk=blk, g=g)
    wmap = lambda i, be, na: (layer, be[i], 0, 0)
    smap = lambda i, be, na: (i, 0, 0)
    nmap = lambda i, be, na: (jnp.minimum(i + 1, n_blocks - 1), 0, 0)
    grid_spec = pltpu.PrefetchScalarGridSpec(
        num_scalar_prefetch=2,
        grid=(n_blocks,),
        in_specs=[pl.BlockSpec((1, 1, blk), smap, memory_space=pltpu.SMEM),
                  pl.BlockSpec((1, 1, blk), nmap, memory_space=pltpu.SMEM),
                  pl.BlockSpec((1, 1, blk), smap, memory_space=pltpu.SMEM),
                  pl.BlockSpec(memory_space=pl.ANY),
                  pl.BlockSpec((1, 1, d, f), wmap), pl.BlockSpec((1, 1, 1, f), wmap),
                  pl.BlockSpec((1, 1, d, f), wmap), pl.BlockSpec((1, 1, 1, f), wmap),
                  pl.BlockSpec((1, 1, f, d), wmap), pl.BlockSpec((1, 1, 1, d), wmap)],
        out_specs=pl.BlockSpec(memory_space=pl.ANY),
        scratch_shapes=[pltpu.VMEM((blk * g, LANES), F32), pltpu.VMEM((blk * g, LANES), F32),
                        pltpu.VMEM((blk, d), BF16), pltpu.VMEM((blk, f), BF16),
                        pltpu.VMEM((d, f), BF16), pltpu.VMEM((d, f), BF16), pltpu.VMEM((f, d), BF16),
                        pltpu.SemaphoreType.DMA, pltpu.SemaphoreType.DMA])
    st3 = (slot_t * g).reshape(n_blocks, 1, blk)
    depth = wg.shape[0]
    return pl.pallas_call(
        kern,
        grid_spec=grid_spec,
        out_shape=jax.ShapeDtypeStruct(((TOP_K * t + blk) * g, LANES), F32),
        compiler_params=_params(("arbitrary",)),
        name="moe_experts",
    )(block_e, n_active, st3, st3, (slot_r * g).reshape(n_blocks, 1, blk), hf_tm,
      wg, bg.reshape(depth, N_EXPERTS, 1, f), wu, bu.reshape(depth, N_EXPERTS, 1, f),
      wd, bd.reshape(depth, N_EXPERTS, 1, d))


def _route(ti8, blk):
    nb, _, rows = ti8.shape
    t = nb * rows
    npair = t * TOP_K
    assert npair % blk == 0
    flat_e = ti8[:, :TOP_K, :].reshape(-1)
    experts = jnp.arange(N_EXPERTS, dtype=jnp.int32)
    counts = jnp.sum((flat_e[:, None] == experts[None, :]).astype(jnp.int32), axis=0)
    padded = (counts + blk - 1) // blk * blk
    pend = jnp.cumsum(padded)
    need = padded - counts
    dummy_key = jnp.where(jnp.arange(blk, dtype=jnp.int32)[None, :] < need[:, None],
                          2 * experts[:, None] + 1, 2 * N_EXPERTS)
    keys = jnp.concatenate([2 * flat_e, dummy_key.reshape(-1)])
    vals = jnp.concatenate([jnp.arange(npair, dtype=jnp.int32),
                            jnp.full((N_EXPERTS * blk,), -1, jnp.int32)])
    _, slot_pair = lax.sort_key_val(keys, vals)
    n_blocks = npair // blk + N_EXPERTS
    real = slot_pair >= 0
    q = jnp.maximum(slot_pair, 0)
    tok = (q // (TOP_K * rows)) * rows + q % rows
    choice = (q // rows) % TOP_K
    slot_t = jnp.where(real, tok, 0)
    spare = npair + jnp.arange(n_blocks * blk, dtype=jnp.int32) % blk
    slot_r = jnp.where(real, choice * t + tok, spare)
    bstart = jnp.arange(n_blocks, dtype=jnp.int32) * blk
    block_e = jnp.minimum(jnp.sum((pend[None, :] <= bstart[:, None]).astype(jnp.int32), axis=1),
                          N_EXPERTS - 1)
    n_active = (pend[-1] // blk).astype(jnp.int32).reshape(1)
    return block_e, n_active, slot_t, slot_r


def _combine_kernel(x_ref, mod_ref, tg_ref, y0, y1, y2, y3, lng_ref, lnb_ref, o_ref):
    bb, tl, d = x_ref.shape
    rows = bb * tl
    sel = (lax.broadcasted_iota(jnp.int32, (8, LANES), 0) ==
           lax.broadcasted_iota(jnp.int32, (8, LANES), 1)).astype(F32)
    gt = lax.dot_general(tg_ref[0], sel, (((0,), (0,)), ((), ())), preferred_element_type=F32,
                         precision=lax.Precision.HIGHEST)
    ys = (y0, y1, y2, y3)
    terms = [gt[:, k:k + 1] * _load_token_major(ys[k], rows, d) for k in range(TOP_K)]
    ff = ((terms[0] + terms[1]) + (terms[2] + terms[3])).reshape(bb, tl, d)
    g_f = mod_ref[:, 5:6, :]
    o_ref[...] = _layer_norm(DEEPNORM_ALPHA * x_ref[...] + g_f * ff, lng_ref[...], lnb_ref[...])


def _combine_layer(x1, mod, tg8, y, lng, lnb, bb, tl):
    b, l, d = x1.shape
    nb, nt = b // bb, l // tl
    rows = bb * tl
    tb = (b * l) // rows
    ymaps = [functools.partial(lambda ib, it, k: (k * tb + ib * nt + it, 0), k=k) for k in range(TOP_K)]
    return pl.pallas_call(
        _combine_kernel,
        grid=(nb, nt),
        in_specs=[pl.BlockSpec((bb, tl, d), lambda ib, it: (ib, it, 0)),
                  pl.BlockSpec((bb, 8, d), lambda ib, it: (ib, 0, 0)),
                  pl.BlockSpec((1, 8, rows), lambda ib, it: (ib * nt + it, 0, 0))]
                 + [pl.BlockSpec((rows * (d // LANES), LANES), m) for m in ymaps]
                 + [_const_spec(lng.shape), _const_spec(lnb.shape)],
        out_specs=pl.BlockSpec((bb, tl, d), lambda ib, it: (ib, it, 0)),
        out_shape=jax.ShapeDtypeStruct((b, l, d), F32),
        compiler_params=_params(("arbitrary", "arbitrary")),
        name="moe_combine",
    )(x1, mod, tg8, y, y, y, y, lng, lnb)


def _moe_layer(x1, hf, ti8, tg8, mod, layer, moe_w, lng, lnb, blk, bb, tl):
    block_e, n_active, slot_t, slot_r = _route(ti8, blk)
    y = _moe_experts(hf, block_e, n_active, slot_t, slot_r, layer, *moe_w, blk)
    return _combine_layer(x1, mod, tg8, y, lng, lnb, bb, tl)


def _rel_bias_table(rel_bias, c):
    w = BAND_PAST + c
    dist = jnp.arange(c)[:, None] - jnp.arange(w)[None, :] + BAND_PAST
    return rel_bias[:, jnp.clip(dist, -REL_CLIP, REL_CLIP) + REL_CLIP].astype(F32)


def _trunk(x, mods, s0, past_k, past_v, wts, gla_bb, att_bb, blk):
    b, l, d = x.shape
    tl = min(TOKEN_TILE, l)
    c = min(CHUNK, l)
    row = lambda v: v.reshape(1, -1)

    def mod_of(layer):
        m = mods[layer].reshape(b, 6, d)
        return jnp.concatenate([m, jnp.zeros((b, 2, d), F32)], axis=1)

    def router_w(layer):
        return wts["moe_wr"][layer].T, wts["moe_br"][layer].reshape(N_EXPERTS, 1)

    moe_w = (wts["moe_wg"], wts["moe_bg"], wts["moe_wu"], wts["moe_bu"], wts["moe_wd"], wts["moe_bd"])

    wrt, br = router_w(0)
    gw = dict(wq=wts["gla_wq"][0].astype(BF16), wk=wts["gla_wk"][0].astype(BF16),
              wv=wts["gla_wv"][0].astype(BF16), wr=wts["gla_wr"][0].astype(BF16),
              wa1=wts["gla_wa1"][0].astype(BF16), wa2=wts["gla_wa2"][0].astype(BF16),
              ba=row(wts["gla_ba"][0]), ng=row(wts["gla_norm_g"][0]),
              wo=wts["gla_wo"][0].astype(BF16), lng=row(wts["ln_g"][0, 0]), lnb=row(wts["ln_b"][0, 0]),
              wrt=wrt, br=br)
    mod0 = mod_of(0)
    s0t = jnp.swapaxes(s0, -1, -2)
    x1, hf, ti8, tg8, sfin_t = _gla_layer(x, mod0, s0t, gw, gla_bb, tl)
    x = _moe_layer(x1, hf, ti8, tg8, mod0, 0, moe_w, row(wts["ln_g"][0, 1]), row(wts["ln_b"][0, 1]),
                   blk, gla_bb, tl)
    s_fin = jnp.swapaxes(sfin_t, -1, -2)[None]

    wrt, br = router_w(1)
    aw = dict(wkv_k=wts["kv_wk"].astype(BF16), wkv_v=wts["kv_wv"].astype(BF16),
              wq=wts["att_wq"][0].astype(BF16), wo=wts["att_wo"][0].astype(BF16),
              bias=_rel_bias_table(wts["att_rel_bias"][0], c),
              lng=row(wts["ln_g"][1, 0]), lnb=row(wts["ln_b"][1, 0]), wrt=wrt, br=br)
    mod1 = mod_of(1)
    x1, hf, ti8, tg8, k_new, v_new = _attn_layer(x, mod1, past_k, past_v, aw, att_bb, tl)
    x = _moe_layer(x1, hf, ti8, tg8, mod1, 1, moe_w, row(wts["ln_g"][1, 1]), row(wts["ln_b"][1, 1]),
                   blk, att_bb, tl)
    keep = k_new.shape[1]
    hd = d // ATT_HEADS
    return (x, s_fin, k_new.reshape(b, keep, ATT_HEADS, hd), v_new.reshape(b, keep, ATT_HEADS, hd))


def kernel(x_prompt, x_sample, c_prompt, c_sample, state_gla, cache_k, cache_v, ada_w, ada_b, ln_g, ln_b, gla_wq, gla_wk, gla_wv, gla_wa1, gla_wa2, gla_ba, gla_wr, gla_norm_g, gla_wo, kv_wk, kv_wv, att_wq, att_wo, att_rel_bias, moe_wr, moe_br, moe_wg, moe_bg, moe_wu, moe_bu, moe_wd, moe_bd):
    wts = dict(ln_g=ln_g, ln_b=ln_b, gla_wq=gla_wq, gla_wk=gla_wk, gla_wv=gla_wv, gla_wa1=gla_wa1,
               gla_wa2=gla_wa2, gla_ba=gla_ba, gla_wr=gla_wr, gla_norm_g=gla_norm_g, gla_wo=gla_wo,
               kv_wk=kv_wk, kv_wv=kv_wv, att_wq=att_wq, att_wo=att_wo, att_rel_bias=att_rel_bias,
               moe_wr=moe_wr, moe_br=moe_br, moe_wg=moe_wg, moe_bg=moe_bg, moe_wu=moe_wu,
               moe_bu=moe_bu, moe_wd=moe_wd, moe_bd=moe_bd)
    b_p, l_p, d = x_prompt.shape
    b_s, l_s, _ = x_sample.shape
    assert cache_k.shape[1] == BAND_PAST
    mods = _ada_modulation(jnp.concatenate([c_prompt, c_sample], axis=0), ada_w, ada_b)
    mods_p, mods_s = mods[:, :b_p], mods[:, b_p:]

    s0_p = jnp.zeros((b_p,) + state_gla.shape[2:], F32)
    blk_p = MOE_BLOCK_LARGE if b_p * l_p * TOP_K >= 64 * MOE_BLOCK_LARGE else MOE_BLOCK_SMALL
    y_p, s_p, k_p, v_p = _trunk(x_prompt, mods_p, s0_p, None, None, wts, 1, 1, blk_p)

    pk = cache_k.reshape(b_s, BAND_PAST, d).astype(BF16)
    pv = cache_v.reshape(b_s, BAND_PAST, d).astype(BF16)
    blk_s = MOE_BLOCK_LARGE if b_s * l_s * TOP_K >= 64 * MOE_BLOCK_LARGE else MOE_BLOCK_SMALL
    y_s, s_s, k_s, v_s = _trunk(x_sample, mods_s, state_gla[0], pk, pv, wts, b_s, min(4, b_s), blk_s)
    return (y_p, y_s, s_p, k_p, v_p, s_s, k_s, v_s)
```

```python
import functools
import math

import jax
import jax.numpy as jnp
from jax import lax
from jax.experimental import pallas as pl
from jax.experimental.pallas import tpu as pltpu

F32 = jnp.float32
BF16 = jnp.bfloat16

CHUNK = 64
GLA_HEADS = 4
GLA_TAU = 16.0
ATT_HEADS = 16
BAND_PAST = 512
REL_CLIP = 128
N_EXPERTS = 32
TOP_K = 4
SWIGLU_LIMIT = 7.0
SWIGLU_ALPHA = 1.702
DEPTH = 2
DEEPNORM_ALPHA = (2.0 * DEPTH) ** 0.25
LN_EPS = 1e-5
NEG_INF = -1e30

V7X_VMEM_LIMIT_BYTES = 56 * 1024 * 1024
TOKEN_TILE = 512
GLA_SUB = 16
MOE_BLOCK_LARGE = 512
MOE_BLOCK_SMALL = 64
LANES = 128


def _dot(a, b):
    return jnp.dot(a, b, preferred_element_type=F32)


def _dot_nt(a, b):
    return lax.dot_general(a, b, (((1,), (1,)), ((), ())), preferred_element_type=F32)


def _dot_tn(a, b):
    return lax.dot_general(a, b, (((0,), (0,)), ((), ())), preferred_element_type=F32)


def _layer_norm(y, g, b):
    mu = jnp.mean(y, axis=-1, keepdims=True)
    yc = y - mu
    var = jnp.mean(yc * yc, axis=-1, keepdims=True)
    return yc * lax.rsqrt(var + LN_EPS) * g + b


def _store_token_major(ref, val):
    rows, d = val.shape
    g = d // LANES
    for s in range(g):
        ref[pl.ds(s, rows, stride=g), :] = val[:, s * LANES:(s + 1) * LANES]


def _load_token_major(ref, rows, d):
    g = d // LANES
    return jnp.concatenate([ref[pl.ds(s, rows, stride=g), :] for s in range(g)], axis=1)


def _const_spec(shape):
    nd = len(shape)
    return pl.BlockSpec(shape, lambda *_: (0,) * nd, pipeline_mode=pl.Buffered(1))


def _params(sem):
    return pltpu.CompilerParams(dimension_semantics=sem, vmem_limit_bytes=V7X_VMEM_LIMIT_BYTES)


def _ada_kernel(c_ref, w_ref, b_ref, o_ref):
    c = c_ref[...]
    s = c * jax.nn.sigmoid(c)
    o_ref[0] = jnp.dot(s, w_ref[0], preferred_element_type=F32,
                       precision=lax.Precision.HIGHEST) + b_ref[0]


def _ada_modulation(c, ada_w, ada_b):
    n, d = c.shape
    depth, _, d6 = ada_w.shape
    tn = d6 // 4
    return pl.pallas_call(
        _ada_kernel,
        grid=(depth, d6 // tn),
        in_specs=[pl.BlockSpec((n, d), lambda l, j: (0, 0)),
                  pl.BlockSpec((1, d, tn), lambda l, j: (l, 0, j)),
                  pl.BlockSpec((1, 1, tn), lambda l, j: (l, 0, j))],
        out_specs=pl.BlockSpec((1, n, tn), lambda l, j: (l, 0, j)),
        out_shape=jax.ShapeDtypeStruct((depth, n, d6), F32),
        compiler_params=_params(("arbitrary", "arbitrary")),
        name="ada_modulation",
    )(c, ada_w, ada_b.reshape(depth, 1, d6))


def _pair_ranks(onehots, cnt_s):
    e, rows = onehots[0].shape
    before = (lax.broadcasted_iota(jnp.int32, (rows, rows), 0) <
              lax.broadcasted_iota(jnp.int32, (rows, rows), 1)).astype(BF16)
    prefix = _dot(jnp.concatenate(onehots, axis=0).astype(BF16), before)
    base = cnt_s[:, 0:1]
    ranks = []
    for k, oh in enumerate(onehots):
        ranks.append(jnp.sum(oh * (base + prefix[k * e:(k + 1) * e]), axis=0, keepdims=True))
        base = base + jnp.sum(oh, axis=1, keepdims=True)
    cnt_s[...] = jnp.broadcast_to(base, cnt_s.shape)
    return ranks


def _post_mixer(x3, mix3, g_m, sh_f, sc_f, lng, lnb, wrt, br, cnt_s=None):
    bb, tl, d = x3.shape
    rows = bb * tl
    x1 = _layer_norm(DEEPNORM_ALPHA * x3 + g_m * mix3, lng, lnb)
    hf = x1 * (1.0 + sc_f) + sh_f
    logits = lax.dot_general(wrt, hf.reshape(rows, d), (((1,), (1,)), ((), ())),
                             preferred_element_type=F32, precision=lax.Precision.HIGHEST) + br
    eidx = lax.broadcasted_iota(jnp.int32, logits.shape, 0).astype(F32)
    vals, idxs = [], []
    cur = logits
    for _ in range(TOP_K):
        m = jnp.max(cur, axis=0, keepdims=True)
        sel = jnp.min(jnp.where(cur == m, eidx, float(N_EXPERTS)), axis=0, keepdims=True)
        vals.append(m)
        idxs.append(sel)
        cur = jnp.where(eidx == sel, -jnp.inf, cur)
    ex = [jnp.exp(v - vals[0]) for v in vals]
    den = (ex[0] + ex[1]) + (ex[2] + ex[3])
    out_row = lax.broadcasted_iota(jnp.int32, (8, rows), 0)
    ti = jnp.zeros((8, rows), F32)
    tg = jnp.zeros((8, rows), F32)
    for k in range(TOP_K):
        ti = jnp.where(out_row == k, idxs[k], ti)
        tg = jnp.where(out_row == k, ex[k] / den, tg)
    if cnt_s is None:
        return x1, hf, ti.astype(jnp.int32), tg
    ranks = _pair_ranks([(eidx == idxs[k]).astype(F32) for k in range(TOP_K)], cnt_s)
    tr = jnp.zeros((8, rows), F32)
    for k in range(TOP_K):
        tr = jnp.where(out_row == k, ranks[k], tr)
    return x1, hf, ti.astype(jnp.int32), tg, tr.astype(jnp.int32)


def _gla_kernel(x_ref, mod_ref, s0_ref, wq_ref, wk_ref, wv_ref, wr_ref, wa1_ref, wa2_ref, ba_ref,
                ng_ref, wo_ref, lng_ref, lnb_ref, wrt_ref, br_ref,
                x1_ref, hf_ref, ti_ref, tg_ref, tr_ref, cnt_ref, sfin_ref,
                st_s, q_s, k_s, v_s, la_s, o_s, cnt_s, *, bb, tl, c):
    i = pl.program_id(1)
    d = x_ref.shape[-1]
    dk = q_s.shape[-1] // GLA_HEADS
    dv = v_s.shape[-1] // GLA_HEADS
    rows = bb * tl
    sub = min(GLA_SUB, c)

    @pl.when(i == 0)
    def _():
        st_s[...] = s0_ref[...]

    @pl.when((i == 0) & (pl.program_id(0) == 0))
    def _():
        cnt_s[...] = jnp.zeros(cnt_s.shape, F32)

    x3 = x_ref[...]
    mod = mod_ref[...]
    h = (x3 * (1.0 + mod[:, 1:2, :]) + mod[:, 0:1, :]).reshape(rows, d)
    hb = h.astype(BF16)
    q_s[...] = _dot(hb, wq_ref[...]) * (dk ** -0.5)
    k_s[...] = _dot(hb, wk_ref[...])
    v_s[...] = _dot(hb, wv_ref[...])
    a = _dot(_dot(hb, wa1_ref[...]).astype(BF16), wa2_ref[...]) + ba_ref[...]
    la_s[...] = (jnp.minimum(a, 0.0) - jnp.log1p(jnp.exp(-jnp.abs(a)))) * (1.0 / GLA_TAU)

    tril = (lax.broadcasted_iota(jnp.int32, (c, c), 0) >=
            lax.broadcasted_iota(jnp.int32, (c, c), 1)).astype(BF16)
    lane_c = lax.broadcasted_iota(jnp.int32, (sub, c), 1)
    row_c = lax.broadcasted_iota(jnp.int32, (sub, c), 0)
    krow = lax.broadcasted_iota(jnp.int32, (c, dk), 0)
    ng = ng_ref[...]
    chunks_per_seq = tl // c

    def chunk_body(n, carry):
        r0 = pl.multiple_of(n * c, c)
        bi = n // chunks_per_seq
        la = la_s[pl.ds(r0, c), :]
        la_hi = la.astype(BF16)
        la_lo = (la - la_hi.astype(F32)).astype(BF16)
        bcum = _dot(tril, la_hi) + _dot(tril, la_lo)
        for hd in range(GLA_HEADS):
            qh = q_s[pl.ds(r0, c), hd * dk:(hd + 1) * dk]
            kh = k_s[pl.ds(r0, c), hd * dk:(hd + 1) * dk]
            vh = v_s[pl.ds(r0, c), hd * dv:(hd + 1) * dv].astype(BF16)
            bh = bcum[:, hd * dk:(hd + 1) * dk]
            blocks = []
            for sb in range(c // sub):
                lo = sb * sub
                qi = qh[lo:lo + sub]
                bi_rows = bh[lo:lo + sub]
                if sb > 0:
                    ref_b = bh[lo:lo + 1]
                    qt = (qi * jnp.exp(bi_rows - ref_b)).astype(BF16)
                    kt = jnp.where(krow < lo, kh * jnp.exp(jnp.minimum(ref_b - bh, 0.0)), 0.0)
                    acc = _dot_nt(qt, kt.astype(BF16))
                else:
                    acc = jnp.zeros((sub, c), F32)
                for s in range(sub):
                    g = lo + s
                    e = jnp.exp(jnp.minimum(bi_rows - bh[g:g + 1], 0.0))
                    col = jnp.sum(qi * e * kh[g:g + 1], axis=-1, keepdims=True)
                    acc = jnp.where((lane_c == g) & (row_c >= s), col, acc)
                blocks.append(acc)
            amat = blocks[0] if len(blocks) == 1 else jnp.concatenate(blocks, axis=0)
            st = st_s[bi, hd]
            o = _dot(amat.astype(BF16), vh) + _dot_nt((qh * jnp.exp(bh)).astype(BF16), st.astype(BF16))
            ms = jnp.mean(o * o, axis=-1, keepdims=True)
            o_s[pl.ds(r0, c), hd * dv:(hd + 1) * dv] = o * lax.rsqrt(ms + LN_EPS) * ng
            bl = bh[c - 1:c]
            kd = (kh * jnp.exp(bl - bh)).astype(BF16)
            st_s[bi, hd] = st * jnp.exp(bl) + _dot_tn(vh, kd)
        return carry

    lax.fori_loop(0, rows // c, chunk_body, 0)

    r = _dot(hb, wr_ref[...])
    og = (o_s[...] * (r * jax.nn.sigmoid(r))).astype(BF16)
    mix = _dot(og, wo_ref[...]).reshape(bb, tl, d)
    x1, hf, ti, tg, tr = _post_mixer(x3, mix, mod[:, 2:3, :], mod[:, 3:4, :], mod[:, 4:5, :],
                                     lng_ref[...], lnb_ref[...], wrt_ref[...], br_ref[...], cnt_s)
    x1_ref[...] = x1
    _store_token_major(hf_ref, hf.reshape(rows, d))
    ti_ref[0] = ti
    tg_ref[0] = tg
    tr_ref[0] = tr
    cnt_ref[...] = cnt_s[...]

    @pl.when(i == pl.num_programs(1) - 1)
    def _():
        sfin_ref[...] = st_s[...]


def _gla_layer(x, mod, s0t, w, bb, tl):
    b, l, d = x.shape
    c = min(CHUNK, l)
    hk = w["wq"].shape[1]
    hv = w["wv"].shape[1]
    nb, nt = b // bb, l // tl
    assert bb == 1 or nt == 1
    rows = bb * tl
    dv, dk = s0t.shape[2], s0t.shape[3]
    kern = functools.partial(_gla_kernel, bb=bb, tl=tl, c=c)
    xmap = lambda ib, it: (ib, it, 0)
    bmap = lambda ib, it: (ib, 0, 0)
    smap = lambda ib, it: (ib, 0, 0, 0)
    rmap = lambda ib, it: (ib * nt + it, 0, 0)
    outs = pl.pallas_call(
        kern,
        grid=(nb, nt),
        in_specs=[pl.BlockSpec((bb, tl, d), xmap),
                  pl.BlockSpec((bb, 8, d), bmap),
                  pl.BlockSpec((bb, GLA_HEADS, dv, dk), smap),
                  _const_spec(w["wq"].shape), _const_spec(w["wk"].shape), _const_spec(w["wv"].shape),
                  _const_spec(w["wr"].shape), _const_spec(w["wa1"].shape), _const_spec(w["wa2"].shape),
                  _const_spec(w["ba"].shape), _const_spec(w["ng"].shape), _const_spec(w["wo"].shape),
                  _const_spec(w["lng"].shape), _const_spec(w["lnb"].shape),
                  _const_spec(w["wrt"].shape), _const_spec(w["br"].shape)],
        out_specs=[pl.BlockSpec((bb, tl, d), xmap),
                   pl.BlockSpec((rows * (d // LANES), LANES), lambda ib, it: (ib * nt + it, 0)),
                   pl.BlockSpec((1, 8, rows), rmap),
                   pl.BlockSpec((1, 8, rows), rmap),
                   pl.BlockSpec((1, 8, rows), rmap),
                   pl.BlockSpec((N_EXPERTS, LANES), lambda ib, it: (0, 0)),
                   pl.BlockSpec((bb, GLA_HEADS, dv, dk), smap)],
        out_shape=[jax.ShapeDtypeStruct((b, l, d), F32),
                   jax.ShapeDtypeStruct((b * l * (d // LANES), LANES), F32),
                   jax.ShapeDtypeStruct((nb * nt, 8, rows), jnp.int32),
                   jax.ShapeDtypeStruct((nb * nt, 8, rows), F32),
                   jax.ShapeDtypeStruct((nb * nt, 8, rows), jnp.int32),
                   jax.ShapeDtypeStruct((N_EXPERTS, LANES), F32),
                   jax.ShapeDtypeStruct(s0t.shape, F32)],
        scratch_shapes=[pltpu.VMEM((bb, GLA_HEADS, dv, dk), F32),
                        pltpu.VMEM((rows, hk), F32), pltpu.VMEM((rows, hk), F32),
                        pltpu.VMEM((rows, hv), F32), pltpu.VMEM((rows, hk), F32),
                        pltpu.VMEM((rows, hv), F32), pltpu.VMEM((N_EXPERTS, LANES), F32)],
        compiler_params=_params(("arbitrary", "arbitrary")),
        name="gla_layer",
    )(x, mod, s0t, w["wq"], w["wk"], w["wv"], w["wr"], w["wa1"], w["wa2"], w["ba"], w["ng"],
      w["wo"], w["lng"], w["lnb"], w["wrt"], w["br"])
    return outs


def _attn_kernel(*refs, bb, tl, c, has_past):
    it_refs = iter(refs)
    nxt = lambda n: [next(it_refs) for _ in range(n)]
    xp_ref, mod_ref, modp_ref, tgp_ref, pos_ref, posn_ref, y_hbm, lngp_ref, lnbp_ref = nxt(9)
    pk_ref, pv_ref = nxt(2) if has_past else (None, None)
    wkv_k_ref, wkv_v_ref, wq_ref, wo_ref, bias_ref, lng_ref, lnb_ref, wrt_ref, br_ref = nxt(9)
    x1_ref, hf_ref, ti_ref, tg_ref, ko_ref, vo_ref = nxt(6)
    kw_s, vw_s, q_s, o_s, sc_s, m_s, e_s, ygb, ysem = nxt(9)
    i = pl.program_id(1)
    gstep = pl.program_id(0) * pl.num_programs(1) + i
    last_step = pl.num_programs(0) * pl.num_programs(1) - 1
    d = xp_ref.shape[-1]
    g = d // LANES
    p = BAND_PAST
    w = p + c
    rows = bb * tl
    npr = TOP_K * rows
    hd2 = 2 * (d // ATT_HEADS)

    def y_row(tab_ref, j):
        src = pl.multiple_of(tab_ref[0, 0, j], g)
        return pltpu.make_async_copy(y_hbm.at[pl.ds(src, g)],
                                     ygb.at[pl.ds(pl.multiple_of(j * g, g), g)], ysem)

    def y_all():
        return pltpu.make_async_copy(y_hbm.at[pl.ds(0, npr * g)], ygb, ysem)

    @pl.when(gstep == 0)
    def _():
        def prime(j, carry):
            y_row(pos_ref, j).start()
            return carry

        lax.fori_loop(0, npr, prime, 0)

    y_all().wait()
    sel = (lax.broadcasted_iota(jnp.int32, (8, LANES), 0) ==
           lax.broadcasted_iota(jnp.int32, (8, LANES), 1)).astype(F32)
    gt = lax.dot_general(tgp_ref[0], sel, (((0,), (0,)), ((), ())), preferred_element_type=F32,
                         precision=lax.Precision.HIGHEST)
    terms = []
    for k in range(TOP_K):
        yk = jnp.concatenate([ygb[pl.ds(k * rows * g + s, rows, stride=g), :] for s in range(g)], axis=1)
        terms.append(gt[:, k:k + 1] * yk)
    ff = ((terms[0] + terms[1]) + (terms[2] + terms[3])).reshape(bb, tl, d)
    x3 = _layer_norm(DEEPNORM_ALPHA * xp_ref[...] + modp_ref[:, 5:6, :] * ff, lngp_ref[...], lnbp_ref[...])

    @pl.when(i == 0)
    def _():
        if has_past:
            kw_s[:, 0:p, :] = pk_ref[...]
            vw_s[:, 0:p, :] = pv_ref[...]
        else:
            kw_s[:, 0:p, :] = jnp.zeros((bb, p, d), BF16)
            vw_s[:, 0:p, :] = jnp.zeros((bb, p, d), BF16)

    mod = mod_ref[...]
    xb = x3.reshape(rows, d).astype(BF16)
    hb = (x3 * (1.0 + mod[:, 1:2, :]) + mod[:, 0:1, :]).reshape(rows, d).astype(BF16)
    kn = _dot(xb, wkv_k_ref[...])
    vn = _dot(xb, wkv_v_ref[...])
    ko_ref[...] = kn.reshape(bb, tl, d)
    vo_ref[...] = vn.reshape(bb, tl, d)
    kw_s[:, p:p + tl, :] = kn.astype(BF16).reshape(bb, tl, d)
    vw_s[:, p:p + tl, :] = vn.astype(BF16).reshape(bb, tl, d)
    q_s[...] = (_dot(hb, wq_ref[...]) * ((d // ATT_HEADS) ** -0.5)).astype(BF16)

    lane = lax.broadcasted_iota(jnp.int32, (c, hd2), 1)
    low = lane < (hd2 // 2)
    kpos = lax.broadcasted_iota(jnp.int32, (1, w), 1)
    chunks_per_seq = tl // c
    rows_per_chunk = npr // (rows // c)

    def chunk_body(n, carry):
        for jj in range(rows_per_chunk):
            y_row(posn_ref, n * rows_per_chunk + jj).start()
        r0 = pl.multiple_of(n * c, c)
        bi = n // chunks_per_seq
        ci = n - bi * chunks_per_seq
        w0 = pl.multiple_of(ci * c, c)
        if not has_past:
            valid = (kpos + (i * tl + ci * c)) >= p
        for hp in range(ATT_HEADS // 2):
            qp = q_s[pl.ds(r0, c), hp * hd2:(hp + 1) * hd2]
            kwin = kw_s[bi, pl.ds(w0, w), hp * hd2:(hp + 1) * hd2]
            for half in range(2):
                qm = jnp.where(low if half == 0 else ~low, qp, jnp.zeros_like(qp))
                s = _dot_nt(qm, kwin) + bias_ref[2 * hp + half]
                if not has_past:
                    s = jnp.where(valid, s, NEG_INF)
                sc_s[2 * hp + half] = s
        for hh in range(ATT_HEADS):
            m_s[hh] = jnp.broadcast_to(jnp.max(sc_s[hh], axis=-1, keepdims=True), (c, LANES))
        for hh in range(ATT_HEADS):
            s = sc_s[hh]
            m = m_s[hh]
            parts = [jnp.exp(s[:, j * LANES:(j + 1) * LANES] - m) for j in range(w // LANES)]
            if w % LANES:
                parts.append(jnp.exp(s[:, w - w % LANES:] - m[:, :w % LANES]))
            e_s[hh] = jnp.concatenate(parts, axis=1).astype(BF16)
        ones = jnp.ones((w, hd2), BF16)
        for hp in range(ATT_HEADS // 2):
            vext = jnp.concatenate([vw_s[bi, pl.ds(w0, w), hp * hd2:(hp + 1) * hd2], ones], axis=1)
            outs = []
            for half in range(2):
                r = _dot(e_s[2 * hp + half], vext)
                outs.append(r[:, :hd2] / r[:, hd2:])
            o_s[pl.ds(r0, c), hp * hd2:(hp + 1) * hd2] = jnp.where(low, outs[0], outs[1])
        return carry

    lax.fori_loop(0, rows // c, chunk_body, 0)

    @pl.when(gstep == last_step)
    def _():
        y_all().wait()

    if tl >= p:
        kw_s[:, 0:p, :] = kw_s[:, tl:tl + p, :]
        vw_s[:, 0:p, :] = vw_s[:, tl:tl + p, :]

    mix = _dot(o_s[...].astype(BF16), wo_ref[...]).reshape(bb, tl, d)
    x1, hf, ti, tg = _post_mixer(x3, mix, mod[:, 2:3, :], mod[:, 3:4, :], mod[:, 4:5, :],
                                 lng_ref[...], lnb_ref[...], wrt_ref[...], br_ref[...])
    x1_ref[...] = x1
    _store_token_major(hf_ref, hf.reshape(rows, d))
    ti_ref[0] = ti
    tg_ref[0] = tg


def _attn_layer(xp, mod, modp, tgp8, pos, y_sorted, lngp, lnbp, past_k, past_v, w, bb, tl):
    x = xp
    b, l, d = x.shape
    c = min(CHUNK, l)
    has_past = past_k is not None
    nb, nt = b // bb, l // tl
    assert nt == 1 or (tl >= BAND_PAST and bb == 1)
    rows = bb * tl
    keep = min(BAND_PAST, l)
    assert keep == tl or (keep == BAND_PAST and tl == BAND_PAST)
    kern = functools.partial(_attn_kernel, bb=bb, tl=tl, c=c, has_past=has_past)
    xmap = lambda ib, it: (ib, it, 0)
    bmap = lambda ib, it: (ib, 0, 0)
    rmap = lambda ib, it: (ib * nt + it, 0, 0)
    npr = TOP_K * rows
    nsteps = nb * nt
    in_specs = [pl.BlockSpec((bb, tl, d), xmap), pl.BlockSpec((bb, 8, d), bmap),
                pl.BlockSpec((bb, 8, d), bmap), pl.BlockSpec((1, 8, rows), rmap),
                pl.BlockSpec((1, 1, npr), rmap, memory_space=pltpu.SMEM),
                pl.BlockSpec((1, 1, npr), lambda ib, it: (jnp.minimum(ib * nt + it + 1, nsteps - 1), 0, 0),
                             memory_space=pltpu.SMEM),
                pl.BlockSpec(memory_space=pl.ANY),
                _const_spec(lngp.shape), _const_spec(lnbp.shape)]
    args = [x, mod, modp, tgp8, pos, pos, y_sorted, lngp, lnbp]
    if has_past:
        in_specs += [pl.BlockSpec((bb, BAND_PAST, d), bmap)] * 2
        args += [past_k, past_v]
    names = ["wkv_k", "wkv_v", "wq", "wo", "bias", "lng", "lnb", "wrt", "br"]
    in_specs += [_const_spec(w[n].shape) for n in names]
    args += [w[n] for n in names]
    outs = pl.pallas_call(
        kern,
        grid=(nb, nt),
        in_specs=in_specs,
        out_specs=[pl.BlockSpec((bb, tl, d), xmap),
                   pl.BlockSpec((rows * (d // LANES), LANES), lambda ib, it: (ib * nt + it, 0)),
                   pl.BlockSpec((1, 8, rows), rmap),
                   pl.BlockSpec((1, 8, rows), rmap),
                   pl.BlockSpec((bb, tl, d), bmap),
                   pl.BlockSpec((bb, tl, d), bmap)],
        out_shape=[jax.ShapeDtypeStruct((b, l, d), F32),
                   jax.ShapeDtypeStruct((b * l * (d // LANES), LANES), F32),
                   jax.ShapeDtypeStruct((nb * nt, 8, rows), jnp.int32),
                   jax.ShapeDtypeStruct((nb * nt, 8, rows), F32),
                   jax.ShapeDtypeStruct((b, keep, d), F32),
                   jax.ShapeDtypeStruct((b, keep, d), F32)],
        scratch_shapes=[pltpu.VMEM((bb, BAND_PAST + tl, d), BF16),
                        pltpu.VMEM((bb, BAND_PAST + tl, d), BF16),
                        pltpu.VMEM((rows, d), BF16),
                        pltpu.VMEM((rows, d), F32),
                        pltpu.VMEM((ATT_HEADS, c, BAND_PAST + c), F32),
                        pltpu.VMEM((ATT_HEADS, c, LANES), F32),
                        pltpu.VMEM((ATT_HEADS, c, BAND_PAST + c), BF16),
                        pltpu.VMEM((npr * (d // LANES), LANES), F32),
                        pltpu.SemaphoreType.DMA],
        compiler_params=_params(("arbitrary", "arbitrary")),
        name="attn_layer",
    )(*args)
    return outs


def _moe_kernel(be_ref, nact_ref, st_ref, stn_ref, sr_ref, hf_hbm, wg_ref, bg_ref, wu_ref, bu_ref,
                wd_ref, bd_ref, y_hbm, xbuf, ybuf, xb_s, hm_s, wgb, wub, wdb, gsem, ssem, *, blk, g,
                scatter):
    i = pl.program_id(0)
    nact = nact_ref[0]
    f = hm_s.shape[1]
    n_col = 4
    fc = f // n_col
    per = blk // n_col
    rc = min(blk, 256)

    def gather_row(tok_ref, j):
        src = pl.multiple_of(tok_ref[0, 0, j], g)
        return pltpu.make_async_copy(hf_hbm.at[pl.ds(src, g)], xbuf.at[pl.ds(j * g, g)], gsem)

    def gather_all():
        return pltpu.make_async_copy(hf_hbm.at[pl.ds(0, blk * g)], xbuf, gsem)

    def scatter_all():
        return pltpu.make_async_copy(ybuf, y_hbm.at[pl.ds(y_hbm.shape[0] - blk * g, blk * g)], ssem)

    @pl.when(i == 0)
    def _():
        if scatter:
            ybuf[...] = jnp.zeros(ybuf.shape, F32)
            scatter_all().start()

        def prime(j, carry):
            src = pl.multiple_of(st_ref[0, 0, j], g)
            pltpu.make_async_copy(hf_hbm.at[pl.ds(src, g)],
                                  xbuf.at[pl.ds(pl.multiple_of(j * g, g), g)], gsem).start()
            return carry

        lax.fori_loop(0, blk, prime, 0)

    @pl.when(i < nact)
    def _():
        e = be_ref[i]
        prev = be_ref[jnp.maximum(i - 1, 0)]

        @pl.when((i == 0) | (e != prev))
        def _():
            wgb[...] = wg_ref[0, 0].astype(BF16)
            wub[...] = wu_ref[0, 0].astype(BF16)
            wdb[...] = wd_ref[0, 0].astype(BF16)

        gather_all().wait()
        for s in range(g):
            xb_s[:, s * LANES:(s + 1) * LANES] = xbuf[pl.ds(s, blk, stride=g), :].astype(BF16)
        for c in range(n_col):
            cols = slice(c * fc, (c + 1) * fc)
            a = jnp.minimum(_dot(xb_s[...], wgb[:, cols]) + bg_ref[0, 0][:, cols], SWIGLU_LIMIT)
            u = jnp.clip(_dot(xb_s[...], wub[:, cols]) + bu_ref[0, 0][:, cols], -SWIGLU_LIMIT, SWIGLU_LIMIT)
            hm_s[:, cols] = ((u + 1.0) * (a * jax.nn.sigmoid(SWIGLU_ALPHA * a))).astype(BF16)
            for j in range(c * per, (c + 1) * per):
                gather_row(stn_ref, j).start()
        if scatter:
            scatter_all().wait()
        ydst = ybuf if scatter else y_hbm
        for r0 in range(0, blk, rc):
            y = _dot(hm_s[r0:r0 + rc, :], wdb[...]) + bd_ref[0, 0]
            for s in range(g):
                ydst[pl.ds(r0 * g + s, rc, stride=g), :] = y[:, s * LANES:(s + 1) * LANES]
            if scatter:
                for j in range(r0, r0 + rc):
                    dst = pl.multiple_of(sr_ref[0, 0, j], g)
                    pltpu.make_async_copy(ybuf.at[pl.ds(j * g, g)], y_hbm.at[pl.ds(dst, g)], ssem).start()

        @pl.when(i == nact - 1)
        def _():
            if scatter:
                scatter_all().wait()
            gather_all().wait()

    if not scatter:
        @pl.when(i >= nact)
        def _():
            y_hbm[...] = jnp.zeros(y_hbm.shape, F32)


def _moe_experts(hf_tm, block_e, n_active, slot_t, slot_r, layer, wg, bg, wu, bu, wd, bd, blk, scatter):
    d = wg.shape[2]
    g = d // LANES
    t = hf_tm.shape[0] // g
    f = wg.shape[-1]
    n_blocks = block_e.shape[0]
    kern = functools.partial(_moe_kernel, blk=blk, g=g, scatter=scatter)
    wmap = lambda i, be, na: (layer, be[i], 0, 0)
    smap = lambda i, be, na: (i, 0, 0)
    nmap = lambda i, be, na: (jnp.minimum(i + 1, n_blocks - 1), 0, 0)
    grid_spec = pltpu.PrefetchScalarGridSpec(
        num_scalar_prefetch=2,
        grid=(n_blocks,),
        in_specs=[pl.BlockSpec((1, 1, blk), smap, memory_space=pltpu.SMEM),
                  pl.BlockSpec((1, 1, blk), nmap, memory_space=pltpu.SMEM),
                  pl.BlockSpec((1, 1, blk), smap, memory_space=pltpu.SMEM),
                  pl.BlockSpec(memory_space=pl.ANY),
                  pl.BlockSpec((1, 1, d, f), wmap), pl.BlockSpec((1, 1, 1, f), wmap),
                  pl.BlockSpec((1, 1, d, f), wmap), pl.BlockSpec((1, 1, 1, f), wmap),
                  pl.BlockSpec((1, 1, f, d), wmap), pl.BlockSpec((1, 1, 1, d), wmap)],
        out_specs=(pl.BlockSpec(memory_space=pl.ANY) if scatter else
                   pl.BlockSpec((blk * g, LANES), lambda i, be, na: (i, 0))),
        scratch_shapes=[pltpu.VMEM((blk * g, LANES), F32),
                        pltpu.VMEM((blk * g, LANES) if scatter else (8, LANES), F32),
                        pltpu.VMEM((blk, d), BF16), pltpu.VMEM((blk, f), BF16),
                        pltpu.VMEM((d, f), BF16), pltpu.VMEM((d, f), BF16), pltpu.VMEM((f, d), BF16),
                        pltpu.SemaphoreType.DMA, pltpu.SemaphoreType.DMA])
    st3 = (slot_t * g).reshape(n_blocks, 1, blk)
    depth = wg.shape[0]
    return pl.pallas_call(
        kern,
        grid_spec=grid_spec,
        out_shape=jax.ShapeDtypeStruct((((TOP_K * t + blk) if scatter else n_blocks * blk) * g, LANES), F32),
        compiler_params=_params(("arbitrary",)),
        name="moe_experts",
    )(block_e, n_active, st3, st3, (slot_r * g).reshape(n_blocks, 1, blk), hf_tm,
      wg, bg.reshape(depth, N_EXPERTS, 1, f), wu, bu.reshape(depth, N_EXPERTS, 1, f),
      wd, bd.reshape(depth, N_EXPERTS, 1, d))


def _route(ti8, blk, counts=None):
    nb, _, rows = ti8.shape
    t = nb * rows
    npair = t * TOP_K
    assert npair % blk == 0
    flat_e = ti8[:, :TOP_K, :].reshape(-1)
    experts = jnp.arange(N_EXPERTS, dtype=jnp.int32)
    if counts is None:
        counts = jnp.sum((flat_e[:, None] == experts[None, :]).astype(jnp.int32), axis=0)
    padded = (counts + blk - 1) // blk * blk
    pend = jnp.cumsum(padded)
    need = padded - counts
    pair_bits = 19
    assert npair <= 1 << pair_bits and blk <= 1 << pair_bits
    within = jnp.arange(blk, dtype=jnp.int32)[None, :]
    dummy_key = jnp.where(within < need[:, None],
                          (experts[:, None] << (pair_bits + 1)) | (1 << pair_bits) | within,
                          ((2 * N_EXPERTS) << (pair_bits + 1)) | (1 << pair_bits))
    real_key = (flat_e << (pair_bits + 1)) | jnp.arange(npair, dtype=jnp.int32)
    skey = lax.sort(jnp.concatenate([real_key, dummy_key.reshape(-1)]))
    n_blocks = npair // blk + N_EXPERTS
    real = ((skey >> pair_bits) & 1) == 0
    q = jnp.where(real, skey & ((1 << pair_bits) - 1), 0)
    tok = (q // (TOP_K * rows)) * rows + q % rows
    choice = (q // rows) % TOP_K
    slot_t = jnp.where(real, tok, 0)
    spare = npair + jnp.arange(n_blocks * blk, dtype=jnp.int32) % blk
    slot_r = jnp.where(real, choice * t + tok, spare)
    bstart = jnp.arange(n_blocks, dtype=jnp.int32) * blk
    block_e = jnp.minimum(jnp.sum((pend[None, :] <= bstart[:, None]).astype(jnp.int32), axis=1),
                          N_EXPERTS - 1)
    n_active = (pend[-1] // blk).astype(jnp.int32).reshape(1)
    return block_e, n_active, slot_t, slot_r, pend - padded


def _combine_kernel(x_ref, mod_ref, tg_ref, y0, y1, y2, y3, lng_ref, lnb_ref, o_ref):
    bb, tl, d = x_ref.shape
    rows = bb * tl
    sel = (lax.broadcasted_iota(jnp.int32, (8, LANES), 0) ==
           lax.broadcasted_iota(jnp.int32, (8, LANES), 1)).astype(F32)
    gt = lax.dot_general(tg_ref[0], sel, (((0,), (0,)), ((), ())), preferred_element_type=F32,
                         precision=lax.Precision.HIGHEST)
    ys = (y0, y1, y2, y3)
    terms = [gt[:, k:k + 1] * _load_token_major(ys[k], rows, d) for k in range(TOP_K)]
    ff = ((terms[0] + terms[1]) + (terms[2] + terms[3])).reshape(bb, tl, d)
    g_f = mod_ref[:, 5:6, :]
    o_ref[...] = _layer_norm(DEEPNORM_ALPHA * x_ref[...] + g_f * ff, lng_ref[...], lnb_ref[...])


def _combine_layer(x1, mod, tg8, y, lng, lnb, bb, tl):
    b, l, d = x1.shape
    nb, nt = b // bb, l // tl
    rows = bb * tl
    tb = (b * l) // rows
    ymaps = [functools.partial(lambda ib, it, k: (k * tb + ib * nt + it, 0), k=k) for k in range(TOP_K)]
    return pl.pallas_call(
        _combine_kernel,
        grid=(nb, nt),
        in_specs=[pl.BlockSpec((bb, tl, d), lambda ib, it: (ib, it, 0)),
                  pl.BlockSpec((bb, 8, d), lambda ib, it: (ib, 0, 0)),
                  pl.BlockSpec((1, 8, rows), lambda ib, it: (ib * nt + it, 0, 0))]
                 + [pl.BlockSpec((rows * (d // LANES), LANES), m) for m in ymaps]
                 + [_const_spec(lng.shape), _const_spec(lnb.shape)],
        out_specs=pl.BlockSpec((bb, tl, d), lambda ib, it: (ib, it, 0)),
        out_shape=jax.ShapeDtypeStruct((b, l, d), F32),
        compiler_params=_params(("arbitrary", "arbitrary")),
        name="moe_combine",
    )(x1, mod, tg8, y, y, y, y, lng, lnb)


def _moe_layer(x1, hf, ti8, tg8, mod, layer, moe_w, lng, lnb, blk, bb, tl):
    block_e, n_active, slot_t, slot_r, _ = _route(ti8, blk)
    y = _moe_experts(hf, block_e, n_active, slot_t, slot_r, layer, *moe_w, blk, True)
    return _combine_layer(x1, mod, tg8, y, lng, lnb, bb, tl)


def _moe_layer_sorted(hf, ti8, tr8, counts, layer, moe_w, blk):
    g = hf.shape[0] // (ti8.shape[0] * ti8.shape[2])
    block_e, n_active, slot_t, slot_r, pstart = _route(ti8, blk, counts)
    y = _moe_experts(hf, block_e, n_active, slot_t, slot_r, layer, *moe_w, blk, False)
    nb, _, rows = ti8.shape
    pos = (jnp.take(pstart, ti8[:, :TOP_K, :]) + tr8[:, :TOP_K, :]) * g
    return y, pos.reshape(nb, 1, TOP_K * rows)


def _rel_bias_table(rel_bias, c):
    w = BAND_PAST + c
    m = jnp.arange(w + c - 1)
    f = rel_bias[:, jnp.clip(BAND_PAST + c - 1 - m, -REL_CLIP, REL_CLIP) + REL_CLIP].astype(F32)
    return jnp.stack([f[:, c - 1 - t:c - 1 - t + w] for t in range(c)], axis=1)


def _trunk(x, mods, s0, past_k, past_v, wts, gla_bb, att_bb, blk):
    b, l, d = x.shape
    tl = min(TOKEN_TILE, l)
    c = min(CHUNK, l)
    row = lambda v: v.reshape(1, -1)

    def mod_of(layer):
        m = mods[layer].reshape(b, 6, d)
        return jnp.concatenate([m, jnp.zeros((b, 2, d), F32)], axis=1)

    def router_w(layer):
        return wts["moe_wr"][layer].T, wts["moe_br"][layer].reshape(N_EXPERTS, 1)

    moe_w = (wts["moe_wg"], wts["moe_bg"], wts["moe_wu"], wts["moe_bu"], wts["moe_wd"], wts["moe_bd"])

    wrt, br = router_w(0)
    gw = dict(wq=wts["gla_wq"][0].astype(BF16), wk=wts["gla_wk"][0].astype(BF16),
              wv=wts["gla_wv"][0].astype(BF16), wr=wts["gla_wr"][0].astype(BF16),
              wa1=wts["gla_wa1"][0].astype(BF16), wa2=wts["gla_wa2"][0].astype(BF16),
              ba=row(wts["gla_ba"][0]), ng=row(wts["gla_norm_g"][0]),
              wo=wts["gla_wo"][0].astype(BF16), lng=row(wts["ln_g"][0, 0]), lnb=row(wts["ln_b"][0, 0]),
              wrt=wrt, br=br)
    mod0 = mod_of(0)
    s0t = jnp.swapaxes(s0, -1, -2)
    assert gla_bb == att_bb or l // tl == 1
    x1, hf, ti8, tg8, tr8, cnt, sfin_t = _gla_layer(x, mod0, s0t, gw, gla_bb, tl)
    y_sorted, pos = _moe_layer_sorted(hf, ti8, tr8, cnt[:, 0].astype(jnp.int32), 0, moe_w, blk)
    s_fin = jnp.swapaxes(sfin_t, -1, -2)[None]

    wrt, br = router_w(1)
    aw = dict(wkv_k=wts["kv_wk"].astype(BF16), wkv_v=wts["kv_wv"].astype(BF16),
              wq=wts["att_wq"][0].astype(BF16), wo=wts["att_wo"][0].astype(BF16),
              bias=_rel_bias_table(wts["att_rel_bias"][0], c),
              lng=row(wts["ln_g"][1, 0]), lnb=row(wts["ln_b"][1, 0]), wrt=wrt, br=br)
    mod1 = mod_of(1)
    n_att = (b // att_bb) * (l // tl)
    tg8 = tg8.reshape(ti8.shape[0], 8, -1, att_bb * tl).transpose(0, 2, 1, 3).reshape(n_att, 8, att_bb * tl)
    pos = pos.reshape(ti8.shape[0], TOP_K, -1, att_bb * tl).transpose(0, 2, 1, 3).reshape(n_att, 1, -1)
    x1, hf, ti8, tg8, k_new, v_new = _attn_layer(x1, mod1, mod0, tg8, pos, y_sorted,
                                                 row(wts["ln_g"][0, 1]), row(wts["ln_b"][0, 1]),
                                                 past_k, past_v, aw, att_bb, tl)
    x = _moe_layer(x1, hf, ti8, tg8, mod1, 1, moe_w, row(wts["ln_g"][1, 1]), row(wts["ln_b"][1, 1]),
                   blk, att_bb, tl)
    keep = k_new.shape[1]
    hd = d // ATT_HEADS
    return (x, s_fin, k_new.reshape(b, keep, ATT_HEADS, hd), v_new.reshape(b, keep, ATT_HEADS, hd))


def kernel(x_prompt, x_sample, c_prompt, c_sample, state_gla, cache_k, cache_v, ada_w, ada_b, ln_g, ln_b, gla_wq, gla_wk, gla_wv, gla_wa1, gla_wa2, gla_ba, gla_wr, gla_norm_g, gla_wo, kv_wk, kv_wv, att_wq, att_wo, att_rel_bias, moe_wr, moe_br, moe_wg, moe_bg, moe_wu, moe_bu, moe_wd, moe_bd):
    wts = dict(ln_g=ln_g, ln_b=ln_b, gla_wq=gla_wq, gla_wk=gla_wk, gla_wv=gla_wv, gla_wa1=gla_wa1,
               gla_wa2=gla_wa2, gla_ba=gla_ba, gla_wr=gla_wr, gla_norm_g=gla_norm_g, gla_wo=gla_wo,
               kv_wk=kv_wk, kv_wv=kv_wv, att_wq=att_wq, att_wo=att_wo, att_rel_bias=att_rel_bias,
               moe_wr=moe_wr, moe_br=moe_br, moe_wg=moe_wg, moe_bg=moe_bg, moe_wu=moe_wu,
               moe_bu=moe_bu, moe_wd=moe_wd, moe_bd=moe_bd)
    b_p, l_p, d = x_prompt.shape
    b_s, l_s, _ = x_sample.shape
    assert cache_k.shape[1] == BAND_PAST
    mods = _ada_modulation(jnp.concatenate([c_prompt, c_sample], axis=0), ada_w, ada_b)
    mods_p, mods_s = mods[:, :b_p], mods[:, b_p:]

    s0_p = jnp.zeros((b_p,) + state_gla.shape[2:], F32)
    blk_p = MOE_BLOCK_LARGE if b_p * l_p * TOP_K >= 64 * MOE_BLOCK_LARGE else MOE_BLOCK_SMALL
    y_p, s_p, k_p, v_p = _trunk(x_prompt, mods_p, s0_p, None, None, wts, 1, 1, blk_p)

    pk = cache_k.reshape(b_s, BAND_PAST, d).astype(BF16)
    pv = cache_v.reshape(b_s, BAND_PAST, d).astype(BF16)
    blk_s = MOE_BLOCK_LARGE if b_s * l_s * TOP_K >= 64 * MOE_BLOCK_LARGE else MOE_BLOCK_SMALL
    y_s, s_s, k_s, v_s = _trunk(x_sample, mods_s, state_gla[0], pk, pv, wts, b_s, min(4, b_s), blk_s)
    return (y_p, y_s, s_p, k_p, v_p, s_s, k_s, v_s)
```

```python
import functools
import math

import jax
import jax.numpy as jnp
from jax import lax
from jax.experimental import pallas as pl
from jax.experimental.pallas import tpu as pltpu

F32 = jnp.float32
BF16 = jnp.bfloat16

CHUNK = 64
GLA_HEADS = 4
GLA_TAU = 16.0
ATT_HEADS = 16
BAND_PAST = 512
REL_CLIP = 128
N_EXPERTS = 32
TOP_K = 4
SWIGLU_LIMIT = 7.0
SWIGLU_ALPHA = 1.702
DEPTH = 2
DEEPNORM_ALPHA = (2.0 * DEPTH) ** 0.25
LN_EPS = 1e-5
NEG_INF = -1e30

V7X_VMEM_LIMIT_BYTES = 56 * 1024 * 1024
TOKEN_TILE = 512
GLA_SUB = 16
MOE_BLOCK_LARGE = 512
MOE_BLOCK_SMALL = 64
LANES = 128


def _dot(a, b):
    return jnp.dot(a, b, preferred_element_type=F32)


def _dot_nt(a, b):
    return lax.dot_general(a, b, (((1,), (1,)), ((), ())), preferred_element_type=F32)


def _dot_tn(a, b):
    return lax.dot_general(a, b, (((0,), (0,)), ((), ())), preferred_element_type=F32)


def _layer_norm(y, g, b):
    mu = jnp.mean(y, axis=-1, keepdims=True)
    yc = y - mu
    var = jnp.mean(yc * yc, axis=-1, keepdims=True)
    return yc * lax.rsqrt(var + LN_EPS) * g + b


def _const_spec(shape):
    nd = len(shape)
    return pl.BlockSpec(shape, lambda *_: (0,) * nd, pipeline_mode=pl.Buffered(1))


def _params(sem):
    return pltpu.CompilerParams(dimension_semantics=sem, vmem_limit_bytes=V7X_VMEM_LIMIT_BYTES)


def _ada_kernel(c_ref, w_ref, b_ref, o_ref):
    c = c_ref[...]
    s = c * jax.nn.sigmoid(c)
    o_ref[0] = jnp.dot(s, w_ref[0], preferred_element_type=F32,
                       precision=lax.Precision.HIGHEST) + b_ref[0]


def _ada_modulation(c, ada_w, ada_b):
    n, d = c.shape
    depth, _, d6 = ada_w.shape
    tn = d6 // 4
    return pl.pallas_call(
        _ada_kernel,
        grid=(depth, d6 // tn),
        in_specs=[pl.BlockSpec((n, d), lambda l, j: (0, 0)),
                  pl.BlockSpec((1, d, tn), lambda l, j: (l, 0, j)),
                  pl.BlockSpec((1, 1, tn), lambda l, j: (l, 0, j))],
        out_specs=pl.BlockSpec((1, n, tn), lambda l, j: (l, 0, j)),
        out_shape=jax.ShapeDtypeStruct((depth, n, d6), F32),
        compiler_params=_params(("arbitrary", "arbitrary")),
        name="ada_modulation",
    )(c, ada_w, ada_b.reshape(depth, 1, d6))


def _pair_ranks(onehots, cnt_s):
    e, rows = onehots[0].shape
    before = (lax.broadcasted_iota(jnp.int32, (rows, rows), 0) <
              lax.broadcasted_iota(jnp.int32, (rows, rows), 1)).astype(BF16)
    prefix = _dot(jnp.concatenate(onehots, axis=0).astype(BF16), before)
    base = cnt_s[:, 0:1]
    ranks = []
    for k, oh in enumerate(onehots):
        ranks.append(jnp.sum(oh * (base + prefix[k * e:(k + 1) * e]), axis=0, keepdims=True))
        base = base + jnp.sum(oh, axis=1, keepdims=True)
    cnt_s[...] = jnp.broadcast_to(base, cnt_s.shape)
    return ranks


def _post_mixer(x3, mix3, g_m, sh_f, sc_f, lng, lnb, wrt, br, cnt_s=None):
    bb, tl, d = x3.shape
    rows = bb * tl
    x1 = _layer_norm(DEEPNORM_ALPHA * x3 + g_m * mix3, lng, lnb)
    hf = x1 * (1.0 + sc_f) + sh_f
    logits = lax.dot_general(wrt, hf.reshape(rows, d), (((1,), (1,)), ((), ())),
                             preferred_element_type=F32, precision=lax.Precision.HIGHEST) + br
    eidx = lax.broadcasted_iota(jnp.int32, logits.shape, 0).astype(F32)
    vals, idxs = [], []
    cur = logits
    for _ in range(TOP_K):
        m = jnp.max(cur, axis=0, keepdims=True)
        sel = jnp.min(jnp.where(cur == m, eidx, float(N_EXPERTS)), axis=0, keepdims=True)
        vals.append(m)
        idxs.append(sel)
        cur = jnp.where(eidx == sel, -jnp.inf, cur)
    ex = [jnp.exp(v - vals[0]) for v in vals]
    den = (ex[0] + ex[1]) + (ex[2] + ex[3])
    out_row = lax.broadcasted_iota(jnp.int32, (8, rows), 0)
    ti = jnp.zeros((8, rows), F32)
    tg = jnp.zeros((8, rows), F32)
    for k in range(TOP_K):
        ti = jnp.where(out_row == k, idxs[k], ti)
        tg = jnp.where(out_row == k, ex[k] / den, tg)
    if cnt_s is None:
        return x1, hf, ti.astype(jnp.int32), tg
    ranks = _pair_ranks([(eidx == idxs[k]).astype(F32) for k in range(TOP_K)], cnt_s)
    tr = jnp.zeros((8, rows), F32)
    for k in range(TOP_K):
        tr = jnp.where(out_row == k, ranks[k], tr)
    return x1, hf, ti.astype(jnp.int32), tg, tr.astype(jnp.int32)


def _gla_kernel(x_ref, mod_ref, s0_ref, wq_ref, wk_ref, wv_ref, wr_ref, wa1_ref, wa2_ref, ba_ref,
                ng_ref, wo_ref, lng_ref, lnb_ref, wrt_ref, br_ref,
                x1_ref, hf_ref, ti_ref, tg_ref, tr_ref, cnt_ref, sfin_ref,
                st_s, q_s, k_s, v_s, la_s, o_s, cnt_s, *, bb, tl, c):
    i = pl.program_id(1)
    d = x_ref.shape[-1]
    dk = q_s.shape[-1] // GLA_HEADS
    dv = v_s.shape[-1] // GLA_HEADS
    rows = bb * tl
    sub = min(GLA_SUB, c)

    @pl.when(i == 0)
    def _():
        st_s[...] = s0_ref[...]

    @pl.when((i == 0) & (pl.program_id(0) == 0))
    def _():
        cnt_s[...] = jnp.zeros(cnt_s.shape, F32)

    x3 = x_ref[...]
    mod = mod_ref[...]
    h = (x3 * (1.0 + mod[:, 1:2, :]) + mod[:, 0:1, :]).reshape(rows, d)
    hb = h.astype(BF16)
    q_s[...] = _dot(hb, wq_ref[...]) * (dk ** -0.5)
    k_s[...] = _dot(hb, wk_ref[...])
    v_s[...] = _dot(hb, wv_ref[...])
    a = _dot(_dot(hb, wa1_ref[...]).astype(BF16), wa2_ref[...]) + ba_ref[...]
    la_s[...] = (jnp.minimum(a, 0.0) - jnp.log1p(jnp.exp(-jnp.abs(a)))) * (1.0 / GLA_TAU)

    tril = (lax.broadcasted_iota(jnp.int32, (c, c), 0) >=
            lax.broadcasted_iota(jnp.int32, (c, c), 1)).astype(BF16)
    lane_c = lax.broadcasted_iota(jnp.int32, (sub, c), 1)
    row_c = lax.broadcasted_iota(jnp.int32, (sub, c), 0)
    krow = lax.broadcasted_iota(jnp.int32, (c, dk), 0)
    ng = ng_ref[...]
    chunks_per_seq = tl // c

    def chunk_body(n, carry):
        r0 = pl.multiple_of(n * c, c)
        bi = n // chunks_per_seq
        la = la_s[pl.ds(r0, c), :]
        la_hi = la.astype(BF16)
        la_lo = (la - la_hi.astype(F32)).astype(BF16)
        bcum = _dot(tril, la_hi) + _dot(tril, la_lo)
        for hd in range(GLA_HEADS):
            qh = q_s[pl.ds(r0, c), hd * dk:(hd + 1) * dk]
            kh = k_s[pl.ds(r0, c), hd * dk:(hd + 1) * dk]
            vh = v_s[pl.ds(r0, c), hd * dv:(hd + 1) * dv].astype(BF16)
            bh = bcum[:, hd * dk:(hd + 1) * dk]
            blocks = []
            for sb in range(c // sub):
                lo = sb * sub
                qi = qh[lo:lo + sub]
                bi_rows = bh[lo:lo + sub]
                if sb > 0:
                    ref_b = bh[lo:lo + 1]
                    qt = (qi * jnp.exp(bi_rows - ref_b)).astype(BF16)
                    kt = jnp.where(krow < lo, kh * jnp.exp(jnp.minimum(ref_b - bh, 0.0)), 0.0)
                    acc = _dot_nt(qt, kt.astype(BF16))
                else:
                    acc = jnp.zeros((sub, c), F32)
                for s in range(sub):
                    g = lo + s
                    e = jnp.exp(jnp.minimum(bi_rows - bh[g:g + 1], 0.0))
                    col = jnp.sum(qi * e * kh[g:g + 1], axis=-1, keepdims=True)
                    acc = jnp.where((lane_c == g) & (row_c >= s), col, acc)
                blocks.append(acc)
            amat = blocks[0] if len(blocks) == 1 else jnp.concatenate(blocks, axis=0)
            st = st_s[bi, hd]
            o = _dot(amat.astype(BF16), vh) + _dot_nt((qh * jnp.exp(bh)).astype(BF16), st.astype(BF16))
            ms = jnp.mean(o * o, axis=-1, keepdims=True)
            o_s[pl.ds(r0, c), hd * dv:(hd + 1) * dv] = o * lax.rsqrt(ms + LN_EPS) * ng
            bl = bh[c - 1:c]
            kd = (kh * jnp.exp(bl - bh)).astype(BF16)
            st_s[bi, hd] = st * jnp.exp(bl) + _dot_tn(vh, kd)
        return carry

    lax.fori_loop(0, rows // c, chunk_body, 0)

    r = _dot(hb, wr_ref[...])
    og = (o_s[...] * (r * jax.nn.sigmoid(r))).astype(BF16)
    mix = _dot(og, wo_ref[...]).reshape(bb, tl, d)
    x1, hf, ti, tg, tr = _post_mixer(x3, mix, mod[:, 2:3, :], mod[:, 3:4, :], mod[:, 4:5, :],
                                     lng_ref[...], lnb_ref[...], wrt_ref[...], br_ref[...], cnt_s)
    x1_ref[...] = x1
    hf_ref[...] = hf
    ti_ref[0] = ti
    tg_ref[0] = tg
    tr_ref[0] = tr
    cnt_ref[...] = cnt_s[...]

    @pl.when(i == pl.num_programs(1) - 1)
    def _():
        sfin_ref[...] = st_s[...]


def _gla_layer(x, mod, s0t, w, bb, tl):
    b, l, d = x.shape
    c = min(CHUNK, l)
    hk = w["wq"].shape[1]
    hv = w["wv"].shape[1]
    nb, nt = b // bb, l // tl
    assert bb == 1 or nt == 1
    rows = bb * tl
    dv, dk = s0t.shape[2], s0t.shape[3]
    kern = functools.partial(_gla_kernel, bb=bb, tl=tl, c=c)
    xmap = lambda ib, it: (ib, it, 0)
    bmap = lambda ib, it: (ib, 0, 0)
    smap = lambda ib, it: (ib, 0, 0, 0)
    rmap = lambda ib, it: (ib * nt + it, 0, 0)
    outs = pl.pallas_call(
        kern,
        grid=(nb, nt),
        in_specs=[pl.BlockSpec((bb, tl, d), xmap),
                  pl.BlockSpec((bb, 8, d), bmap),
                  pl.BlockSpec((bb, GLA_HEADS, dv, dk), smap),
                  _const_spec(w["wq"].shape), _const_spec(w["wk"].shape), _const_spec(w["wv"].shape),
                  _const_spec(w["wr"].shape), _const_spec(w["wa1"].shape), _const_spec(w["wa2"].shape),
                  _const_spec(w["ba"].shape), _const_spec(w["ng"].shape), _const_spec(w["wo"].shape),
                  _const_spec(w["lng"].shape), _const_spec(w["lnb"].shape),
                  _const_spec(w["wrt"].shape), _const_spec(w["br"].shape)],
        out_specs=[pl.BlockSpec((bb, tl, d), xmap),
                   pl.BlockSpec((bb, tl, d), xmap),
                   pl.BlockSpec((1, 8, rows), rmap),
                   pl.BlockSpec((1, 8, rows), rmap),
                   pl.BlockSpec((1, 8, rows), rmap),
                   pl.BlockSpec((N_EXPERTS, LANES), lambda ib, it: (0, 0)),
                   pl.BlockSpec((bb, GLA_HEADS, dv, dk), smap)],
        out_shape=[jax.ShapeDtypeStruct((b, l, d), F32),
                   jax.ShapeDtypeStruct((b, l, d), F32),
                   jax.ShapeDtypeStruct((nb * nt, 8, rows), jnp.int32),
                   jax.ShapeDtypeStruct((nb * nt, 8, rows), F32),
                   jax.ShapeDtypeStruct((nb * nt, 8, rows), jnp.int32),
                   jax.ShapeDtypeStruct((N_EXPERTS, LANES), F32),
                   jax.ShapeDtypeStruct(s0t.shape, F32)],
        scratch_shapes=[pltpu.VMEM((bb, GLA_HEADS, dv, dk), F32),
                        pltpu.VMEM((rows, hk), F32), pltpu.VMEM((rows, hk), F32),
                        pltpu.VMEM((rows, hv), F32), pltpu.VMEM((rows, hk), F32),
                        pltpu.VMEM((rows, hv), F32), pltpu.VMEM((N_EXPERTS, LANES), F32)],
        compiler_params=_params(("arbitrary", "arbitrary")),
        name="gla_layer",
    )(x, mod, s0t, w["wq"], w["wk"], w["wv"], w["wr"], w["wa1"], w["wa2"], w["ba"], w["ng"],
      w["wo"], w["lng"], w["lnb"], w["wrt"], w["br"])
    return outs


def _attn_kernel(*refs, bb, tl, c, has_past):
    it_refs = iter(refs)
    nxt = lambda n: [next(it_refs) for _ in range(n)]
    xp_ref, mod_ref, modp_ref, tgp_ref, pos_ref, posn_ref, y_hbm, lngp_ref, lnbp_ref = nxt(9)
    pk_ref, pv_ref = nxt(2) if has_past else (None, None)
    wkv_k_ref, wkv_v_ref, wq_ref, wo_ref, bias_ref, lng_ref, lnb_ref, wrt_ref, br_ref = nxt(9)
    x1_ref, hf_ref, ti_ref, tg_ref, ko_ref, vo_ref = nxt(6)
    kw_s, vw_s, q_s, o_s, sc_s, m_s, e_s, ygb, ysem = nxt(9)
    i = pl.program_id(1)
    gstep = pl.program_id(0) * pl.num_programs(1) + i
    last_step = pl.num_programs(0) * pl.num_programs(1) - 1
    d = xp_ref.shape[-1]
    p = BAND_PAST
    w = p + c
    rows = bb * tl
    npr = TOP_K * rows
    hd2 = 2 * (d // ATT_HEADS)

    def y_row(tab_ref, j):
        return pltpu.make_async_copy(y_hbm.at[pl.ds(tab_ref[0, 0, j], 1)], ygb.at[pl.ds(j, 1)], ysem)

    def y_all():
        return pltpu.make_async_copy(y_hbm.at[pl.ds(0, npr)], ygb, ysem)

    @pl.when(gstep == 0)
    def _():
        def prime(j, carry):
            y_row(pos_ref, j).start()
            return carry

        lax.fori_loop(0, npr, prime, 0)

    y_all().wait()
    sel = (lax.broadcasted_iota(jnp.int32, (8, LANES), 0) ==
           lax.broadcasted_iota(jnp.int32, (8, LANES), 1)).astype(F32)
    gt = lax.dot_general(tgp_ref[0], sel, (((0,), (0,)), ((), ())), preferred_element_type=F32,
                         precision=lax.Precision.HIGHEST)
    terms = [gt[:, k:k + 1] * ygb[k * rows:(k + 1) * rows, :] for k in range(TOP_K)]
    ff = ((terms[0] + terms[1]) + (terms[2] + terms[3])).reshape(bb, tl, d)
    x3 = _layer_norm(DEEPNORM_ALPHA * xp_ref[...] + modp_ref[:, 5:6, :] * ff, lngp_ref[...], lnbp_ref[...])

    @pl.when(i == 0)
    def _():
        if has_past:
            kw_s[:, 0:p, :] = pk_ref[...]
            vw_s[:, 0:p, :] = pv_ref[...]
        else:
            kw_s[:, 0:p, :] = jnp.zeros((bb, p, d), BF16)
            vw_s[:, 0:p, :] = jnp.zeros((bb, p, d), BF16)

    mod = mod_ref[...]
    xb = x3.reshape(rows, d).astype(BF16)
    hb = (x3 * (1.0 + mod[:, 1:2, :]) + mod[:, 0:1, :]).reshape(rows, d).astype(BF16)
    kn = _dot(xb, wkv_k_ref[...])
    vn = _dot(xb, wkv_v_ref[...])
    ko_ref[...] = kn.reshape(bb, tl, d)
    vo_ref[...] = vn.reshape(bb, tl, d)
    kw_s[:, p:p + tl, :] = kn.astype(BF16).reshape(bb, tl, d)
    vw_s[:, p:p + tl, :] = vn.astype(BF16).reshape(bb, tl, d)
    q_s[...] = (_dot(hb, wq_ref[...]) * ((d // ATT_HEADS) ** -0.5)).astype(BF16)

    lane = lax.broadcasted_iota(jnp.int32, (c, hd2), 1)
    low = lane < (hd2 // 2)
    kpos = lax.broadcasted_iota(jnp.int32, (1, w), 1)
    chunks_per_seq = tl // c
    rows_per_chunk = npr // (rows // c)

    def chunk_body(n, carry):
        for jj in range(rows_per_chunk):
            y_row(posn_ref, n * rows_per_chunk + jj).start()
        r0 = pl.multiple_of(n * c, c)
        bi = n // chunks_per_seq
        ci = n - bi * chunks_per_seq
        w0 = pl.multiple_of(ci * c, c)
        if not has_past:
            valid = (kpos + (i * tl + ci * c)) >= p
        for hp in range(ATT_HEADS // 2):
            qp = q_s[pl.ds(r0, c), hp * hd2:(hp + 1) * hd2]
            kwin = kw_s[bi, pl.ds(w0, w), hp * hd2:(hp + 1) * hd2]
            for half in range(2):
                qm = jnp.where(low if half == 0 else ~low, qp, jnp.zeros_like(qp))
                s = _dot_nt(qm, kwin) + bias_ref[2 * hp + half]
                if not has_past:
                    s = jnp.where(valid, s, NEG_INF)
                sc_s[2 * hp + half] = s
        for hh in range(ATT_HEADS):
            m_s[hh] = jnp.broadcast_to(jnp.max(sc_s[hh], axis=-1, keepdims=True), (c, LANES))
        for hh in range(ATT_HEADS):
            s = sc_s[hh]
            m = m_s[hh]
            parts = [jnp.exp(s[:, j * LANES:(j + 1) * LANES] - m) for j in range(w // LANES)]
            if w % LANES:
                parts.append(jnp.exp(s[:, w - w % LANES:] - m[:, :w % LANES]))
            e_s[hh] = jnp.concatenate(parts, axis=1).astype(BF16)
        ones = jnp.ones((w, hd2), BF16)
        for hp in range(ATT_HEADS // 2):
            vext = jnp.concatenate([vw_s[bi, pl.ds(w0, w), hp * hd2:(hp + 1) * hd2], ones], axis=1)
            outs = []
            for half in range(2):
                r = _dot(e_s[2 * hp + half], vext)
                outs.append(r[:, :hd2] / r[:, hd2:])
            o_s[pl.ds(r0, c), hp * hd2:(hp + 1) * hd2] = jnp.where(low, outs[0], outs[1])
        return carry

    lax.fori_loop(0, rows // c, chunk_body, 0)

    @pl.when(gstep == last_step)
    def _():
        y_all().wait()

    if tl >= p:
        kw_s[:, 0:p, :] = kw_s[:, tl:tl + p, :]
        vw_s[:, 0:p, :] = vw_s[:, tl:tl + p, :]

    mix = _dot(o_s[...].astype(BF16), wo_ref[...]).reshape(bb, tl, d)
    x1, hf, ti, tg = _post_mixer(x3, mix, mod[:, 2:3, :], mod[:, 3:4, :], mod[:, 4:5, :],
                                 lng_ref[...], lnb_ref[...], wrt_ref[...], br_ref[...])
    x1_ref[...] = x1
    hf_ref[...] = hf
    ti_ref[0] = ti
    tg_ref[0] = tg


def _attn_layer(xp, mod, modp, tgp8, pos, y_sorted, lngp, lnbp, past_k, past_v, w, bb, tl):
    x = xp
    b, l, d = x.shape
    c = min(CHUNK, l)
    has_past = past_k is not None
    nb, nt = b // bb, l // tl
    assert nt == 1 or (tl >= BAND_PAST and bb == 1)
    rows = bb * tl
    keep = min(BAND_PAST, l)
    assert keep == tl or (keep == BAND_PAST and tl == BAND_PAST)
    kern = functools.partial(_attn_kernel, bb=bb, tl=tl, c=c, has_past=has_past)
    xmap = lambda ib, it: (ib, it, 0)
    bmap = lambda ib, it: (ib, 0, 0)
    rmap = lambda ib, it: (ib * nt + it, 0, 0)
    npr = TOP_K * rows
    nsteps = nb * nt
    in_specs = [pl.BlockSpec((bb, tl, d), xmap), pl.BlockSpec((bb, 8, d), bmap),
                pl.BlockSpec((bb, 8, d), bmap), pl.BlockSpec((1, 8, rows), rmap),
                pl.BlockSpec((1, 1, npr), rmap, memory_space=pltpu.SMEM),
                pl.BlockSpec((1, 1, npr), lambda ib, it: (jnp.minimum(ib * nt + it + 1, nsteps - 1), 0, 0),
                             memory_space=pltpu.SMEM),
                pl.BlockSpec(memory_space=pl.ANY),
                _const_spec(lngp.shape), _const_spec(lnbp.shape)]
    args = [x, mod, modp, tgp8, pos, pos, y_sorted, lngp, lnbp]
    if has_past:
        in_specs += [pl.BlockSpec((bb, BAND_PAST, d), bmap)] * 2
        args += [past_k, past_v]
    names = ["wkv_k", "wkv_v", "wq", "wo", "bias", "lng", "lnb", "wrt", "br"]
    in_specs += [_const_spec(w[n].shape) for n in names]
    args += [w[n] for n in names]
    outs = pl.pallas_call(
        kern,
        grid=(nb, nt),
        in_specs=in_specs,
        out_specs=[pl.BlockSpec((bb, tl, d), xmap),
                   pl.BlockSpec((bb, tl, d), xmap),
                   pl.BlockSpec((1, 8, rows), rmap),
                   pl.BlockSpec((1, 8, rows), rmap),
                   pl.BlockSpec((bb, tl, d), bmap),
                   pl.BlockSpec((bb, tl, d), bmap)],
        out_shape=[jax.ShapeDtypeStruct((b, l, d), F32),
                   jax.ShapeDtypeStruct((b, l, d), F32),
                   jax.ShapeDtypeStruct((nb * nt, 8, rows), jnp.int32),
                   jax.ShapeDtypeStruct((nb * nt, 8, rows), F32),
                   jax.ShapeDtypeStruct((b, keep, d), F32),
                   jax.ShapeDtypeStruct((b, keep, d), F32)],
        scratch_shapes=[pltpu.VMEM((bb, BAND_PAST + tl, d), BF16),
                        pltpu.VMEM((bb, BAND_PAST + tl, d), BF16),
                        pltpu.VMEM((rows, d), BF16),
                        pltpu.VMEM((rows, d), F32),
                        pltpu.VMEM((ATT_HEADS, c, BAND_PAST + c), F32),
                        pltpu.VMEM((ATT_HEADS, c, LANES), F32),
                        pltpu.VMEM((ATT_HEADS, c, BAND_PAST + c), BF16),
                        pltpu.VMEM((npr, d), F32),
                        pltpu.SemaphoreType.DMA],
        compiler_params=_params(("arbitrary", "arbitrary")),
        name="attn_layer",
    )(*args)
    return outs


def _moe_kernel(be_ref, nact_ref, st_ref, stn_ref, sr_ref, hf_hbm, wg_ref, bg_ref, wu_ref, bu_ref,
                wd_ref, bd_ref, y_hbm, xbuf, ybuf, xb_s, hm_s, wgb, wub, wdb, gsem, ssem, *, blk, scatter):
    i = pl.program_id(0)
    nact = nact_ref[0]
    f = hm_s.shape[1]
    n_col = 4
    fc = f // n_col
    per = blk // n_col
    rc = min(blk, 256)

    def gather_row(tok_ref, j):
        return pltpu.make_async_copy(hf_hbm.at[pl.ds(tok_ref[0, 0, j], 1)], xbuf.at[pl.ds(j, 1)], gsem)

    def gather_all():
        return pltpu.make_async_copy(hf_hbm.at[pl.ds(0, blk)], xbuf, gsem)

    def scatter_all():
        return pltpu.make_async_copy(ybuf, y_hbm.at[pl.ds(y_hbm.shape[0] - blk, blk)], ssem)

    @pl.when(i == 0)
    def _():
        if scatter:
            ybuf[...] = jnp.zeros(ybuf.shape, F32)
            scatter_all().start()

        def prime(j, carry):
            gather_row(st_ref, j).start()
            return carry

        lax.fori_loop(0, blk, prime, 0)

    @pl.when(i < nact)
    def _():
        e = be_ref[i]
        prev = be_ref[jnp.maximum(i - 1, 0)]

        @pl.when((i == 0) | (e != prev))
        def _():
            wgb[...] = wg_ref[0, 0].astype(BF16)
            wub[...] = wu_ref[0, 0].astype(BF16)
            wdb[...] = wd_ref[0, 0].astype(BF16)

        gather_all().wait()
        xb_s[...] = xbuf[...].astype(BF16)
        for c in range(n_col):
            cols = slice(c * fc, (c + 1) * fc)
            a = jnp.minimum(_dot(xb_s[...], wgb[:, cols]) + bg_ref[0, 0][:, cols], SWIGLU_LIMIT)
            u = jnp.clip(_dot(xb_s[...], wub[:, cols]) + bu_ref[0, 0][:, cols], -SWIGLU_LIMIT, SWIGLU_LIMIT)
            hm_s[:, cols] = ((u + 1.0) * (a * jax.nn.sigmoid(SWIGLU_ALPHA * a))).astype(BF16)
            for j in range(c * per, (c + 1) * per):
                gather_row(stn_ref, j).start()
        if scatter:
            scatter_all().wait()
        ydst = ybuf if scatter else y_hbm
        for r0 in range(0, blk, rc):
            ydst[r0:r0 + rc, :] = _dot(hm_s[r0:r0 + rc, :], wdb[...]) + bd_ref[0, 0]
            if scatter:
                for j in range(r0, r0 + rc):
                    pltpu.make_async_copy(ybuf.at[pl.ds(j, 1)], y_hbm.at[pl.ds(sr_ref[0, 0, j], 1)], ssem).start()

        @pl.when(i == nact - 1)
        def _():
            if scatter:
                scatter_all().wait()
            gather_all().wait()

    if not scatter:
        @pl.when(i >= nact)
        def _():
            y_hbm[...] = jnp.zeros(y_hbm.shape, F32)


def _moe_experts(hf2d, block_e, n_active, slot_t, slot_r, layer, wg, bg, wu, bu, wd, bd, blk, scatter):
    t, d = hf2d.shape
    f = wg.shape[-1]
    n_blocks = block_e.shape[0]
    kern = functools.partial(_moe_kernel, blk=blk, scatter=scatter)
    wmap = lambda i, be, na: (layer, be[i], 0, 0)
    smap = lambda i, be, na: (i, 0, 0)
    nmap = lambda i, be, na: (jnp.minimum(i + 1, n_blocks - 1), 0, 0)
    grid_spec = pltpu.PrefetchScalarGridSpec(
        num_scalar_prefetch=2,
        grid=(n_blocks,),
        in_specs=[pl.BlockSpec((1, 1, blk), smap, memory_space=pltpu.SMEM),
                  pl.BlockSpec((1, 1, blk), nmap, memory_space=pltpu.SMEM),
                  pl.BlockSpec((1, 1, blk), smap, memory_space=pltpu.SMEM),
                  pl.BlockSpec(memory_space=pl.ANY),
                  pl.BlockSpec((1, 1, d, f), wmap), pl.BlockSpec((1, 1, 1, f), wmap),
                  pl.BlockSpec((1, 1, d, f), wmap), pl.BlockSpec((1, 1, 1, f), wmap),
                  pl.BlockSpec((1, 1, f, d), wmap), pl.BlockSpec((1, 1, 1, d), wmap)],
        out_specs=(pl.BlockSpec(memory_space=pl.ANY) if scatter else
                   pl.BlockSpec((blk, d), lambda i, be, na: (i, 0))),
        scratch_shapes=[pltpu.VMEM((blk, d), F32),
                        pltpu.VMEM((blk, d) if scatter else (8, LANES), F32),
                        pltpu.VMEM((blk, d), BF16), pltpu.VMEM((blk, f), BF16),
                        pltpu.VMEM((d, f), BF16), pltpu.VMEM((d, f), BF16), pltpu.VMEM((f, d), BF16),
                        pltpu.SemaphoreType.DMA, pltpu.SemaphoreType.DMA])
    st3 = slot_t.reshape(n_blocks, 1, blk)
    depth = wg.shape[0]
    return pl.pallas_call(
        kern,
        grid_spec=grid_spec,
        out_shape=jax.ShapeDtypeStruct(((TOP_K * t + blk) if scatter else n_blocks * blk, d), F32),
        compiler_params=_params(("arbitrary",)),
        name="moe_experts",
    )(block_e, n_active, st3, st3, slot_r.reshape(n_blocks, 1, blk), hf2d,
      wg, bg.reshape(depth, N_EXPERTS, 1, f), wu, bu.reshape(depth, N_EXPERTS, 1, f),
      wd, bd.reshape(depth, N_EXPERTS, 1, d))


def _route(ti8, blk, counts=None):
    nb, _, rows = ti8.shape
    t = nb * rows
    npair = t * TOP_K
    assert npair % blk == 0
    flat_e = ti8[:, :TOP_K, :].reshape(-1)
    experts = jnp.arange(N_EXPERTS, dtype=jnp.int32)
    if counts is None:
        counts = jnp.sum((flat_e[:, None] == experts[None, :]).astype(jnp.int32), axis=0)
    padded = (counts + blk - 1) // blk * blk
    pend = jnp.cumsum(padded)
    need = padded - counts
    pair_bits = 19
    assert npair <= 1 << pair_bits and blk <= 1 << pair_bits
    within = jnp.arange(blk, dtype=jnp.int32)[None, :]
    dummy_key = jnp.where(within < need[:, None],
                          (experts[:, None] << (pair_bits + 1)) | (1 << pair_bits) | within,
                          ((2 * N_EXPERTS) << (pair_bits + 1)) | (1 << pair_bits))
    real_key = (flat_e << (pair_bits + 1)) | jnp.arange(npair, dtype=jnp.int32)
    skey = lax.sort(jnp.concatenate([real_key, dummy_key.reshape(-1)]))
    n_blocks = npair // blk + N_EXPERTS
    real = ((skey >> pair_bits) & 1) == 0
    q = jnp.where(real, skey & ((1 << pair_bits) - 1), 0)
    tok = (q // (TOP_K * rows)) * rows + q % rows
    choice = (q // rows) % TOP_K
    slot_t = jnp.where(real, tok, 0)
    spare = npair + jnp.arange(n_blocks * blk, dtype=jnp.int32) % blk
    slot_r = jnp.where(real, choice * t + tok, spare)
    bstart = jnp.arange(n_blocks, dtype=jnp.int32) * blk
    block_e = jnp.minimum(jnp.sum((pend[None, :] <= bstart[:, None]).astype(jnp.int32), axis=1),
                          N_EXPERTS - 1)
    n_active = (pend[-1] // blk).astype(jnp.int32).reshape(1)
    return block_e, n_active, slot_t, slot_r, pend - padded


def _combine_kernel(x_ref, mod_ref, tg_ref, y0, y1, y2, y3, lng_ref, lnb_ref, o_ref):
    bb, tl, d = x_ref.shape
    rows = bb * tl
    sel = (lax.broadcasted_iota(jnp.int32, (8, LANES), 0) ==
           lax.broadcasted_iota(jnp.int32, (8, LANES), 1)).astype(F32)
    gt = lax.dot_general(tg_ref[0], sel, (((0,), (0,)), ((), ())), preferred_element_type=F32,
                         precision=lax.Precision.HIGHEST)
    ys = (y0, y1, y2, y3)
    terms = [gt[:, k:k + 1] * ys[k][...] for k in range(TOP_K)]
    ff = ((terms[0] + terms[1]) + (terms[2] + terms[3])).reshape(bb, tl, d)
    g_f = mod_ref[:, 5:6, :]
    o_ref[...] = _layer_norm(DEEPNORM_ALPHA * x_ref[...] + g_f * ff, lng_ref[...], lnb_ref[...])


def _combine_layer(x1, mod, tg8, y, lng, lnb, bb, tl):
    b, l, d = x1.shape
    nb, nt = b // bb, l // tl
    rows = bb * tl
    tb = (b * l) // rows
    ymaps = [functools.partial(lambda ib, it, k: (k * tb + ib * nt + it, 0), k=k) for k in range(TOP_K)]
    return pl.pallas_call(
        _combine_kernel,
        grid=(nb, nt),
        in_specs=[pl.BlockSpec((bb, tl, d), lambda ib, it: (ib, it, 0)),
                  pl.BlockSpec((bb, 8, d), lambda ib, it: (ib, 0, 0)),
                  pl.BlockSpec((1, 8, rows), lambda ib, it: (ib * nt + it, 0, 0))]
                 + [pl.BlockSpec((rows, d), m) for m in ymaps]
                 + [_const_spec(lng.shape), _const_spec(lnb.shape)],
        out_specs=pl.BlockSpec((bb, tl, d), lambda ib, it: (ib, it, 0)),
        out_shape=jax.ShapeDtypeStruct((b, l, d), F32),
        compiler_params=_params(("arbitrary", "arbitrary")),
        name="moe_combine",
    )(x1, mod, tg8, y, y, y, y, lng, lnb)


def _moe_layer(x1, hf, ti8, tg8, mod, layer, moe_w, lng, lnb, blk, bb, tl):
    block_e, n_active, slot_t, slot_r, _ = _route(ti8, blk)
    y = _moe_experts(hf.reshape(-1, hf.shape[-1]), block_e, n_active, slot_t, slot_r, layer, *moe_w, blk, True)
    return _combine_layer(x1, mod, tg8, y, lng, lnb, bb, tl)


def _moe_layer_sorted(hf, ti8, tr8, counts, layer, moe_w, blk):
    block_e, n_active, slot_t, slot_r, pstart = _route(ti8, blk, counts)
    y = _moe_experts(hf.reshape(-1, hf.shape[-1]), block_e, n_active, slot_t, slot_r, layer, *moe_w, blk, False)
    nb, _, rows = ti8.shape
    e4 = ti8[:, :TOP_K, :]
    start = jnp.sum(jnp.where(e4[..., None] == jnp.arange(N_EXPERTS, dtype=jnp.int32), pstart, 0), axis=-1)
    return y, (start + tr8[:, :TOP_K, :]).reshape(nb, 1, TOP_K * rows)


def _rel_bias_table(rel_bias, c):
    w = BAND_PAST + c
    m = jnp.arange(w + c - 1)
    f = rel_bias[:, jnp.clip(BAND_PAST + c - 1 - m, -REL_CLIP, REL_CLIP) + REL_CLIP].astype(F32)
    return jnp.stack([f[:, c - 1 - t:c - 1 - t + w] for t in range(c)], axis=1)


def _trunk(x, mods, s0, past_k, past_v, wts, gla_bb, att_bb, blk):
    b, l, d = x.shape
    tl = min(TOKEN_TILE, l)
    c = min(CHUNK, l)
    row = lambda v: v.reshape(1, -1)

    def mod_of(layer):
        m = mods[layer].reshape(b, 6, d)
        return jnp.concatenate([m, jnp.zeros((b, 2, d), F32)], axis=1)

    def router_w(layer):
        return wts["moe_wr"][layer].T, wts["moe_br"][layer].reshape(N_EXPERTS, 1)

    moe_w = (wts["moe_wg"], wts["moe_bg"], wts["moe_wu"], wts["moe_bu"], wts["moe_wd"], wts["moe_bd"])

    wrt, br = router_w(0)
    gw = dict(wq=wts["gla_wq"][0].astype(BF16), wk=wts["gla_wk"][0].astype(BF16),
              wv=wts["gla_wv"][0].astype(BF16), wr=wts["gla_wr"][0].astype(BF16),
              wa1=wts["gla_wa1"][0].astype(BF16), wa2=wts["gla_wa2"][0].astype(BF16),
              ba=row(wts["gla_ba"][0]), ng=row(wts["gla_norm_g"][0]),
              wo=wts["gla_wo"][0].astype(BF16), lng=row(wts["ln_g"][0, 0]), lnb=row(wts["ln_b"][0, 0]),
              wrt=wrt, br=br)
    mod0 = mod_of(0)
    s0t = jnp.swapaxes(s0, -1, -2)
    assert gla_bb == att_bb or l // tl == 1
    x1, hf, ti8, tg8, tr8, cnt, sfin_t = _gla_layer(x, mod0, s0t, gw, gla_bb, tl)
    y_sorted, pos = _moe_layer_sorted(hf, ti8, tr8, cnt[:, 0].astype(jnp.int32), 0, moe_w, blk)
    s_fin = jnp.swapaxes(sfin_t, -1, -2)[None]

    wrt, br = router_w(1)
    aw = dict(wkv_k=wts["kv_wk"].astype(BF16), wkv_v=wts["kv_wv"].astype(BF16),
              wq=wts["att_wq"][0].astype(BF16), wo=wts["att_wo"][0].astype(BF16),
              bias=_rel_bias_table(wts["att_rel_bias"][0], c),
              lng=row(wts["ln_g"][1, 0]), lnb=row(wts["ln_b"][1, 0]), wrt=wrt, br=br)
    mod1 = mod_of(1)
    n_att = (b // att_bb) * (l // tl)
    tg8 = tg8.reshape(ti8.shape[0], 8, -1, att_bb * tl).transpose(0, 2, 1, 3).reshape(n_att, 8, att_bb * tl)
    pos = pos.reshape(ti8.shape[0], TOP_K, -1, att_bb * tl).transpose(0, 2, 1, 3).reshape(n_att, 1, -1)
    x1, hf, ti8, tg8, k_new, v_new = _attn_layer(x1, mod1, mod0, tg8, pos, y_sorted,
                                                 row(wts["ln_g"][0, 1]), row(wts["ln_b"][0, 1]),
                                                 past_k, past_v, aw, att_bb, tl)
    x = _moe_layer(x1, hf, ti8, tg8, mod1, 1, moe_w, row(wts["ln_g"][1, 1]), row(wts["ln_b"][1, 1]),
                   blk, att_bb, tl)
    keep = k_new.shape[1]
    hd = d // ATT_HEADS
    return (x, s_fin, k_new.reshape(b, keep, ATT_HEADS, hd), v_new.reshape(b, keep, ATT_HEADS, hd))


def kernel(x_prompt, x_sample, c_prompt, c_sample, state_gla, cache_k, cache_v, ada_w, ada_b, ln_g, ln_b, gla_wq, gla_wk, gla_wv, gla_wa1, gla_wa2, gla_ba, gla_wr, gla_norm_g, gla_wo, kv_wk, kv_wv, att_wq, att_wo, att_rel_bias, moe_wr, moe_br, moe_wg, moe_bg, moe_wu, moe_bu, moe_wd, moe_bd):
    wts = dict(ln_g=ln_g, ln_b=ln_b, gla_wq=gla_wq, gla_wk=gla_wk, gla_wv=gla_wv, gla_wa1=gla_wa1,
               gla_wa2=gla_wa2, gla_ba=gla_ba, gla_wr=gla_wr, gla_norm_g=gla_norm_g, gla_wo=gla_wo,
               kv_wk=kv_wk, kv_wv=kv_wv, att_wq=att_wq, att_wo=att_wo, att_rel_bias=att_rel_bias,
               moe_wr=moe_wr, moe_br=moe_br, moe_wg=moe_wg, moe_bg=moe_bg, moe_wu=moe_wu,
               moe_bu=moe_bu, moe_wd=moe_wd, moe_bd=moe_bd)
    b_p, l_p, d = x_prompt.shape
    b_s, l_s, _ = x_sample.shape
    assert cache_k.shape[1] == BAND_PAST
    mods = _ada_modulation(jnp.concatenate([c_prompt, c_sample], axis=0), ada_w, ada_b)
    mods_p, mods_s = mods[:, :b_p], mods[:, b_p:]

    s0_p = jnp.zeros((b_p,) + state_gla.shape[2:], F32)
    blk_p = MOE_BLOCK_LARGE if b_p * l_p * TOP_K >= 64 * MOE_BLOCK_LARGE else MOE_BLOCK_SMALL
    y_p, s_p, k_p, v_p = _trunk(x_prompt, mods_p, s0_p, None, None, wts, 1, 1, blk_p)

    pk = cache_k.reshape(b_s, BAND_PAST, d).astype(BF16)
    pv = cache_v.reshape(b_s, BAND_PAST, d).astype(BF16)
    blk_s = MOE_BLOCK_LARGE if b_s * l_s * TOP_K >= 64 * MOE_BLOCK_LARGE else MOE_BLOCK_SMALL
    y_s, s_s, k_s, v_s = _trunk(x_sample, mods_s, state_gla[0], pk, pv, wts, b_s, min(4, b_s), blk_s)
    return (y_p, y_s, s_p, k_p, v_p, s_s, k_s, v_s)
```

```python
import functools
import math

import jax
import jax.numpy as jnp
from jax import lax
from jax.experimental import pallas as pl
from jax.experimental.pallas import tpu as pltpu

F32 = jnp.float32
BF16 = jnp.bfloat16
F8 = jnp.float8_e4m3fn
F8_MAX = 448.0

CHUNK = 64
GLA_HEADS = 4
GLA_TAU = 16.0
ATT_HEADS = 16
BAND_PAST = 512
REL_CLIP = 128
N_EXPERTS = 32
TOP_K = 4
SWIGLU_LIMIT = 7.0
SWIGLU_ALPHA = 1.702
DEPTH = 2
DEEPNORM_ALPHA = (2.0 * DEPTH) ** 0.25
LN_EPS = 1e-5
NEG_INF = -1e30

V7X_VMEM_LIMIT_BYTES = 56 * 1024 * 1024
TOKEN_TILE = 512
GLA_SUB = 16
MOE_BLOCK_LARGE = 512
MOE_BLOCK_SMALL = 64
LANES = 128


def _dot(a, b):
    return jnp.dot(a, b, preferred_element_type=F32)


def _dot_nt(a, b):
    return lax.dot_general(a, b, (((1,), (1,)), ((), ())), preferred_element_type=F32)


def _dot_tn(a, b):
    return lax.dot_general(a, b, (((0,), (0,)), ((), ())), preferred_element_type=F32)


def _layer_norm(y, g, b):
    mu = jnp.mean(y, axis=-1, keepdims=True)
    yc = y - mu
    var = jnp.mean(yc * yc, axis=-1, keepdims=True)
    return yc * lax.rsqrt(var + LN_EPS) * g + b


def _store_token_major(ref, val):
    rows, d = val.shape
    g = d // LANES
    for s in range(g):
        ref[pl.ds(s, rows, stride=g), :] = val[:, s * LANES:(s + 1) * LANES]


def _load_token_major(ref, rows, d):
    g = d // LANES
    return jnp.concatenate([ref[pl.ds(s, rows, stride=g), :] for s in range(g)], axis=1)


def _const_spec(shape):
    nd = len(shape)
    return pl.BlockSpec(shape, lambda *_: (0,) * nd, pipeline_mode=pl.Buffered(1))


def _params(sem):
    return pltpu.CompilerParams(dimension_semantics=sem, vmem_limit_bytes=V7X_VMEM_LIMIT_BYTES)


def _ada_kernel(c_ref, w_ref, b_ref, o_ref):
    c = c_ref[...]
    s = c * jax.nn.sigmoid(c)
    o_ref[0] = jnp.dot(s, w_ref[0], preferred_element_type=F32,
                       precision=lax.Precision.HIGHEST) + b_ref[0]


def _ada_modulation(c, ada_w, ada_b):
    n, d = c.shape
    depth, _, d6 = ada_w.shape
    tn = d6 // 4
    return pl.pallas_call(
        _ada_kernel,
        grid=(depth, d6 // tn),
        in_specs=[pl.BlockSpec((n, d), lambda l, j: (0, 0)),
                  pl.BlockSpec((1, d, tn), lambda l, j: (l, 0, j)),
                  pl.BlockSpec((1, 1, tn), lambda l, j: (l, 0, j))],
        out_specs=pl.BlockSpec((1, n, tn), lambda l, j: (l, 0, j)),
        out_shape=jax.ShapeDtypeStruct((depth, n, d6), F32),
        compiler_params=_params(("arbitrary", "arbitrary")),
        name="ada_modulation",
    )(c, ada_w, ada_b.reshape(depth, 1, d6))


def _post_mixer(x3, mix3, g_m, sh_f, sc_f, lng, lnb, wrt, br):
    bb, tl, d = x3.shape
    rows = bb * tl
    x1 = _layer_norm(DEEPNORM_ALPHA * x3 + g_m * mix3, lng, lnb)
    hf = x1 * (1.0 + sc_f) + sh_f
    logits = lax.dot_general(wrt, hf.reshape(rows, d), (((1,), (1,)), ((), ())),
                             preferred_element_type=F32, precision=lax.Precision.HIGHEST) + br
    eidx = lax.broadcasted_iota(jnp.int32, logits.shape, 0).astype(F32)
    vals, idxs = [], []
    cur = logits
    for _ in range(TOP_K):
        m = jnp.max(cur, axis=0, keepdims=True)
        sel = jnp.min(jnp.where(cur == m, eidx, float(N_EXPERTS)), axis=0, keepdims=True)
        vals.append(m)
        idxs.append(sel)
        cur = jnp.where(eidx == sel, -jnp.inf, cur)
    ex = [jnp.exp(v - vals[0]) for v in vals]
    den = (ex[0] + ex[1]) + (ex[2] + ex[3])
    out_row = lax.broadcasted_iota(jnp.int32, (8, rows), 0)
    ti = jnp.zeros((8, rows), F32)
    tg = jnp.zeros((8, rows), F32)
    for k in range(TOP_K):
        ti = jnp.where(out_row == k, idxs[k], ti)
        tg = jnp.where(out_row == k, ex[k] / den, tg)
    return x1, hf, ti.astype(jnp.int32), tg


def _gla_kernel(x_ref, mod_ref, s0_ref, wq_ref, wk_ref, wv_ref, wr_ref, wa1_ref, wa2_ref, ba_ref,
                ng_ref, wo_ref, lng_ref, lnb_ref, wrt_ref, br_ref,
                x1_ref, hf_ref, ti_ref, tg_ref, sfin_ref,
                st_s, q_s, k_s, v_s, la_s, o_s, *, bb, tl, c):
    i = pl.program_id(1)
    d = x_ref.shape[-1]
    dk = q_s.shape[-1] // GLA_HEADS
    dv = v_s.shape[-1] // GLA_HEADS
    rows = bb * tl
    sub = min(GLA_SUB, c)

    @pl.when(i == 0)
    def _():
        st_s[...] = s0_ref[...]

    x3 = x_ref[...]
    mod = mod_ref[...]
    h = (x3 * (1.0 + mod[:, 1:2, :]) + mod[:, 0:1, :]).reshape(rows, d)
    hb = h.astype(BF16)
    q_s[...] = _dot(hb, wq_ref[...]) * (dk ** -0.5)
    k_s[...] = _dot(hb, wk_ref[...])
    v_s[...] = _dot(hb, wv_ref[...])
    a = _dot(_dot(hb, wa1_ref[...]).astype(BF16), wa2_ref[...]) + ba_ref[...]
    la_s[...] = (jnp.minimum(a, 0.0) - jnp.log1p(jnp.exp(-jnp.abs(a)))) * (1.0 / GLA_TAU)

    tril = (lax.broadcasted_iota(jnp.int32, (c, c), 0) >=
            lax.broadcasted_iota(jnp.int32, (c, c), 1)).astype(BF16)
    lane_c = lax.broadcasted_iota(jnp.int32, (sub, c), 1)
    row_c = lax.broadcasted_iota(jnp.int32, (sub, c), 0)
    krow = lax.broadcasted_iota(jnp.int32, (c, dk), 0)
    ng = ng_ref[...]
    chunks_per_seq = tl // c

    def chunk_body(n, carry):
        r0 = pl.multiple_of(n * c, c)
        bi = n // chunks_per_seq
        la = la_s[pl.ds(r0, c), :]
        la_hi = la.astype(BF16)
        la_lo = (la - la_hi.astype(F32)).astype(BF16)
        bcum = _dot(tril, la_hi) + _dot(tril, la_lo)
        for hd in range(GLA_HEADS):
            qh = q_s[pl.ds(r0, c), hd * dk:(hd + 1) * dk]
            kh = k_s[pl.ds(r0, c), hd * dk:(hd + 1) * dk]
            vh = v_s[pl.ds(r0, c), hd * dv:(hd + 1) * dv].astype(BF16)
            bh = bcum[:, hd * dk:(hd + 1) * dk]
            blocks = []
            for sb in range(c // sub):
                lo = sb * sub
                qi = qh[lo:lo + sub]
                bi_rows = bh[lo:lo + sub]
                if sb > 0:
                    ref_b = bh[lo:lo + 1]
                    qt = (qi * jnp.exp(bi_rows - ref_b)).astype(BF16)
                    kt = jnp.where(krow < lo, kh * jnp.exp(jnp.minimum(ref_b - bh, 0.0)), 0.0)
                    acc = _dot_nt(qt, kt.astype(BF16))
                else:
                    acc = jnp.zeros((sub, c), F32)
                for s in range(sub):
                    g = lo + s
                    e = jnp.exp(jnp.minimum(bi_rows - bh[g:g + 1], 0.0))
                    col = jnp.sum(qi * e * kh[g:g + 1], axis=-1, keepdims=True)
                    acc = jnp.where((lane_c == g) & (row_c >= s), col, acc)
                blocks.append(acc)
            amat = blocks[0] if len(blocks) == 1 else jnp.concatenate(blocks, axis=0)
            st = st_s[bi, hd]
            o = _dot(amat.astype(BF16), vh) + _dot_nt((qh * jnp.exp(bh)).astype(BF16), st.astype(BF16))
            ms = jnp.mean(o * o, axis=-1, keepdims=True)
            o_s[pl.ds(r0, c), hd * dv:(hd + 1) * dv] = o * lax.rsqrt(ms + LN_EPS) * ng
            bl = bh[c - 1:c]
            kd = (kh * jnp.exp(bl - bh)).astype(BF16)
            st_s[bi, hd] = st * jnp.exp(bl) + _dot_tn(vh, kd)
        return carry

    lax.fori_loop(0, rows // c, chunk_body, 0)

    r = _dot(hb, wr_ref[...])
    og = (o_s[...] * (r * jax.nn.sigmoid(r))).astype(BF16)
    mix = _dot(og, wo_ref[...]).reshape(bb, tl, d)
    x1, hf, ti, tg = _post_mixer(x3, mix, mod[:, 2:3, :], mod[:, 3:4, :], mod[:, 4:5, :],
                                 lng_ref[...], lnb_ref[...], wrt_ref[...], br_ref[...])
    x1_ref[...] = x1
    _store_token_major(hf_ref, hf.reshape(rows, d))
    ti_ref[0] = ti
    tg_ref[0] = tg

    @pl.when(i == pl.num_programs(1) - 1)
    def _():
        sfin_ref[...] = st_s[...]


def _gla_layer(x, mod, s0t, w, bb, tl):
    b, l, d = x.shape
    c = min(CHUNK, l)
    hk = w["wq"].shape[1]
    hv = w["wv"].shape[1]
    nb, nt = b // bb, l // tl
    assert bb == 1 or nt == 1
    rows = bb * tl
    dv, dk = s0t.shape[2], s0t.shape[3]
    kern = functools.partial(_gla_kernel, bb=bb, tl=tl, c=c)
    xmap = lambda ib, it: (ib, it, 0)
    bmap = lambda ib, it: (ib, 0, 0)
    smap = lambda ib, it: (ib, 0, 0, 0)
    rmap = lambda ib, it: (ib * nt + it, 0, 0)
    outs = pl.pallas_call(
        kern,
        grid=(nb, nt),
        in_specs=[pl.BlockSpec((bb, tl, d), xmap),
                  pl.BlockSpec((bb, 8, d), bmap),
                  pl.BlockSpec((bb, GLA_HEADS, dv, dk), smap),
                  _const_spec(w["wq"].shape), _const_spec(w["wk"].shape), _const_spec(w["wv"].shape),
                  _const_spec(w["wr"].shape), _const_spec(w["wa1"].shape), _const_spec(w["wa2"].shape),
                  _const_spec(w["ba"].shape), _const_spec(w["ng"].shape), _const_spec(w["wo"].shape),
                  _const_spec(w["lng"].shape), _const_spec(w["lnb"].shape),
                  _const_spec(w["wrt"].shape), _const_spec(w["br"].shape)],
        out_specs=[pl.BlockSpec((bb, tl, d), xmap),
                   pl.BlockSpec((rows * (d // LANES), LANES), lambda ib, it: (ib * nt + it, 0)),
                   pl.BlockSpec((1, 8, rows), rmap),
                   pl.BlockSpec((1, 8, rows), rmap),
                   pl.BlockSpec((bb, GLA_HEADS, dv, dk), smap)],
        out_shape=[jax.ShapeDtypeStruct((b, l, d), F32),
                   jax.ShapeDtypeStruct((b * l * (d // LANES), LANES), F32),
                   jax.ShapeDtypeStruct((nb * nt, 8, rows), jnp.int32),
                   jax.ShapeDtypeStruct((nb * nt, 8, rows), F32),
                   jax.ShapeDtypeStruct(s0t.shape, F32)],
        scratch_shapes=[pltpu.VMEM((bb, GLA_HEADS, dv, dk), F32),
                        pltpu.VMEM((rows, hk), F32), pltpu.VMEM((rows, hk), F32),
                        pltpu.VMEM((rows, hv), F32), pltpu.VMEM((rows, hk), F32),
                        pltpu.VMEM((rows, hv), F32)],
        compiler_params=_params(("arbitrary", "arbitrary")),
        name="gla_layer",
    )(x, mod, s0t, w["wq"], w["wk"], w["wv"], w["wr"], w["wa1"], w["wa2"], w["ba"], w["ng"],
      w["wo"], w["lng"], w["lnb"], w["wrt"], w["br"])
    return outs


def _attn_kernel(*refs, bb, tl, c, has_past):
    if has_past:
        (x_ref, mod_ref, pk_ref, pv_ref, wkv_k_ref, wkv_v_ref, wq_ref, wo_ref, bias_ref, lng_ref,
         lnb_ref, wrt_ref, br_ref, x1_ref, hf_ref, ti_ref, tg_ref, ko_ref, vo_ref,
         kw_s, vw_s, q_s, o_s, sc_s, m_s, e_s) = refs
    else:
        (x_ref, mod_ref, wkv_k_ref, wkv_v_ref, wq_ref, wo_ref, bias_ref, lng_ref,
         lnb_ref, wrt_ref, br_ref, x1_ref, hf_ref, ti_ref, tg_ref, ko_ref, vo_ref,
         kw_s, vw_s, q_s, o_s, sc_s, m_s, e_s) = refs
    i = pl.program_id(1)
    d = x_ref.shape[-1]
    p = BAND_PAST
    w = p + c
    rows = bb * tl
    hd2 = 2 * (d // ATT_HEADS)

    @pl.when(i == 0)
    def _():
        if has_past:
            kw_s[:, 0:p, :] = pk_ref[...]
            vw_s[:, 0:p, :] = pv_ref[...]
        else:
            kw_s[:, 0:p, :] = jnp.zeros((bb, p, d), BF16)
            vw_s[:, 0:p, :] = jnp.zeros((bb, p, d), BF16)

    x3 = x_ref[...]
    mod = mod_ref[...]
    xb = x3.reshape(rows, d).astype(BF16)
    hb = (x3 * (1.0 + mod[:, 1:2, :]) + mod[:, 0:1, :]).reshape(rows, d).astype(BF16)
    kn = _dot(xb, wkv_k_ref[...])
    vn = _dot(xb, wkv_v_ref[...])
    ko_ref[...] = kn.reshape(bb, tl, d)
    vo_ref[...] = vn.reshape(bb, tl, d)
    kw_s[:, p:p + tl, :] = kn.astype(BF16).reshape(bb, tl, d)
    vw_s[:, p:p + tl, :] = vn.astype(BF16).reshape(bb, tl, d)
    q_s[...] = (_dot(hb, wq_ref[...]) * ((d // ATT_HEADS) ** -0.5)).astype(BF16)

    lane = lax.broadcasted_iota(jnp.int32, (c, hd2), 1)
    low = lane < (hd2 // 2)
    kpos = lax.broadcasted_iota(jnp.int32, (1, w), 1)
    chunks_per_seq = tl // c

    def chunk_body(n, carry):
        r0 = pl.multiple_of(n * c, c)
        bi = n // chunks_per_seq
        ci = n - bi * chunks_per_seq
        w0 = pl.multiple_of(ci * c, c)
        if not has_past:
            valid = (kpos + (i * tl + ci * c)) >= p
        for hp in range(ATT_HEADS // 2):
            qp = q_s[pl.ds(r0, c), hp * hd2:(hp + 1) * hd2]
            kwin = kw_s[bi, pl.ds(w0, w), hp * hd2:(hp + 1) * hd2]
            for half in range(2):
                qm = jnp.where(low if half == 0 else ~low, qp, jnp.zeros_like(qp))
                s = _dot_nt(qm, kwin) + bias_ref[2 * hp + half]
                if not has_past:
                    s = jnp.where(valid, s, NEG_INF)
                sc_s[2 * hp + half] = s
        for hh in range(ATT_HEADS):
            m_s[hh] = jnp.broadcast_to(jnp.max(sc_s[hh], axis=-1, keepdims=True), (c, LANES))
        for hh in range(ATT_HEADS):
            s = sc_s[hh]
            m = m_s[hh]
            parts = [jnp.exp(s[:, j * LANES:(j + 1) * LANES] - m) for j in range(w // LANES)]
            if w % LANES:
                parts.append(jnp.exp(s[:, w - w % LANES:] - m[:, :w % LANES]))
            e_s[hh] = jnp.concatenate(parts, axis=1).astype(BF16)
        ones = jnp.ones((w, hd2), BF16)
        for hp in range(ATT_HEADS // 2):
            vext = jnp.concatenate([vw_s[bi, pl.ds(w0, w), hp * hd2:(hp + 1) * hd2], ones], axis=1)
            outs = []
            for half in range(2):
                r = _dot(e_s[2 * hp + half], vext)
                outs.append(r[:, :hd2] / r[:, hd2:])
            o_s[pl.ds(r0, c), hp * hd2:(hp + 1) * hd2] = jnp.where(low, outs[0], outs[1])
        return carry

    lax.fori_loop(0, rows // c, chunk_body, 0)

    if tl >= p:
        kw_s[:, 0:p, :] = kw_s[:, tl:tl + p, :]
        vw_s[:, 0:p, :] = vw_s[:, tl:tl + p, :]

    mix = _dot(o_s[...].astype(BF16), wo_ref[...]).reshape(bb, tl, d)
    x1, hf, ti, tg = _post_mixer(x3, mix, mod[:, 2:3, :], mod[:, 3:4, :], mod[:, 4:5, :],
                                 lng_ref[...], lnb_ref[...], wrt_ref[...], br_ref[...])
    x1_ref[...] = x1
    _store_token_major(hf_ref, hf.reshape(rows, d))
    ti_ref[0] = ti
    tg_ref[0] = tg


def _attn_layer(x, mod, past_k, past_v, w, bb, tl):
    b, l, d = x.shape
    c = min(CHUNK, l)
    has_past = past_k is not None
    nb, nt = b // bb, l // tl
    assert nt == 1 or (tl >= BAND_PAST and bb == 1)
    rows = bb * tl
    keep = min(BAND_PAST, l)
    assert keep == tl or (keep == BAND_PAST and tl == BAND_PAST)
    kern = functools.partial(_attn_kernel, bb=bb, tl=tl, c=c, has_past=has_past)
    xmap = lambda ib, it: (ib, it, 0)
    bmap = lambda ib, it: (ib, 0, 0)
    rmap = lambda ib, it: (ib * nt + it, 0, 0)
    in_specs = [pl.BlockSpec((bb, tl, d), xmap), pl.BlockSpec((bb, 8, d), bmap)]
    args = [x, mod]
    if has_past:
        in_specs += [pl.BlockSpec((bb, BAND_PAST, d), bmap)] * 2
        args += [past_k, past_v]
    names = ["wkv_k", "wkv_v", "wq", "wo", "bias", "lng", "lnb", "wrt", "br"]
    in_specs += [_const_spec(w[n].shape) for n in names]
    args += [w[n] for n in names]
    outs = pl.pallas_call(
        kern,
        grid=(nb, nt),
        in_specs=in_specs,
        out_specs=[pl.BlockSpec((bb, tl, d), xmap),
                   pl.BlockSpec((rows * (d // LANES), LANES), lambda ib, it: (ib * nt + it, 0)),
                   pl.BlockSpec((1, 8, rows), rmap),
                   pl.BlockSpec((1, 8, rows), rmap),
                   pl.BlockSpec((bb, tl, d), bmap),
                   pl.BlockSpec((bb, tl, d), bmap)],
        out_shape=[jax.ShapeDtypeStruct((b, l, d), F32),
                   jax.ShapeDtypeStruct((b * l * (d // LANES), LANES), F32),
                   jax.ShapeDtypeStruct((nb * nt, 8, rows), jnp.int32),
                   jax.ShapeDtypeStruct((nb * nt, 8, rows), F32),
                   jax.ShapeDtypeStruct((b, keep, d), F32),
                   jax.ShapeDtypeStruct((b, keep, d), F32)],
        scratch_shapes=[pltpu.VMEM((bb, BAND_PAST + tl, d), BF16),
                        pltpu.VMEM((bb, BAND_PAST + tl, d), BF16),
                        pltpu.VMEM((rows, d), BF16),
                        pltpu.VMEM((rows, d), F32),
                        pltpu.VMEM((ATT_HEADS, c, BAND_PAST + c), F32),
                        pltpu.VMEM((ATT_HEADS, c, LANES), F32),
                        pltpu.VMEM((ATT_HEADS, c, BAND_PAST + c), BF16)],
        compiler_params=_params(("arbitrary", "arbitrary")),
        name="attn_layer",
    )(*args)
    return outs


def _moe_kernel(be_ref, nact_ref, st_ref, stn_ref, sr_ref, hf_hbm, wg_ref, bg_ref, wu_ref, bu_ref,
                wd_ref, bd_ref, y_hbm, xbuf, ybuf, xb_s, hm_s, wgb, wub, wdb, xs_s, ws_s, gsem, ssem,
                *, blk, g):
    i = pl.program_id(0)
    nact = nact_ref[0]
    f = hm_s.shape[1]
    n_col = 4
    fc = f // n_col
    per = blk // n_col
    rc = min(blk, 256)

    def gather_row(tok_ref, j):
        src = pl.multiple_of(tok_ref[0, 0, j], g)
        return pltpu.make_async_copy(hf_hbm.at[pl.ds(src, g)], xbuf.at[pl.ds(j * g, g)], gsem)

    def gather_all():
        return pltpu.make_async_copy(hf_hbm.at[pl.ds(0, blk * g)], xbuf, gsem)

    def scatter_all():
        return pltpu.make_async_copy(ybuf, y_hbm.at[pl.ds(y_hbm.shape[0] - blk * g, blk * g)], ssem)

    @pl.when(i == 0)
    def _():
        ybuf[...] = jnp.zeros(ybuf.shape, F32)
        scatter_all().start()

        def prime(j, carry):
            src = pl.multiple_of(st_ref[0, 0, j], g)
            pltpu.make_async_copy(hf_hbm.at[pl.ds(src, g)],
                                  xbuf.at[pl.ds(pl.multiple_of(j * g, g), g)], gsem).start()
            return carry

        lax.fori_loop(0, blk, prime, 0)

    @pl.when(i < nact)
    def _():
        e = be_ref[i]
        prev = be_ref[jnp.maximum(i - 1, 0)]

        @pl.when((i == 0) | (e != prev))
        def _():
            for k, (w_ref, w8) in enumerate(((wg_ref, wgb), (wu_ref, wub), (wd_ref, wdb))):
                w = w_ref[0, 0]
                top = jnp.max(jnp.max(jnp.abs(w), axis=0, keepdims=True), axis=1, keepdims=True)
                top = jnp.maximum(top, 1e-30)
                w8[...] = (w * (F8_MAX / top)).astype(F8)
                ws_s[k:k + 1, :] = jnp.broadcast_to(top * (1.0 / F8_MAX), (1, LANES))

        gather_all().wait()
        xparts = [xbuf[pl.ds(s, blk, stride=g), :] for s in range(g)]
        row_top = jnp.max(jnp.abs(xparts[0]), axis=-1, keepdims=True)
        for s in range(1, g):
            row_top = jnp.maximum(row_top, jnp.max(jnp.abs(xparts[s]), axis=-1, keepdims=True))
        row_top = jnp.maximum(row_top, 1e-30)
        x_scale = F8_MAX / row_top
        for s in range(g):
            xb_s[:, s * LANES:(s + 1) * LANES] = (xparts[s] * x_scale).astype(F8)
        xs_s[...] = jnp.broadcast_to(row_top * (1.0 / F8_MAX), xs_s.shape)
        h_scale = F8_MAX / ((SWIGLU_LIMIT + 1.0) * SWIGLU_LIMIT)
        for c in range(n_col):
            cols = slice(c * fc, (c + 1) * fc)
            xinv = xs_s[:, 0:1]
            a = jnp.minimum(_dot(xb_s[...], wgb[:, cols]) * (xinv * ws_s[0:1, 0:1]) + bg_ref[0, 0][:, cols],
                            SWIGLU_LIMIT)
            u = jnp.clip(_dot(xb_s[...], wub[:, cols]) * (xinv * ws_s[1:2, 0:1]) + bu_ref[0, 0][:, cols],
                         -SWIGLU_LIMIT, SWIGLU_LIMIT)
            hm_s[:, cols] = ((u + 1.0) * (a * jax.nn.sigmoid(SWIGLU_ALPHA * a)) * h_scale).astype(F8)
            for j in range(c * per, (c + 1) * per):
                gather_row(stn_ref, j).start()
        scatter_all().wait()
        for r0 in range(0, blk, rc):
            y = _dot(hm_s[r0:r0 + rc, :], wdb[...]) * (ws_s[2:3, 0:1] * (1.0 / h_scale)) + bd_ref[0, 0]
            for s in range(g):
                ybuf[pl.ds(r0 * g + s, rc, stride=g), :] = y[:, s * LANES:(s + 1) * LANES]
            for j in range(r0, r0 + rc):
                dst = pl.multiple_of(sr_ref[0, 0, j], g)
                pltpu.make_async_copy(ybuf.at[pl.ds(j * g, g)], y_hbm.at[pl.ds(dst, g)], ssem).start()

        @pl.when(i == nact - 1)
        def _():
            scatter_all().wait()
            gather_all().wait()


def _moe_experts(hf_tm, block_e, n_active, slot_t, slot_r, layer, wg, bg, wu, bu, wd, bd, blk):
    d = wg.shape[2]
    g = d // LANES
    t = hf_tm.shape[0] // g
    f = wg.shape[-1]
    n_blocks = block_e.shape[0]
    kern = functools.partial(_moe_kernel, blk=blk, g=g)
    wmap = lambda i, be, na: (layer, be[i], 0, 0)
    smap = lambda i, be, na: (i, 0, 0)
    nmap = lambda i, be, na: (jnp.minimum(i + 1, n_blocks - 1), 0, 0)
    grid_spec = pltpu.PrefetchScalarGridSpec(
        num_scalar_prefetch=2,
        grid=(n_blocks,),
        in_specs=[pl.BlockSpec((1, 1, blk), smap, memory_space=pltpu.SMEM),
                  pl.BlockSpec((1, 1, blk), nmap, memory_space=pltpu.SMEM),
                  pl.BlockSpec((1, 1, blk), smap, memory_space=pltpu.SMEM),
                  pl.BlockSpec(memory_space=pl.ANY),
                  pl.BlockSpec((1, 1, d, f), wmap), pl.BlockSpec((1, 1, 1, f), wmap),
                  pl.BlockSpec((1, 1, d, f), wmap), pl.BlockSpec((1, 1, 1, f), wmap),
                  pl.BlockSpec((1, 1, f, d), wmap), pl.BlockSpec((1, 1, 1, d), wmap)],
        out_specs=pl.BlockSpec(memory_space=pl.ANY),
        scratch_shapes=[pltpu.VMEM((blk * g, LANES), F32), pltpu.VMEM((blk * g, LANES), F32),
                        pltpu.VMEM((blk, d), F8), pltpu.VMEM((blk, f), F8),
                        pltpu.VMEM((d, f), F8), pltpu.VMEM((d, f), F8), pltpu.VMEM((f, d), F8),
                        pltpu.VMEM((blk, LANES), F32), pltpu.VMEM((8, LANES), F32),
                        pltpu.SemaphoreType.DMA, pltpu.SemaphoreType.DMA])
    st3 = (slot_t * g).reshape(n_blocks, 1, blk)
    depth = wg.shape[0]
    return pl.pallas_call(
        kern,
        grid_spec=grid_spec,
        out_shape=jax.ShapeDtypeStruct(((TOP_K * t + blk) * g, LANES), F32),
        compiler_params=_params(("arbitrary",)),
        name="moe_experts",
    )(block_e, n_active, st3, st3, (slot_r * g).reshape(n_blocks, 1, blk), hf_tm,
      wg, bg.reshape(depth, N_EXPERTS, 1, f), wu, bu.reshape(depth, N_EXPERTS, 1, f),
      wd, bd.reshape(depth, N_EXPERTS, 1, d))


def _route(ti8, blk):
    nb, _, rows = ti8.shape
    t = nb * rows
    npair = t * TOP_K
    assert npair % blk == 0
    flat_e = ti8[:, :TOP_K, :].reshape(-1)
    experts = jnp.arange(N_EXPERTS, dtype=jnp.int32)
    counts = jnp.sum((flat_e[:, None] == experts[None, :]).astype(jnp.int32), axis=0)
    padded = (counts + blk - 1) // blk * blk
    pend = jnp.cumsum(padded)
    need = padded - counts
    pair_bits = 19
    assert npair <= 1 << pair_bits and blk <= 1 << pair_bits
    within = jnp.arange(blk, dtype=jnp.int32)[None, :]
    dummy_key = jnp.where(within < need[:, None],
                          (experts[:, None] << (pair_bits + 1)) | (1 << pair_bits) | within,
                          ((2 * N_EXPERTS) << (pair_bits + 1)) | (1 << pair_bits))
    real_key = (flat_e << (pair_bits + 1)) | jnp.arange(npair, dtype=jnp.int32)
    skey = lax.sort(jnp.concatenate([real_key, dummy_key.reshape(-1)]))
    n_blocks = npair // blk + N_EXPERTS
    real = ((skey >> pair_bits) & 1) == 0
    q = jnp.where(real, skey & ((1 << pair_bits) - 1), 0)
    tok = (q // (TOP_K * rows)) * rows + q % rows
    choice = (q // rows) % TOP_K
    slot_t = jnp.where(real, tok, 0)
    spare = npair + jnp.arange(n_blocks * blk, dtype=jnp.int32) % blk
    slot_r = jnp.where(real, choice * t + tok, spare)
    bstart = jnp.arange(n_blocks, dtype=jnp.int32) * blk
    block_e = jnp.minimum(jnp.sum((pend[None, :] <= bstart[:, None]).astype(jnp.int32), axis=1),
                          N_EXPERTS - 1)
    n_active = (pend[-1] // blk).astype(jnp.int32).reshape(1)
    return block_e, n_active, slot_t, slot_r


def _combine_kernel(x_ref, mod_ref, tg_ref, y0, y1, y2, y3, lng_ref, lnb_ref, o_ref):
    bb, tl, d = x_ref.shape
    rows = bb * tl
    sel = (lax.broadcasted_iota(jnp.int32, (8, LANES), 0) ==
           lax.broadcasted_iota(jnp.int32, (8, LANES), 1)).astype(F32)
    gt = lax.dot_general(tg_ref[0], sel, (((0,), (0,)), ((), ())), preferred_element_type=F32,
                         precision=lax.Precision.HIGHEST)
    ys = (y0, y1, y2, y3)
    terms = [gt[:, k:k + 1] * _load_token_major(ys[k], rows, d) for k in range(TOP_K)]
    ff = ((terms[0] + terms[1]) + (terms[2] + terms[3])).reshape(bb, tl, d)
    g_f = mod_ref[:, 5:6, :]
    o_ref[...] = _layer_norm(DEEPNORM_ALPHA * x_ref[...] + g_f * ff, lng_ref[...], lnb_ref[...])


def _combine_layer(x1, mod, tg8, y, lng, lnb, bb, tl):
    b, l, d = x1.shape
    nb, nt = b // bb, l // tl
    rows = bb * tl
    tb = (b * l) // rows
    ymaps = [functools.partial(lambda ib, it, k: (k * tb + ib * nt + it, 0), k=k) for k in range(TOP_K)]
    return pl.pallas_call(
        _combine_kernel,
        grid=(nb, nt),
        in_specs=[pl.BlockSpec((bb, tl, d), lambda ib, it: (ib, it, 0)),
                  pl.BlockSpec((bb, 8, d), lambda ib, it: (ib, 0, 0)),
                  pl.BlockSpec((1, 8, rows), lambda ib, it: (ib * nt + it, 0, 0))]
                 + [pl.BlockSpec((rows * (d // LANES), LANES), m) for m in ymaps]
                 + [_const_spec(lng.shape), _const_spec(lnb.shape)],
        out_specs=pl.BlockSpec((bb, tl, d), lambda ib, it: (ib, it, 0)),
        out_shape=jax.ShapeDtypeStruct((b, l, d), F32),
        compiler_params=_params(("arbitrary", "arbitrary")),
        name="moe_combine",
    )(x1, mod, tg8, y, y, y, y, lng, lnb)


def _moe_layer(x1, hf, ti8, tg8, mod, layer, moe_w, lng, lnb, blk, bb, tl):
    block_e, n_active, slot_t, slot_r = _route(ti8, blk)
    y = _moe_experts(hf, block_e, n_active, slot_t, slot_r, layer, *moe_w, blk)
    return _combine_layer(x1, mod, tg8, y, lng, lnb, bb, tl)


def _rel_bias_table(rel_bias, c):
    w = BAND_PAST + c
    m = jnp.arange(w + c - 1)
    f = rel_bias[:, jnp.clip(BAND_PAST + c - 1 - m, -REL_CLIP, REL_CLIP) + REL_CLIP].astype(F32)
    return jnp.stack([f[:, c - 1 - t:c - 1 - t + w] for t in range(c)], axis=1)


def _trunk(x, mods, s0, past_k, past_v, wts, gla_bb, att_bb, blk):
    b, l, d = x.shape
    tl = min(TOKEN_TILE, l)
    c = min(CHUNK, l)
    row = lambda v: v.reshape(1, -1)

    def mod_of(layer):
        m = mods[layer].reshape(b, 6, d)
        return jnp.concatenate([m, jnp.zeros((b, 2, d), F32)], axis=1)

    def router_w(layer):
        return wts["moe_wr"][layer].T, wts["moe_br"][layer].reshape(N_EXPERTS, 1)

    moe_w = (wts["moe_wg"], wts["moe_bg"], wts["moe_wu"], wts["moe_bu"], wts["moe_wd"], wts["moe_bd"])

    wrt, br = router_w(0)
    gw = dict(wq=wts["gla_wq"][0].astype(BF16), wk=wts["gla_wk"][0].astype(BF16),
              wv=wts["gla_wv"][0].astype(BF16), wr=wts["gla_wr"][0].astype(BF16),
              wa1=wts["gla_wa1"][0].astype(BF16), wa2=wts["gla_wa2"][0].astype(BF16),
              ba=row(wts["gla_ba"][0]), ng=row(wts["gla_norm_g"][0]),
              wo=wts["gla_wo"][0].astype(BF16), lng=row(wts["ln_g"][0, 0]), lnb=row(wts["ln_b"][0, 0]),
              wrt=wrt, br=br)
    mod0 = mod_of(0)
    s0t = jnp.swapaxes(s0, -1, -2)
    x1, hf, ti8, tg8, sfin_t = _gla_layer(x, mod0, s0t, gw, gla_bb, tl)
    x = _moe_layer(x1, hf, ti8, tg8, mod0, 0, moe_w, row(wts["ln_g"][0, 1]), row(wts["ln_b"][0, 1]),
                   blk, gla_bb, tl)
    s_fin = jnp.swapaxes(sfin_t, -1, -2)[None]

    wrt, br = router_w(1)
    aw = dict(wkv_k=wts["kv_wk"].astype(BF16), wkv_v=wts["kv_wv"].astype(BF16),
              wq=wts["att_wq"][0].astype(BF16), wo=wts["att_wo"][0].astype(BF16),
              bias=_rel_bias_table(wts["att_rel_bias"][0], c),
              lng=row(wts["ln_g"][1, 0]), lnb=row(wts["ln_b"][1, 0]), wrt=wrt, br=br)
    mod1 = mod_of(1)
    x1, hf, ti8, tg8, k_new, v_new = _attn_layer(x, mod1, past_k, past_v, aw, att_bb, tl)
    x = _moe_layer(x1, hf, ti8, tg8, mod1, 1, moe_w, row(wts["ln_g"][1, 1]), row(wts["ln_b"][1, 1]),
                   blk, att_bb, tl)
    keep = k_new.shape[1]
    hd = d // ATT_HEADS
    return (x, s_fin, k_new.reshape(b, keep, ATT_HEADS, hd), v_new.reshape(b, keep, ATT_HEADS, hd))


def kernel(x_prompt, x_sample, c_prompt, c_sample, state_gla, cache_k, cache_v, ada_w, ada_b, ln_g, ln_b, gla_wq, gla_wk, gla_wv, gla_wa1, gla_wa2, gla_ba, gla_wr, gla_norm_g, gla_wo, kv_wk, kv_wv, att_wq, att_wo, att_rel_bias, moe_wr, moe_br, moe_wg, moe_bg, moe_wu, moe_bu, moe_wd, moe_bd):
    wts = dict(ln_g=ln_g, ln_b=ln_b, gla_wq=gla_wq, gla_wk=gla_wk, gla_wv=gla_wv, gla_wa1=gla_wa1,
               gla_wa2=gla_wa2, gla_ba=gla_ba, gla_wr=gla_wr, gla_norm_g=gla_norm_g, gla_wo=gla_wo,
               kv_wk=kv_wk, kv_wv=kv_wv, att_wq=att_wq, att_wo=att_wo, att_rel_bias=att_rel_bias,
               moe_wr=moe_wr, moe_br=moe_br, moe_wg=moe_wg, moe_bg=moe_bg, moe_wu=moe_wu,
               moe_bu=moe_bu, moe_wd=moe_wd, moe_bd=moe_bd)
    b_p, l_p, d = x_prompt.shape
    b_s, l_s, _ = x_sample.shape
    assert cache_k.shape[1] == BAND_PAST
    mods = _ada_modulation(jnp.concatenate([c_prompt, c_sample], axis=0), ada_w, ada_b)
    mods_p, mods_s = mods[:, :b_p], mods[:, b_p:]

    s0_p = jnp.zeros((b_p,) + state_gla.shape[2:], F32)
    blk_p = MOE_BLOCK_LARGE if b_p * l_p * TOP_K >= 64 * MOE_BLOCK_LARGE else MOE_BLOCK_SMALL
    y_p, s_p, k_p, v_p = _trunk(x_prompt, mods_p, s0_p, None, None, wts, 1, 1, blk_p)

    pk = cache_k.reshape(b_s, BAND_PAST, d).astype(BF16)
    pv = cache_v.reshape(b_s, BAND_PAST, d).astype(BF16)
    blk_s = MOE_BLOCK_LARGE if b_s * l_s * TOP_K >= 64 * MOE_BLOCK_LARGE else MOE_BLOCK_SMALL
    y_s, s_s, k_s, v_s = _trunk(x_sample, mods_s, state_gla[0], pk, pv, wts, b_s, min(4, b_s), blk_s)
    return (y_p, y_s, s_p, k_p, v_p, s_s, k_s, v_s)
```

```python
import functools
import math

import jax
import jax.numpy as jnp
from jax import lax
from jax.experimental import pallas as pl
from jax.experimental.pallas import tpu as pltpu

F32 = jnp.float32
BF16 = jnp.bfloat16
F8 = jnp.float8_e4m3fn
F8_MAX = 448.0

CHUNK = 64
GLA_HEADS = 4
GLA_TAU = 16.0
ATT_HEADS = 16
BAND_PAST = 512
REL_CLIP = 128
N_EXPERTS = 32
TOP_K = 4
SWIGLU_LIMIT = 7.0
SWIGLU_ALPHA = 1.702
DEPTH = 2
DEEPNORM_ALPHA = (2.0 * DEPTH) ** 0.25
LN_EPS = 1e-5
NEG_INF = -1e30

V7X_VMEM_LIMIT_BYTES = 56 * 1024 * 1024
TOKEN_TILE = 512
GLA_SUB = 16
MOE_BLOCK_LARGE = 512
MOE_BLOCK_SMALL = 64
LANES = 128


def _dot(a, b):
    return jnp.dot(a, b, preferred_element_type=F32)


def _dot_nt(a, b):
    return lax.dot_general(a, b, (((1,), (1,)), ((), ())), preferred_element_type=F32)


def _dot_tn(a, b):
    return lax.dot_general(a, b, (((0,), (0,)), ((), ())), preferred_element_type=F32)


def _layer_norm(y, g, b):
    mu = jnp.mean(y, axis=-1, keepdims=True)
    yc = y - mu
    var = jnp.mean(yc * yc, axis=-1, keepdims=True)
    return yc * lax.rsqrt(var + LN_EPS) * g + b


def _const_spec(shape):
    nd = len(shape)
    return pl.BlockSpec(shape, lambda *_: (0,) * nd, pipeline_mode=pl.Buffered(1))


def _params(sem):
    return pltpu.CompilerParams(dimension_semantics=sem, vmem_limit_bytes=V7X_VMEM_LIMIT_BYTES)


def _ada_kernel(c_ref, w_ref, b_ref, o_ref):
    c = c_ref[...]
    s = c * jax.nn.sigmoid(c)
    o_ref[0] = jnp.dot(s, w_ref[0], preferred_element_type=F32,
                       precision=lax.Precision.HIGHEST) + b_ref[0]


def _ada_modulation(c, ada_w, ada_b):
    n, d = c.shape
    depth, _, d6 = ada_w.shape
    tn = d6 // 4
    return pl.pallas_call(
        _ada_kernel,
        grid=(depth, d6 // tn),
        in_specs=[pl.BlockSpec((n, d), lambda l, j: (0, 0)),
                  pl.BlockSpec((1, d, tn), lambda l, j: (l, 0, j)),
                  pl.BlockSpec((1, 1, tn), lambda l, j: (l, 0, j))],
        out_specs=pl.BlockSpec((1, n, tn), lambda l, j: (l, 0, j)),
        out_shape=jax.ShapeDtypeStruct((depth, n, d6), F32),
        compiler_params=_params(("arbitrary", "arbitrary")),
        name="ada_modulation",
    )(c, ada_w, ada_b.reshape(depth, 1, d6))


def _post_mixer(x3, mix3, g_m, sh_f, sc_f, lng, lnb, wrt, br):
    bb, tl, d = x3.shape
    rows = bb * tl
    x1 = _layer_norm(DEEPNORM_ALPHA * x3 + g_m * mix3, lng, lnb)
    hf = x1 * (1.0 + sc_f) + sh_f
    logits = lax.dot_general(wrt, hf.reshape(rows, d), (((1,), (1,)), ((), ())),
                             preferred_element_type=F32, precision=lax.Precision.HIGHEST) + br
    eidx = lax.broadcasted_iota(jnp.int32, logits.shape, 0).astype(F32)
    vals, idxs = [], []
    cur = logits
    for _ in range(TOP_K):
        m = jnp.max(cur, axis=0, keepdims=True)
        sel = jnp.min(jnp.where(cur == m, eidx, float(N_EXPERTS)), axis=0, keepdims=True)
        vals.append(m)
        idxs.append(sel)
        cur = jnp.where(eidx == sel, -jnp.inf, cur)
    ex = [jnp.exp(v - vals[0]) for v in vals]
    den = (ex[0] + ex[1]) + (ex[2] + ex[3])
    out_row = lax.broadcasted_iota(jnp.int32, (8, rows), 0)
    ti = jnp.zeros((8, rows), F32)
    tg = jnp.zeros((8, rows), F32)
    for k in range(TOP_K):
        ti = jnp.where(out_row == k, idxs[k], ti)
        tg = jnp.where(out_row == k, ex[k] / den, tg)
    return x1, hf, ti.astype(jnp.int32), tg


def _gla_kernel(x_ref, mod_ref, s0_ref, wq_ref, wk_ref, wv_ref, wr_ref, wa1_ref, wa2_ref, ba_ref,
                ng_ref, wo_ref, lng_ref, lnb_ref, wrt_ref, br_ref,
                x1_ref, hf_ref, ti_ref, tg_ref, sfin_ref,
                st_s, q_s, k_s, v_s, la_s, o_s, *, bb, tl, c):
    i = pl.program_id(1)
    d = x_ref.shape[-1]
    dk = q_s.shape[-1] // GLA_HEADS
    dv = v_s.shape[-1] // GLA_HEADS
    rows = bb * tl
    sub = min(GLA_SUB, c)

    @pl.when(i == 0)
    def _():
        st_s[...] = s0_ref[...]

    x3 = x_ref[...]
    mod = mod_ref[...]
    h = (x3 * (1.0 + mod[:, 1:2, :]) + mod[:, 0:1, :]).reshape(rows, d)
    hb = h.astype(BF16)
    q_s[...] = _dot(hb, wq_ref[...]) * (dk ** -0.5)
    k_s[...] = _dot(hb, wk_ref[...])
    v_s[...] = _dot(hb, wv_ref[...])
    a = _dot(_dot(hb, wa1_ref[...]).astype(BF16), wa2_ref[...]) + ba_ref[...]
    la_s[...] = (jnp.minimum(a, 0.0) - jnp.log1p(jnp.exp(-jnp.abs(a)))) * (1.0 / GLA_TAU)

    tril = (lax.broadcasted_iota(jnp.int32, (c, c), 0) >=
            lax.broadcasted_iota(jnp.int32, (c, c), 1)).astype(BF16)
    lane_c = lax.broadcasted_iota(jnp.int32, (sub, c), 1)
    row_c = lax.broadcasted_iota(jnp.int32, (sub, c), 0)
    krow = lax.broadcasted_iota(jnp.int32, (c, dk), 0)
    ng = ng_ref[...]
    chunks_per_seq = tl // c

    def chunk_body(n, carry):
        r0 = pl.multiple_of(n * c, c)
        bi = n // chunks_per_seq
        la = la_s[pl.ds(r0, c), :]
        la_hi = la.astype(BF16)
        la_lo = (la - la_hi.astype(F32)).astype(BF16)
        bcum = _dot(tril, la_hi) + _dot(tril, la_lo)
        for hd in range(GLA_HEADS):
            qh = q_s[pl.ds(r0, c), hd * dk:(hd + 1) * dk]
            kh = k_s[pl.ds(r0, c), hd * dk:(hd + 1) * dk]
            vh = v_s[pl.ds(r0, c), hd * dv:(hd + 1) * dv].astype(BF16)
            bh = bcum[:, hd * dk:(hd + 1) * dk]
            blocks = []
            for sb in range(c // sub):
                lo = sb * sub
                qi = qh[lo:lo + sub]
                bi_rows = bh[lo:lo + sub]
                if sb > 0:
                    ref_b = bh[lo:lo + 1]
                    qt = (qi * jnp.exp(bi_rows - ref_b)).astype(BF16)
                    kt = jnp.where(krow < lo, kh * jnp.exp(jnp.minimum(ref_b - bh, 0.0)), 0.0)
                    acc = _dot_nt(qt, kt.astype(BF16))
                else:
                    acc = jnp.zeros((sub, c), F32)
                for s in range(sub):
                    g = lo + s
                    e = jnp.exp(jnp.minimum(bi_rows - bh[g:g + 1], 0.0))
                    col = jnp.sum(qi * e * kh[g:g + 1], axis=-1, keepdims=True)
                    acc = jnp.where((lane_c == g) & (row_c >= s), col, acc)
                blocks.append(acc)
            amat = blocks[0] if len(blocks) == 1 else jnp.concatenate(blocks, axis=0)
            st = st_s[bi, hd]
            o = _dot(amat.astype(BF16), vh) + _dot_nt((qh * jnp.exp(bh)).astype(BF16), st.astype(BF16))
            ms = jnp.mean(o * o, axis=-1, keepdims=True)
            o_s[pl.ds(r0, c), hd * dv:(hd + 1) * dv] = o * lax.rsqrt(ms + LN_EPS) * ng
            bl = bh[c - 1:c]
            kd = (kh * jnp.exp(bl - bh)).astype(BF16)
            st_s[bi, hd] = st * jnp.exp(bl) + _dot_tn(vh, kd)
        return carry

    lax.fori_loop(0, rows // c, chunk_body, 0)

    r = _dot(hb, wr_ref[...])
    og = (o_s[...] * (r * jax.nn.sigmoid(r))).astype(BF16)
    mix = _dot(og, wo_ref[...]).reshape(bb, tl, d)
    x1, hf, ti, tg = _post_mixer(x3, mix, mod[:, 2:3, :], mod[:, 3:4, :], mod[:, 4:5, :],
                                 lng_ref[...], lnb_ref[...], wrt_ref[...], br_ref[...])
    x1_ref[...] = x1
    hf_ref[...] = hf
    ti_ref[0] = ti
    tg_ref[0] = tg

    @pl.when(i == pl.num_programs(1) - 1)
    def _():
        sfin_ref[...] = st_s[...]


def _gla_layer(x, mod, s0t, w, bb, tl):
    b, l, d = x.shape
    c = min(CHUNK, l)
    hk = w["wq"].shape[1]
    hv = w["wv"].shape[1]
    nb, nt = b // bb, l // tl
    assert bb == 1 or nt == 1
    rows = bb * tl
    dv, dk = s0t.shape[2], s0t.shape[3]
    kern = functools.partial(_gla_kernel, bb=bb, tl=tl, c=c)
    xmap = lambda ib, it: (ib, it, 0)
    bmap = lambda ib, it: (ib, 0, 0)
    smap = lambda ib, it: (ib, 0, 0, 0)
    rmap = lambda ib, it: (ib * nt + it, 0, 0)
    outs = pl.pallas_call(
        kern,
        grid=(nb, nt),
        in_specs=[pl.BlockSpec((bb, tl, d), xmap),
                  pl.BlockSpec((bb, 8, d), bmap),
                  pl.BlockSpec((bb, GLA_HEADS, dv, dk), smap),
                  _const_spec(w["wq"].shape), _const_spec(w["wk"].shape), _const_spec(w["wv"].shape),
                  _const_spec(w["wr"].shape), _const_spec(w["wa1"].shape), _const_spec(w["wa2"].shape),
                  _const_spec(w["ba"].shape), _const_spec(w["ng"].shape), _const_spec(w["wo"].shape),
                  _const_spec(w["lng"].shape), _const_spec(w["lnb"].shape),
                  _const_spec(w["wrt"].shape), _const_spec(w["br"].shape)],
        out_specs=[pl.BlockSpec((bb, tl, d), xmap),
                   pl.BlockSpec((bb, tl, d), xmap),
                   pl.BlockSpec((1, 8, rows), rmap),
                   pl.BlockSpec((1, 8, rows), rmap),
                   pl.BlockSpec((bb, GLA_HEADS, dv, dk), smap)],
        out_shape=[jax.ShapeDtypeStruct((b, l, d), F32),
                   jax.ShapeDtypeStruct((b, l, d), F32),
                   jax.ShapeDtypeStruct((nb * nt, 8, rows), jnp.int32),
                   jax.ShapeDtypeStruct((nb * nt, 8, rows), F32),
                   jax.ShapeDtypeStruct(s0t.shape, F32)],
        scratch_shapes=[pltpu.VMEM((bb, GLA_HEADS, dv, dk), F32),
                        pltpu.VMEM((rows, hk), F32), pltpu.VMEM((rows, hk), F32),
                        pltpu.VMEM((rows, hv), F32), pltpu.VMEM((rows, hk), F32),
                        pltpu.VMEM((rows, hv), F32)],
        compiler_params=_params(("arbitrary", "arbitrary")),
        name="gla_layer",
    )(x, mod, s0t, w["wq"], w["wk"], w["wv"], w["wr"], w["wa1"], w["wa2"], w["ba"], w["ng"],
      w["wo"], w["lng"], w["lnb"], w["wrt"], w["br"])
    return outs


def _attn_kernel(*refs, bb, tl, c, has_past):
    if has_past:
        (x_ref, mod_ref, pk_ref, pv_ref, wkv_k_ref, wkv_v_ref, wq_ref, wo_ref, bias_ref, lng_ref,
         lnb_ref, wrt_ref, br_ref, x1_ref, hf_ref, ti_ref, tg_ref, ko_ref, vo_ref,
         kw_s, vw_s, q_s, o_s, sc_s, m_s, e_s) = refs
    else:
        (x_ref, mod_ref, wkv_k_ref, wkv_v_ref, wq_ref, wo_ref, bias_ref, lng_ref,
         lnb_ref, wrt_ref, br_ref, x1_ref, hf_ref, ti_ref, tg_ref, ko_ref, vo_ref,
         kw_s, vw_s, q_s, o_s, sc_s, m_s, e_s) = refs
    i = pl.program_id(1)
    d = x_ref.shape[-1]
    p = BAND_PAST
    w = p + c
    rows = bb * tl
    hd2 = 2 * (d // ATT_HEADS)

    @pl.when(i == 0)
    def _():
        if has_past:
            kw_s[:, 0:p, :] = pk_ref[...]
            vw_s[:, 0:p, :] = pv_ref[...]
        else:
            kw_s[:, 0:p, :] = jnp.zeros((bb, p, d), BF16)
            vw_s[:, 0:p, :] = jnp.zeros((bb, p, d), BF16)

    x3 = x_ref[...]
    mod = mod_ref[...]
    xb = x3.reshape(rows, d).astype(BF16)
    hb = (x3 * (1.0 + mod[:, 1:2, :]) + mod[:, 0:1, :]).reshape(rows, d).astype(BF16)
    kn = _dot(xb, wkv_k_ref[...])
    vn = _dot(xb, wkv_v_ref[...])
    ko_ref[...] = kn.reshape(bb, tl, d)
    vo_ref[...] = vn.reshape(bb, tl, d)
    kw_s[:, p:p + tl, :] = kn.astype(BF16).reshape(bb, tl, d)
    vw_s[:, p:p + tl, :] = vn.astype(BF16).reshape(bb, tl, d)
    q_s[...] = (_dot(hb, wq_ref[...]) * ((d // ATT_HEADS) ** -0.5)).astype(BF16)

    lane = lax.broadcasted_iota(jnp.int32, (c, hd2), 1)
    low = lane < (hd2 // 2)
    kpos = lax.broadcasted_iota(jnp.int32, (1, w), 1)
    chunks_per_seq = tl // c

    def chunk_body(n, carry):
        r0 = pl.multiple_of(n * c, c)
        bi = n // chunks_per_seq
        ci = n - bi * chunks_per_seq
        w0 = pl.multiple_of(ci * c, c)
        if not has_past:
            valid = (kpos + (i * tl + ci * c)) >= p
        for hp in range(ATT_HEADS // 2):
            qp = q_s[pl.ds(r0, c), hp * hd2:(hp + 1) * hd2]
            kwin = kw_s[bi, pl.ds(w0, w), hp * hd2:(hp + 1) * hd2]
            for half in range(2):
                qm = jnp.where(low if half == 0 else ~low, qp, jnp.zeros_like(qp))
                s = _dot_nt(qm, kwin) + bias_ref[2 * hp + half]
                if not has_past:
                    s = jnp.where(valid, s, NEG_INF)
                sc_s[2 * hp + half] = s
        for hh in range(ATT_HEADS):
            m_s[hh] = jnp.broadcast_to(jnp.max(sc_s[hh], axis=-1, keepdims=True), (c, LANES))
        for hh in range(ATT_HEADS):
            s = sc_s[hh]
            m = m_s[hh]
            parts = [jnp.exp(s[:, j * LANES:(j + 1) * LANES] - m) for j in range(w // LANES)]
            if w % LANES:
                parts.append(jnp.exp(s[:, w - w % LANES:] - m[:, :w % LANES]))
            e_s[hh] = jnp.concatenate(parts, axis=1).astype(BF16)
        ones = jnp.ones((w, hd2), BF16)
        for hp in range(ATT_HEADS // 2):
            vext = jnp.concatenate([vw_s[bi, pl.ds(w0, w), hp * hd2:(hp + 1) * hd2], ones], axis=1)
            outs = []
            for half in range(2):
                r = _dot(e_s[2 * hp + half], vext)
                outs.append(r[:, :hd2] / r[:, hd2:])
            o_s[pl.ds(r0, c), hp * hd2:(hp + 1) * hd2] = jnp.where(low, outs[0], outs[1])
        return carry

    lax.fori_loop(0, rows // c, chunk_body, 0)

    if tl >= p:
        kw_s[:, 0:p, :] = kw_s[:, tl:tl + p, :]
        vw_s[:, 0:p, :] = vw_s[:, tl:tl + p, :]

    mix = _dot(o_s[...].astype(BF16), wo_ref[...]).reshape(bb, tl, d)
    x1, hf, ti, tg = _post_mixer(x3, mix, mod[:, 2:3, :], mod[:, 3:4, :], mod[:, 4:5, :],
                                 lng_ref[...], lnb_ref[...], wrt_ref[...], br_ref[...])
    x1_ref[...] = x1
    hf_ref[...] = hf
    ti_ref[0] = ti
    tg_ref[0] = tg


def _attn_layer(x, mod, past_k, past_v, w, bb, tl):
    b, l, d = x.shape
    c = min(CHUNK, l)
    has_past = past_k is not None
    nb, nt = b // bb, l // tl
    assert nt == 1 or (tl >= BAND_PAST and bb == 1)
    rows = bb * tl
    keep = min(BAND_PAST, l)
    assert keep == tl or (keep == BAND_PAST and tl == BAND_PAST)
    kern = functools.partial(_attn_kernel, bb=bb, tl=tl, c=c, has_past=has_past)
    xmap = lambda ib, it: (ib, it, 0)
    bmap = lambda ib, it: (ib, 0, 0)
    rmap = lambda ib, it: (ib * nt + it, 0, 0)
    in_specs = [pl.BlockSpec((bb, tl, d), xmap), pl.BlockSpec((bb, 8, d), bmap)]
    args = [x, mod]
    if has_past:
        in_specs += [pl.BlockSpec((bb, BAND_PAST, d), bmap)] * 2
        args += [past_k, past_v]
    names = ["wkv_k", "wkv_v", "wq", "wo", "bias", "lng", "lnb", "wrt", "br"]
    in_specs += [_const_spec(w[n].shape) for n in names]
    args += [w[n] for n in names]
    outs = pl.pallas_call(
        kern,
        grid=(nb, nt),
        in_specs=in_specs,
        out_specs=[pl.BlockSpec((bb, tl, d), xmap),
                   pl.BlockSpec((bb, tl, d), xmap),
                   pl.BlockSpec((1, 8, rows), rmap),
                   pl.BlockSpec((1, 8, rows), rmap),
                   pl.BlockSpec((bb, tl, d), bmap),
                   pl.BlockSpec((bb, tl, d), bmap)],
        out_shape=[jax.ShapeDtypeStruct((b, l, d), F32),
                   jax.ShapeDtypeStruct((b, l, d), F32),
                   jax.ShapeDtypeStruct((nb * nt, 8, rows), jnp.int32),
                   jax.ShapeDtypeStruct((nb * nt, 8, rows), F32),
                   jax.ShapeDtypeStruct((b, keep, d), F32),
                   jax.ShapeDtypeStruct((b, keep, d), F32)],
        scratch_shapes=[pltpu.VMEM((bb, BAND_PAST + tl, d), BF16),
                        pltpu.VMEM((bb, BAND_PAST + tl, d), BF16),
                        pltpu.VMEM((rows, d), BF16),
                        pltpu.VMEM((rows, d), F32),
                        pltpu.VMEM((ATT_HEADS, c, BAND_PAST + c), F32),
                        pltpu.VMEM((ATT_HEADS, c, LANES), F32),
                        pltpu.VMEM((ATT_HEADS, c, BAND_PAST + c), BF16)],
        compiler_params=_params(("arbitrary", "arbitrary")),
        name="attn_layer",
    )(*args)
    return outs


def _moe_kernel(be_ref, nact_ref, st_ref, stn_ref, sr_ref, hf_hbm, wg_ref, bg_ref, wu_ref, bu_ref,
                wd_ref, bd_ref, y_hbm, xbuf, ybuf, xb_s, hm_s, wgb, wub, wdb, xs_s, ws_s, gsem, ssem,
                *, blk):
    i = pl.program_id(0)
    nact = nact_ref[0]
    f = hm_s.shape[1]
    n_col = 4
    fc = f // n_col
    per = blk // n_col
    rc = min(blk, 256)

    def gather_row(tok_ref, j):
        return pltpu.make_async_copy(hf_hbm.at[pl.ds(tok_ref[0, 0, j], 1)], xbuf.at[pl.ds(j, 1)], gsem)

    def gather_all():
        return pltpu.make_async_copy(hf_hbm.at[pl.ds(0, blk)], xbuf, gsem)

    def scatter_all():
        return pltpu.make_async_copy(ybuf, y_hbm.at[pl.ds(y_hbm.shape[0] - blk, blk)], ssem)

    @pl.when(i == 0)
    def _():
        ybuf[...] = jnp.zeros(ybuf.shape, F32)
        scatter_all().start()

        def prime(j, carry):
            gather_row(st_ref, j).start()
            return carry

        lax.fori_loop(0, blk, prime, 0)

    @pl.when(i < nact)
    def _():
        e = be_ref[i]
        prev = be_ref[jnp.maximum(i - 1, 0)]

        @pl.when((i == 0) | (e != prev))
        def _():
            for k, (w_ref, w8) in enumerate(((wg_ref, wgb), (wu_ref, wub), (wd_ref, wdb))):
                w = w_ref[0, 0]
                top = jnp.max(jnp.max(jnp.abs(w), axis=0, keepdims=True), axis=1, keepdims=True)
                top = jnp.maximum(top, 1e-30)
                w8[...] = (w * (F8_MAX / top)).astype(F8)
                ws_s[k:k + 1, :] = jnp.broadcast_to(top * (1.0 / F8_MAX), (1, LANES))

        gather_all().wait()
        x = xbuf[...]
        row_top = jnp.maximum(jnp.max(jnp.abs(x), axis=-1, keepdims=True), 1e-30)
        xb_s[...] = (x * (F8_MAX / row_top)).astype(F8)
        xs_s[...] = jnp.broadcast_to(row_top * (1.0 / F8_MAX), xs_s.shape)
        h_scale = F8_MAX / ((SWIGLU_LIMIT + 1.0) * SWIGLU_LIMIT)
        for c in range(n_col):
            cols = slice(c * fc, (c + 1) * fc)
            xinv = xs_s[:, 0:1]
            a = jnp.minimum(_dot(xb_s[...], wgb[:, cols]) * (xinv * ws_s[0:1, 0:1]) + bg_ref[0, 0][:, cols],
                            SWIGLU_LIMIT)
            u = jnp.clip(_dot(xb_s[...], wub[:, cols]) * (xinv * ws_s[1:2, 0:1]) + bu_ref[0, 0][:, cols],
                         -SWIGLU_LIMIT, SWIGLU_LIMIT)
            hm_s[:, cols] = ((u + 1.0) * (a * jax.nn.sigmoid(SWIGLU_ALPHA * a)) * h_scale).astype(F8)
            for j in range(c * per, (c + 1) * per):
                gather_row(stn_ref, j).start()
        scatter_all().wait()
        for r0 in range(0, blk, rc):
            ybuf[r0:r0 + rc, :] = (_dot(hm_s[r0:r0 + rc, :], wdb[...]) * (ws_s[2:3, 0:1] * (1.0 / h_scale))
                                   + bd_ref[0, 0])
            for j in range(r0, r0 + rc):
                pltpu.make_async_copy(ybuf.at[pl.ds(j, 1)], y_hbm.at[pl.ds(sr_ref[0, 0, j], 1)], ssem).start()

        @pl.when(i == nact - 1)
        def _():
            scatter_all().wait()
            gather_all().wait()


def _moe_experts(hf2d, block_e, n_active, slot_t, slot_r, layer, wg, bg, wu, bu, wd, bd, blk):
    t, d = hf2d.shape
    f = wg.shape[-1]
    n_blocks = block_e.shape[0]
    kern = functools.partial(_moe_kernel, blk=blk)
    wmap = lambda i, be, na: (layer, be[i], 0, 0)
    smap = lambda i, be, na: (i, 0, 0)
    nmap = lambda i, be, na: (jnp.minimum(i + 1, n_blocks - 1), 0, 0)
    grid_spec = pltpu.PrefetchScalarGridSpec(
        num_scalar_prefetch=2,
        grid=(n_blocks,),
        in_specs=[pl.BlockSpec((1, 1, blk), smap, memory_space=pltpu.SMEM),
                  pl.BlockSpec((1, 1, blk), nmap, memory_space=pltpu.SMEM),
                  pl.BlockSpec((1, 1, blk), smap, memory_space=pltpu.SMEM),
                  pl.BlockSpec(memory_space=pl.ANY),
                  pl.BlockSpec((1, 1, d, f), wmap), pl.BlockSpec((1, 1, 1, f), wmap),
                  pl.BlockSpec((1, 1, d, f), wmap), pl.BlockSpec((1, 1, 1, f), wmap),
                  pl.BlockSpec((1, 1, f, d), wmap), pl.BlockSpec((1, 1, 1, d), wmap)],
        out_specs=pl.BlockSpec(memory_space=pl.ANY),
        scratch_shapes=[pltpu.VMEM((blk, d), F32), pltpu.VMEM((blk, d), F32),
                        pltpu.VMEM((blk, d), F8), pltpu.VMEM((blk, f), F8),
                        pltpu.VMEM((d, f), F8), pltpu.VMEM((d, f), F8), pltpu.VMEM((f, d), F8),
                        pltpu.VMEM((blk, LANES), F32), pltpu.VMEM((8, LANES), F32),
                        pltpu.SemaphoreType.DMA, pltpu.SemaphoreType.DMA])
    st3 = slot_t.reshape(n_blocks, 1, blk)
    depth = wg.shape[0]
    return pl.pallas_call(
        kern,
        grid_spec=grid_spec,
        out_shape=jax.ShapeDtypeStruct((TOP_K * t + blk, d), F32),
        compiler_params=_params(("arbitrary",)),
        name="moe_experts",
    )(block_e, n_active, st3, st3, slot_r.reshape(n_blocks, 1, blk), hf2d,
      wg, bg.reshape(depth, N_EXPERTS, 1, f), wu, bu.reshape(depth, N_EXPERTS, 1, f),
      wd, bd.reshape(depth, N_EXPERTS, 1, d))


def _route(ti8, blk):
    nb, _, rows = ti8.shape
    t = nb * rows
    npair = t * TOP_K
    assert npair % blk == 0
    flat_e = ti8[:, :TOP_K, :].reshape(-1)
    experts = jnp.arange(N_EXPERTS, dtype=jnp.int32)
    counts = jnp.sum((flat_e[:, None] == experts[None, :]).astype(jnp.int32), axis=0)
    padded = (counts + blk - 1) // blk * blk
    pend = jnp.cumsum(padded)
    need = padded - counts
    pair_bits = 19
    assert npair <= 1 << pair_bits and blk <= 1 << pair_bits
    within = jnp.arange(blk, dtype=jnp.int32)[None, :]
    dummy_key = jnp.where(within < need[:, None],
                          (experts[:, None] << (pair_bits + 1)) | (1 << pair_bits) | within,
                          ((2 * N_EXPERTS) << (pair_bits + 1)) | (1 << pair_bits))
    real_key = (flat_e << (pair_bits + 1)) | jnp.arange(npair, dtype=jnp.int32)
    skey = lax.sort(jnp.concatenate([real_key, dummy_key.reshape(-1)]))
    n_blocks = npair // blk + N_EXPERTS
    real = ((skey >> pair_bits) & 1) == 0
    q = jnp.where(real, skey & ((1 << pair_bits) - 1), 0)
    tok = (q // (TOP_K * rows)) * rows + q % rows
    choice = (q // rows) % TOP_K
    slot_t = jnp.where(real, tok, 0)
    spare = npair + jnp.arange(n_blocks * blk, dtype=jnp.int32) % blk
    slot_r = jnp.where(real, choice * t + tok, spare)
    bstart = jnp.arange(n_blocks, dtype=jnp.int32) * blk
    block_e = jnp.minimum(jnp.sum((pend[None, :] <= bstart[:, None]).astype(jnp.int32), axis=1),
                          N_EXPERTS - 1)
    n_active = (pend[-1] // blk).astype(jnp.int32).reshape(1)
    return block_e, n_active, slot_t, slot_r


def _combine_kernel(x_ref, mod_ref, tg_ref, y0, y1, y2, y3, lng_ref, lnb_ref, o_ref):
    bb, tl, d = x_ref.shape
    rows = bb * tl
    sel = (lax.broadcasted_iota(jnp.int32, (8, LANES), 0) ==
           lax.broadcasted_iota(jnp.int32, (8, LANES), 1)).astype(F32)
    gt = lax.dot_general(tg_ref[0], sel, (((0,), (0,)), ((), ())), preferred_element_type=F32,
                         precision=lax.Precision.HIGHEST)
    ys = (y0, y1, y2, y3)
    terms = [gt[:, k:k + 1] * ys[k][...] for k in range(TOP_K)]
    ff = ((terms[0] + terms[1]) + (terms[2] + terms[3])).reshape(bb, tl, d)
    g_f = mod_ref[:, 5:6, :]
    o_ref[...] = _layer_norm(DEEPNORM_ALPHA * x_ref[...] + g_f * ff, lng_ref[...], lnb_ref[...])


def _combine_layer(x1, mod, tg8, y, lng, lnb, bb, tl):
    b, l, d = x1.shape
    nb, nt = b // bb, l // tl
    rows = bb * tl
    tb = (b * l) // rows
    ymaps = [functools.partial(lambda ib, it, k: (k * tb + ib * nt + it, 0), k=k) for k in range(TOP_K)]
    return pl.pallas_call(
        _combine_kernel,
        grid=(nb, nt),
        in_specs=[pl.BlockSpec((bb, tl, d), lambda ib, it: (ib, it, 0)),
                  pl.BlockSpec((bb, 8, d), lambda ib, it: (ib, 0, 0)),
                  pl.BlockSpec((1, 8, rows), lambda ib, it: (ib * nt + it, 0, 0))]
                 + [pl.BlockSpec((rows, d), m) for m in ymaps]
                 + [_const_spec(lng.shape), _const_spec(lnb.shape)],
        out_specs=pl.BlockSpec((bb, tl, d), lambda ib, it: (ib, it, 0)),
        out_shape=jax.ShapeDtypeStruct((b, l, d), F32),
        compiler_params=_params(("arbitrary", "arbitrary")),
        name="moe_combine",
    )(x1, mod, tg8, y, y, y, y, lng, lnb)


def _moe_layer(x1, hf, ti8, tg8, mod, layer, moe_w, lng, lnb, blk, bb, tl):
    block_e, n_active, slot_t, slot_r = _route(ti8, blk)
    y = _moe_experts(hf.reshape(-1, hf.shape[-1]), block_e, n_active, slot_t, slot_r, layer, *moe_w, blk)
    return _combine_layer(x1, mod, tg8, y, lng, lnb, bb, tl)


def _rel_bias_table(rel_bias, c):
    w = BAND_PAST + c
    m = jnp.arange(w + c - 1)
    f = rel_bias[:, jnp.clip(BAND_PAST + c - 1 - m, -REL_CLIP, REL_CLIP) + REL_CLIP].astype(F32)
    return jnp.stack([f[:, c - 1 - t:c - 1 - t + w] for t in range(c)], axis=1)


def _trunk(x, mods, s0, past_k, past_v, wts, gla_bb, att_bb, blk):
    b, l, d = x.shape
    tl = min(TOKEN_TILE, l)
    c = min(CHUNK, l)
    row = lambda v: v.reshape(1, -1)

    def mod_of(layer):
        m = mods[layer].reshape(b, 6, d)
        return jnp.concatenate([m, jnp.zeros((b, 2, d), F32)], axis=1)

    def router_w(layer):
        return wts["moe_wr"][layer].T, wts["moe_br"][layer].reshape(N_EXPERTS, 1)

    moe_w = (wts["moe_wg"], wts["moe_bg"], wts["moe_wu"], wts["moe_bu"], wts["moe_wd"], wts["moe_bd"])

    wrt, br = router_w(0)
    gw = dict(wq=wts["gla_wq"][0].astype(BF16), wk=wts["gla_wk"][0].astype(BF16),
              wv=wts["gla_wv"][0].astype(BF16), wr=wts["gla_wr"][0].astype(BF16),
              wa1=wts["gla_wa1"][0].astype(BF16), wa2=wts["gla_wa2"][0].astype(BF16),
              ba=row(wts["gla_ba"][0]), ng=row(wts["gla_norm_g"][0]),
              wo=wts["gla_wo"][0].astype(BF16), lng=row(wts["ln_g"][0, 0]), lnb=row(wts["ln_b"][0, 0]),
              wrt=wrt, br=br)
    mod0 = mod_of(0)
    s0t = jnp.swapaxes(s0, -1, -2)
    x1, hf, ti8, tg8, sfin_t = _gla_layer(x, mod0, s0t, gw, gla_bb, tl)
    x = _moe_layer(x1, hf, ti8, tg8, mod0, 0, moe_w, row(wts["ln_g"][0, 1]), row(wts["ln_b"][0, 1]),
                   blk, gla_bb, tl)
    s_fin = jnp.swapaxes(sfin_t, -1, -2)[None]

    wrt, br = router_w(1)
    aw = dict(wkv_k=wts["kv_wk"].astype(BF16), wkv_v=wts["kv_wv"].astype(BF16),
              wq=wts["att_wq"][0].astype(BF16), wo=wts["att_wo"][0].astype(BF16),
              bias=_rel_bias_table(wts["att_rel_bias"][0], c),
              lng=row(wts["ln_g"][1, 0]), lnb=row(wts["ln_b"][1, 0]), wrt=wrt, br=br)
    mod1 = mod_of(1)
    x1, hf, ti8, tg8, k_new, v_new = _attn_layer(x, mod1, past_k, past_v, aw, att_bb, tl)
    x = _moe_layer(x1, hf, ti8, tg8, mod1, 1, moe_w, row(wts["ln_g"][1, 1]), row(wts["ln_b"][1, 1]),
                   blk, att_bb, tl)
    keep = k_new.shape[1]
    hd = d // ATT_HEADS
    return (x, s_fin, k_new.reshape(b, keep, ATT_HEADS, hd), v_new.reshape(b, keep, ATT_HEADS, hd))


def kernel(x_prompt, x_sample, c_prompt, c_sample, state_gla, cache_k, cache_v, ada_w, ada_b, ln_g, ln_b, gla_wq, gla_wk, gla_wv, gla_wa1, gla_wa2, gla_ba, gla_wr, gla_norm_g, gla_wo, kv_wk, kv_wv, att_wq, att_wo, att_rel_bias, moe_wr, moe_br, moe_wg, moe_bg, moe_wu, moe_bu, moe_wd, moe_bd):
    wts = dict(ln_g=ln_g, ln_b=ln_b, gla_wq=gla_wq, gla_wk=gla_wk, gla_wv=gla_wv, gla_wa1=gla_wa1,
               gla_wa2=gla_wa2, gla_ba=gla_ba, gla_wr=gla_wr, gla_norm_g=gla_norm_g, gla_wo=gla_wo,
               kv_wk=kv_wk, kv_wv=kv_wv, att_wq=att_wq, att_wo=att_wo, att_rel_bias=att_rel_bias,
               moe_wr=moe_wr, moe_br=moe_br, moe_wg=moe_wg, moe_bg=moe_bg, moe_wu=moe_wu,
               moe_bu=moe_bu, moe_wd=moe_wd, moe_bd=moe_bd)
    b_p, l_p, d = x_prompt.shape
    b_s, l_s, _ = x_sample.shape
    assert cache_k.shape[1] == BAND_PAST
    mods = _ada_modulation(jnp.concatenate([c_prompt, c_sample], axis=0), ada_w, ada_b)
    mods_p, mods_s = mods[:, :b_p], mods[:, b_p:]

    s0_p = jnp.zeros((b_p,) + state_gla.shape[2:], F32)
    blk_p = MOE_BLOCK_LARGE if b_p * l_p * TOP_K >= 64 * MOE_BLOCK_LARGE else MOE_BLOCK_SMALL
    y_p, s_p, k_p, v_p = _trunk(x_prompt, mods_p, s0_p, None, None, wts, 1, 1, blk_p)

    pk = cache_k.reshape(b_s, BAND_PAST, d).astype(BF16)
    pv = cache_v.reshape(b_s, BAND_PAST, d).astype(BF16)
    blk_s = MOE_BLOCK_LARGE if b_s * l_s * TOP_K >= 64 * MOE_BLOCK_LARGE else MOE_BLOCK_SMALL
    y_s, s_s, k_s, v_s = _trunk(x_sample, mods_s, state_gla[0], pk, pv, wts, b_s, min(4, b_s), blk_s)
    return (y_p, y_s, s_p, k_p, v_p, s_s, k_s, v_s)
```

```python
import functools
import math

import jax
import jax.numpy as jnp
from jax import lax
from jax.experimental import pallas as pl
from jax.experimental.pallas import tpu as pltpu

F32 = jnp.float32
BF16 = jnp.bfloat16
F8 = jnp.float8_e4m3fn
F8_MAX = 448.0

CHUNK = 64
GLA_HEADS = 4
GLA_TAU = 16.0
ATT_HEADS = 16
BAND_PAST = 512
REL_CLIP = 128
N_EXPERTS = 32
TOP_K = 4
SWIGLU_LIMIT = 7.0
SWIGLU_ALPHA = 1.702
DEPTH = 2
DEEPNORM_ALPHA = (2.0 * DEPTH) ** 0.25
LN_EPS = 1e-5
NEG_INF = -1e30

V7X_VMEM_LIMIT_BYTES = 56 * 1024 * 1024
TOKEN_TILE = 512
GLA_SUB = 16
MOE_BLOCK_LARGE = 512
MOE_BLOCK_SMALL = 64
LANES = 128


def _dot(a, b):
    return jnp.dot(a, b, preferred_element_type=F32)


def _dot_nt(a, b):
    return lax.dot_general(a, b, (((1,), (1,)), ((), ())), preferred_element_type=F32)


def _dot_tn(a, b):
    return lax.dot_general(a, b, (((0,), (0,)), ((), ())), preferred_element_type=F32)


def _layer_norm(y, g, b):
    mu = jnp.mean(y, axis=-1, keepdims=True)
    yc = y - mu
    var = jnp.mean(yc * yc, axis=-1, keepdims=True)
    return yc * lax.rsqrt(var + LN_EPS) * g + b


def _store_token_major(ref, val):
    rows, d = val.shape
    g = d // LANES
    for s in range(g):
        ref[pl.ds(s, rows, stride=g), :] = val[:, s * LANES:(s + 1) * LANES]


def _load_token_major(ref, rows, d):
    g = d // LANES
    return jnp.concatenate([ref[pl.ds(s, rows, stride=g), :] for s in range(g)], axis=1)


def _const_spec(shape):
    nd = len(shape)
    return pl.BlockSpec(shape, lambda *_: (0,) * nd, pipeline_mode=pl.Buffered(1))


def _params(sem):
    return pltpu.CompilerParams(dimension_semantics=sem, vmem_limit_bytes=V7X_VMEM_LIMIT_BYTES)


def _ada_kernel(c_ref, w_ref, b_ref, o_ref):
    c = c_ref[...]
    s = c * jax.nn.sigmoid(c)
    o_ref[0] = jnp.dot(s, w_ref[0], preferred_element_type=F32,
                       precision=lax.Precision.HIGHEST) + b_ref[0]


def _ada_modulation(c, ada_w, ada_b):
    n, d = c.shape
    depth, _, d6 = ada_w.shape
    tn = d6 // 4
    return pl.pallas_call(
        _ada_kernel,
        grid=(depth, d6 // tn),
        in_specs=[pl.BlockSpec((n, d), lambda l, j: (0, 0)),
                  pl.BlockSpec((1, d, tn), lambda l, j: (l, 0, j)),
                  pl.BlockSpec((1, 1, tn), lambda l, j: (l, 0, j))],
        out_specs=pl.BlockSpec((1, n, tn), lambda l, j: (l, 0, j)),
        out_shape=jax.ShapeDtypeStruct((depth, n, d6), F32),
        compiler_params=_params(("arbitrary", "arbitrary")),
        name="ada_modulation",
    )(c, ada_w, ada_b.reshape(depth, 1, d6))


def _post_mixer(x3, mix3, g_m, sh_f, sc_f, lng, lnb, wrt, br):
    bb, tl, d = x3.shape
    rows = bb * tl
    x1 = _layer_norm(DEEPNORM_ALPHA * x3 + g_m * mix3, lng, lnb)
    hf = x1 * (1.0 + sc_f) + sh_f
    logits = lax.dot_general(wrt, hf.reshape(rows, d), (((1,), (1,)), ((), ())),
                             preferred_element_type=F32, precision=lax.Precision.HIGHEST) + br
    eidx = lax.broadcasted_iota(jnp.int32, logits.shape, 0).astype(F32)
    vals, idxs = [], []
    cur = logits
    for _ in range(TOP_K):
        m = jnp.max(cur, axis=0, keepdims=True)
        sel = jnp.min(jnp.where(cur == m, eidx, float(N_EXPERTS)), axis=0, keepdims=True)
        vals.append(m)
        idxs.append(sel)
        cur = jnp.where(eidx == sel, -jnp.inf, cur)
    ex = [jnp.exp(v - vals[0]) for v in vals]
    den = (ex[0] + ex[1]) + (ex[2] + ex[3])
    out_row = lax.broadcasted_iota(jnp.int32, (8, rows), 0)
    ti = jnp.zeros((8, rows), F32)
    tg = jnp.zeros((8, rows), F32)
    for k in range(TOP_K):
        ti = jnp.where(out_row == k, idxs[k], ti)
        tg = jnp.where(out_row == k, ex[k] / den, tg)
    return x1, hf, ti.astype(jnp.int32), tg


def _gla_kernel(x_ref, mod_ref, s0_ref, wq_ref, wk_ref, wv_ref, wr_ref, wa1_ref, wa2_ref, ba_ref,
                ng_ref, wo_ref, lng_ref, lnb_ref, wrt_ref, br_ref,
                x1_ref, hf_ref, ti_ref, tg_ref, sfin_ref,
                st_s, q_s, k_s, v_s, la_s, o_s, *, bb, tl, c):
    i = pl.program_id(1)
    d = x_ref.shape[-1]
    dk = q_s.shape[-1] // GLA_HEADS
    dv = v_s.shape[-1] // GLA_HEADS
    rows = bb * tl
    sub = min(GLA_SUB, c)

    @pl.when(i == 0)
    def _():
        st_s[...] = s0_ref[...]

    x3 = x_ref[...]
    mod = mod_ref[...]
    h = (x3 * (1.0 + mod[:, 1:2, :]) + mod[:, 0:1, :]).reshape(rows, d)
    hb = h.astype(BF16)
    q_s[...] = _dot(hb, wq_ref[...]) * (dk ** -0.5)
    k_s[...] = _dot(hb, wk_ref[...])
    v_s[...] = _dot(hb, wv_ref[...])
    a = _dot(_dot(hb, wa1_ref[...]).astype(BF16), wa2_ref[...]) + ba_ref[...]
    la_s[...] = (jnp.minimum(a, 0.0) - jnp.log1p(jnp.exp(-jnp.abs(a)))) * (1.0 / GLA_TAU)

    tril = (lax.broadcasted_iota(jnp.int32, (c, c), 0) >=
            lax.broadcasted_iota(jnp.int32, (c, c), 1)).astype(BF16)
    lane_c = lax.broadcasted_iota(jnp.int32, (sub, c), 1)
    row_c = lax.broadcasted_iota(jnp.int32, (sub, c), 0)
    krow = lax.broadcasted_iota(jnp.int32, (c, dk), 0)
    ng = ng_ref[...]
    chunks_per_seq = tl // c

    def chunk_body(n, carry):
        r0 = pl.multiple_of(n * c, c)
        bi = n // chunks_per_seq
        la = la_s[pl.ds(r0, c), :]
        la_hi = la.astype(BF16)
        la_lo = (la - la_hi.astype(F32)).astype(BF16)
        bcum = _dot(tril, la_hi) + _dot(tril, la_lo)
        for hd in range(GLA_HEADS):
            qh = q_s[pl.ds(r0, c), hd * dk:(hd + 1) * dk]
            kh = k_s[pl.ds(r0, c), hd * dk:(hd + 1) * dk]
            vh = v_s[pl.ds(r0, c), hd * dv:(hd + 1) * dv].astype(BF16)
            bh = bcum[:, hd * dk:(hd + 1) * dk]
            blocks = []
            for sb in range(c // sub):
                lo = sb * sub
                qi = qh[lo:lo + sub]
                bi_rows = bh[lo:lo + sub]
                if sb > 0:
                    ref_b = bh[lo:lo + 1]
                    qt = (qi * jnp.exp(bi_rows - ref_b)).astype(BF16)
                    kt = jnp.where(krow < lo, kh * jnp.exp(jnp.minimum(ref_b - bh, 0.0)), 0.0)
                    acc = _dot_nt(qt, kt.astype(BF16))
                else:
                    acc = jnp.zeros((sub, c), F32)
                for s in range(sub):
                    g = lo + s
                    e = jnp.exp(jnp.minimum(bi_rows - bh[g:g + 1], 0.0))
                    col = jnp.sum(qi * e * kh[g:g + 1], axis=-1, keepdims=True)
                    acc = jnp.where((lane_c == g) & (row_c >= s), col, acc)
                blocks.append(acc)
            amat = blocks[0] if len(blocks) == 1 else jnp.concatenate(blocks, axis=0)
            st = st_s[bi, hd]
            o = _dot(amat.astype(BF16), vh) + _dot_nt((qh * jnp.exp(bh)).astype(BF16), st.astype(BF16))
            ms = jnp.mean(o * o, axis=-1, keepdims=True)
            o_s[pl.ds(r0, c), hd * dv:(hd + 1) * dv] = o * lax.rsqrt(ms + LN_EPS) * ng
            bl = bh[c - 1:c]
            kd = (kh * jnp.exp(bl - bh)).astype(BF16)
            st_s[bi, hd] = st * jnp.exp(bl) + _dot_tn(vh, kd)
        return carry

    lax.fori_loop(0, rows // c, chunk_body, 0)

    r = _dot(hb, wr_ref[...])
    og = (o_s[...] * (r * jax.nn.sigmoid(r))).astype(BF16)
    mix = _dot(og, wo_ref[...]).reshape(bb, tl, d)
    x1, hf, ti, tg = _post_mixer(x3, mix, mod[:, 2:3, :], mod[:, 3:4, :], mod[:, 4:5, :],
                                 lng_ref[...], lnb_ref[...], wrt_ref[...], br_ref[...])
    x1_ref[...] = x1
    _store_token_major(hf_ref, hf.reshape(rows, d))
    ti_ref[0] = ti
    tg_ref[0] = tg

    @pl.when(i == pl.num_programs(1) - 1)
    def _():
        sfin_ref[...] = st_s[...]


def _gla_layer(x, mod, s0t, w, bb, tl):
    b, l, d = x.shape
    c = min(CHUNK, l)
    hk = w["wq"].shape[1]
    hv = w["wv"].shape[1]
    nb, nt = b // bb, l // tl
    assert bb == 1 or nt == 1
    rows = bb * tl
    dv, dk = s0t.shape[2], s0t.shape[3]
    kern = functools.partial(_gla_kernel, bb=bb, tl=tl, c=c)
    xmap = lambda ib, it: (ib, it, 0)
    bmap = lambda ib, it: (ib, 0, 0)
    smap = lambda ib, it: (ib, 0, 0, 0)
    rmap = lambda ib, it: (ib * nt + it, 0, 0)
    outs = pl.pallas_call(
        kern,
        grid=(nb, nt),
        in_specs=[pl.BlockSpec((bb, tl, d), xmap),
                  pl.BlockSpec((bb, 8, d), bmap),
                  pl.BlockSpec((bb, GLA_HEADS, dv, dk), smap),
                  _const_spec(w["wq"].shape), _const_spec(w["wk"].shape), _const_spec(w["wv"].shape),
                  _const_spec(w["wr"].shape), _const_spec(w["wa1"].shape), _const_spec(w["wa2"].shape),
                  _const_spec(w["ba"].shape), _const_spec(w["ng"].shape), _const_spec(w["wo"].shape),
                  _const_spec(w["lng"].shape), _const_spec(w["lnb"].shape),
                  _const_spec(w["wrt"].shape), _const_spec(w["br"].shape)],
        out_specs=[pl.BlockSpec((bb, tl, d), xmap),
                   pl.BlockSpec((rows * (d // LANES), LANES), lambda ib, it: (ib * nt + it, 0)),
                   pl.BlockSpec((1, 8, rows), rmap),
                   pl.BlockSpec((1, 8, rows), rmap),
                   pl.BlockSpec((bb, GLA_HEADS, dv, dk), smap)],
        out_shape=[jax.ShapeDtypeStruct((b, l, d), F32),
                   jax.ShapeDtypeStruct((b * l * (d // LANES), LANES), F32),
                   jax.ShapeDtypeStruct((nb * nt, 8, rows), jnp.int32),
                   jax.ShapeDtypeStruct((nb * nt, 8, rows), F32),
                   jax.ShapeDtypeStruct(s0t.shape, F32)],
        scratch_shapes=[pltpu.VMEM((bb, GLA_HEADS, dv, dk), F32),
                        pltpu.VMEM((rows, hk), F32), pltpu.VMEM((rows, hk), F32),
                        pltpu.VMEM((rows, hv), F32), pltpu.VMEM((rows, hk), F32),
                        pltpu.VMEM((rows, hv), F32)],
        compiler_params=_params(("arbitrary", "arbitrary")),
        name="gla_layer",
    )(x, mod, s0t, w["wq"], w["wk"], w["wv"], w["wr"], w["wa1"], w["wa2"], w["ba"], w["ng"],
      w["wo"], w["lng"], w["lnb"], w["wrt"], w["br"])
    return outs


def _attn_kernel(*refs, bb, tl, c, has_past):
    if has_past:
        (x_ref, mod_ref, pk_ref, pv_ref, wkv_k_ref, wkv_v_ref, wq_ref, wo_ref, bias_ref, lng_ref,
         lnb_ref, wrt_ref, br_ref, x1_ref, hf_ref, ti_ref, tg_ref, ko_ref, vo_ref,
         kw_s, vw_s, q_s, o_s, sc_s, m_s, e_s) = refs
    else:
        (x_ref, mod_ref, wkv_k_ref, wkv_v_ref, wq_ref, wo_ref, bias_ref, lng_ref,
         lnb_ref, wrt_ref, br_ref, x1_ref, hf_ref, ti_ref, tg_ref, ko_ref, vo_ref,
         kw_s, vw_s, q_s, o_s, sc_s, m_s, e_s) = refs
    i = pl.program_id(1)
    d = x_ref.shape[-1]
    p = BAND_PAST
    w = p + c
    rows = bb * tl
    hd2 = 2 * (d // ATT_HEADS)

    @pl.when(i == 0)
    def _():
        if has_past:
            kw_s[:, 0:p, :] = pk_ref[...]
            vw_s[:, 0:p, :] = pv_ref[...]
        else:
            kw_s[:, 0:p, :] = jnp.zeros((bb, p, d), BF16)
            vw_s[:, 0:p, :] = jnp.zeros((bb, p, d), BF16)

    x3 = x_ref[...]
    mod = mod_ref[...]
    xb = x3.reshape(rows, d).astype(BF16)
    hb = (x3 * (1.0 + mod[:, 1:2, :]) + mod[:, 0:1, :]).reshape(rows, d).astype(BF16)
    kn = _dot(xb, wkv_k_ref[...])
    vn = _dot(xb, wkv_v_ref[...])
    ko_ref[...] = kn.reshape(bb, tl, d)
    vo_ref[...] = vn.reshape(bb, tl, d)
    kw_s[:, p:p + tl, :] = kn.astype(BF16).reshape(bb, tl, d)
    vw_s[:, p:p + tl, :] = vn.astype(BF16).reshape(bb, tl, d)
    q_s[...] = (_dot(hb, wq_ref[...]) * ((d // ATT_HEADS) ** -0.5)).astype(BF16)

    lane = lax.broadcasted_iota(jnp.int32, (c, hd2), 1)
    low = lane < (hd2 // 2)
    kpos = lax.broadcasted_iota(jnp.int32, (1, w), 1)
    chunks_per_seq = tl // c

    def chunk_body(n, carry):
        r0 = pl.multiple_of(n * c, c)
        bi = n // chunks_per_seq
        ci = n - bi * chunks_per_seq
        w0 = pl.multiple_of(ci * c, c)
        if not has_past:
            valid = (kpos + (i * tl + ci * c)) >= p
        for hp in range(ATT_HEADS // 2):
            qp = q_s[pl.ds(r0, c), hp * hd2:(hp + 1) * hd2]
            kwin = kw_s[bi, pl.ds(w0, w), hp * hd2:(hp + 1) * hd2]
            for half in range(2):
                qm = jnp.where(low if half == 0 else ~low, qp, jnp.zeros_like(qp))
                s = _dot_nt(qm, kwin) + bias_ref[2 * hp + half]
                if not has_past:
                    s = jnp.where(valid, s, NEG_INF)
                sc_s[2 * hp + half] = s
        for hh in range(ATT_HEADS):
            m_s[hh] = jnp.broadcast_to(jnp.max(sc_s[hh], axis=-1, keepdims=True), (c, LANES))
        for hh in range(ATT_HEADS):
            s = sc_s[hh]
            m = m_s[hh]
            parts = [jnp.exp(s[:, j * LANES:(j + 1) * LANES] - m) for j in range(w // LANES)]
            if w % LANES:
                parts.append(jnp.exp(s[:, w - w % LANES:] - m[:, :w % LANES]))
            e_s[hh] = jnp.concatenate(parts, axis=1).astype(BF16)
        ones = jnp.ones((w, hd2), BF16)
        for hp in range(ATT_HEADS // 2):
            vext = jnp.concatenate([vw_s[bi, pl.ds(w0, w), hp * hd2:(hp + 1) * hd2], ones], axis=1)
            outs = []
            for half in range(2):
                r = _dot(e_s[2 * hp + half], vext)
                outs.append(r[:, :hd2] / r[:, hd2:])
            o_s[pl.ds(r0, c), hp * hd2:(hp + 1) * hd2] = jnp.where(low, outs[0], outs[1])
        return carry

    lax.fori_loop(0, rows // c, chunk_body, 0)

    if tl >= p:
        kw_s[:, 0:p, :] = kw_s[:, tl:tl + p, :]
        vw_s[:, 0:p, :] = vw_s[:, tl:tl + p, :]

    mix = _dot(o_s[...].astype(BF16), wo_ref[...]).reshape(bb, tl, d)
    x1, hf, ti, tg = _post_mixer(x3, mix, mod[:, 2:3, :], mod[:, 3:4, :], mod[:, 4:5, :],
                                 lng_ref[...], lnb_ref[...], wrt_ref[...], br_ref[...])
    x1_ref[...] = x1
    _store_token_major(hf_ref, hf.reshape(rows, d))
    ti_ref[0] = ti
    tg_ref[0] = tg


def _attn_layer(x, mod, past_k, past_v, w, bb, tl):
    b, l, d = x.shape
    c = min(CHUNK, l)
    has_past = past_k is not None
    nb, nt = b // bb, l // tl
    assert nt == 1 or (tl >= BAND_PAST and bb == 1)
    rows = bb * tl
    keep = min(BAND_PAST, l)
    assert keep == tl or (keep == BAND_PAST and tl == BAND_PAST)
    kern = functools.partial(_attn_kernel, bb=bb, tl=tl, c=c, has_past=has_past)
    xmap = lambda ib, it: (ib, it, 0)
    bmap = lambda ib, it: (ib, 0, 0)
    rmap = lambda ib, it: (ib * nt + it, 0, 0)
    in_specs = [pl.BlockSpec((bb, tl, d), xmap), pl.BlockSpec((bb, 8, d), bmap)]
    args = [x, mod]
    if has_past:
        in_specs += [pl.BlockSpec((bb, BAND_PAST, d), bmap)] * 2
        args += [past_k, past_v]
    names = ["wkv_k", "wkv_v", "wq", "wo", "bias", "lng", "lnb", "wrt", "br"]
    in_specs += [_const_spec(w[n].shape) for n in names]
    args += [w[n] for n in names]
    outs = pl.pallas_call(
        kern,
        grid=(nb, nt),
        in_specs=in_specs,
        out_specs=[pl.BlockSpec((bb, tl, d), xmap),
                   pl.BlockSpec((rows * (d // LANES), LANES), lambda ib, it: (ib * nt + it, 0)),
                   pl.BlockSpec((1, 8, rows), rmap),
                   pl.BlockSpec((1, 8, rows), rmap),
                   pl.BlockSpec((bb, tl, d), bmap),
                   pl.BlockSpec((bb, tl, d), bmap)],
        out_shape=[jax.ShapeDtypeStruct((b, l, d), F32),
                   jax.ShapeDtypeStruct((b * l * (d // LANES), LANES), F32),
                   jax.ShapeDtypeStruct((nb * nt, 8, rows), jnp.int32),
                   jax.ShapeDtypeStruct((nb * nt, 8, rows), F32),
                   jax.ShapeDtypeStruct((b, keep, d), F32),
                   jax.ShapeDtypeStruct((b, keep, d), F32)],
        scratch_shapes=[pltpu.VMEM((bb, BAND_PAST + tl, d), BF16),
                        pltpu.VMEM((bb, BAND_PAST + tl, d), BF16),
                        pltpu.VMEM((rows, d), BF16),
                        pltpu.VMEM((rows, d), F32),
                        pltpu.VMEM((ATT_HEADS, c, BAND_PAST + c), F32),
                        pltpu.VMEM((ATT_HEADS, c, LANES), F32),
                        pltpu.VMEM((ATT_HEADS, c, BAND_PAST + c), BF16)],
        compiler_params=_params(("arbitrary", "arbitrary")),
        name="attn_layer",
    )(*args)
    return outs


def _moe_kernel(be_ref, nact_ref, st_ref, stn_ref, sr_ref, hf_hbm, wg_ref, bg_ref, wu_ref, bu_ref,
                wd_ref, bd_ref, y_hbm, xbuf, ybuf, xb_s, hm_s, wgb, wub, wdb, xs_s, ws_s, gsem, ssem,
                *, blk, g):
    i = pl.program_id(0)
    nact = nact_ref[0]
    f = hm_s.shape[1]
    n_col = 4
    fc = f // n_col
    per = blk // n_col
    rc = min(blk, 128)

    def gather_row(tok_ref, j):
        src = pl.multiple_of(tok_ref[0, 0, j], g)
        return pltpu.make_async_copy(hf_hbm.at[pl.ds(src, g)], xbuf.at[pl.ds(j * g, g)], gsem)

    def gather_all():
        return pltpu.make_async_copy(hf_hbm.at[pl.ds(0, blk * g)], xbuf, gsem)

    def scatter_all():
        return pltpu.make_async_copy(ybuf, y_hbm.at[pl.ds(y_hbm.shape[0] - blk * g, blk * g)], ssem)

    @pl.when(i == 0)
    def _():
        ybuf[...] = jnp.zeros(ybuf.shape, F32)
        scatter_all().start()

        def prime(j, carry):
            src = pl.multiple_of(st_ref[0, 0, j], g)
            pltpu.make_async_copy(hf_hbm.at[pl.ds(src, g)],
                                  xbuf.at[pl.ds(pl.multiple_of(j * g, g), g)], gsem).start()
            return carry

        lax.fori_loop(0, blk, prime, 0)

    @pl.when(i < nact)
    def _():
        e = be_ref[i]
        prev = be_ref[jnp.maximum(i - 1, 0)]

        @pl.when((i == 0) | (e != prev))
        def _():
            for k, (w_ref, w8) in enumerate(((wg_ref, wgb), (wu_ref, wub), (wd_ref, wdb))):
                w = w_ref[0, 0]
                top = jnp.max(jnp.max(jnp.abs(w), axis=0, keepdims=True), axis=1, keepdims=True)
                top = jnp.maximum(top, 1e-30)
                w8[...] = (w * (F8_MAX / top)).astype(F8)
                ws_s[k:k + 1, :] = jnp.broadcast_to(top * (1.0 / F8_MAX), (1, LANES))

        gather_all().wait()
        xparts = [xbuf[pl.ds(s, blk, stride=g), :] for s in range(g)]
        row_top = jnp.max(jnp.abs(xparts[0]), axis=-1, keepdims=True)
        for s in range(1, g):
            row_top = jnp.maximum(row_top, jnp.max(jnp.abs(xparts[s]), axis=-1, keepdims=True))
        row_top = jnp.maximum(row_top, 1e-30)
        x_scale = F8_MAX / row_top
        for s in range(g):
            xb_s[:, s * LANES:(s + 1) * LANES] = (xparts[s] * x_scale).astype(F8)
        xs_s[...] = jnp.broadcast_to(row_top * (1.0 / F8_MAX), xs_s.shape)
        h_scale = F8_MAX / ((SWIGLU_LIMIT + 1.0) * SWIGLU_LIMIT)
        for c in range(n_col):
            if c == n_col - 1:
                scatter_all().wait()
            cols = slice(c * fc, (c + 1) * fc)
            xinv = xs_s[:, 0:1]
            a = jnp.minimum(_dot(xb_s[...], wgb[:, cols]) * (xinv * ws_s[0:1, 0:1]) + bg_ref[0, 0][:, cols],
                            SWIGLU_LIMIT)
            u = jnp.clip(_dot(xb_s[...], wub[:, cols]) * (xinv * ws_s[1:2, 0:1]) + bu_ref[0, 0][:, cols],
                         -SWIGLU_LIMIT, SWIGLU_LIMIT)
            hm_s[:, cols] = ((u + 1.0) * (a * jax.nn.sigmoid(SWIGLU_ALPHA * a)) * h_scale).astype(F8)
            for j in range(c * per, (c + 1) * per):
                gather_row(stn_ref, j).start()
        for r0 in range(0, blk, rc):
            y = _dot(hm_s[r0:r0 + rc, :], wdb[...]) * (ws_s[2:3, 0:1] * (1.0 / h_scale)) + bd_ref[0, 0]
            for s in range(g):
                ybuf[pl.ds(r0 * g + s, rc, stride=g), :] = y[:, s * LANES:(s + 1) * LANES]
            for j in range(r0, r0 + rc):
                dst = pl.multiple_of(sr_ref[0, 0, j], g)
                pltpu.make_async_copy(ybuf.at[pl.ds(j * g, g)], y_hbm.at[pl.ds(dst, g)], ssem).start()

        @pl.when(i == nact - 1)
        def _():
            scatter_all().wait()
            gather_all().wait()


def _moe_experts(hf_tm, block_e, n_active, slot_t, slot_r, layer, wg, bg, wu, bu, wd, bd, blk):
    d = wg.shape[2]
    g = d // LANES
    t = hf_tm.shape[0] // g
    f = wg.shape[-1]
    n_blocks = block_e.shape[0]
    kern = functools.partial(_moe_kernel, blk=blk, g=g)
    wmap = lambda i, be, na: (layer, be[i], 0, 0)
    smap = lambda i, be, na: (i, 0, 0)
    nmap = lambda i, be, na: (jnp.minimum(i + 1, n_blocks - 1), 0, 0)
    grid_spec = pltpu.PrefetchScalarGridSpec(
        num_scalar_prefetch=2,
        grid=(n_blocks,),
        in_specs=[pl.BlockSpec((1, 1, blk), smap, memory_space=pltpu.SMEM),
                  pl.BlockSpec((1, 1, blk), nmap, memory_space=pltpu.SMEM),
                  pl.BlockSpec((1, 1, blk), smap, memory_space=pltpu.SMEM),
                  pl.BlockSpec(memory_space=pl.ANY),
                  pl.BlockSpec((1, 1, d, f), wmap), pl.BlockSpec((1, 1, 1, f), wmap),
                  pl.BlockSpec((1, 1, d, f), wmap), pl.BlockSpec((1, 1, 1, f), wmap),
                  pl.BlockSpec((1, 1, f, d), wmap), pl.BlockSpec((1, 1, 1, d), wmap)],
        out_specs=pl.BlockSpec(memory_space=pl.ANY),
        scratch_shapes=[pltpu.VMEM((blk * g, LANES), F32), pltpu.VMEM((blk * g, LANES), F32),
                        pltpu.VMEM((blk, d), F8), pltpu.VMEM((blk, f), F8),
                        pltpu.VMEM((d, f), F8), pltpu.VMEM((d, f), F8), pltpu.VMEM((f, d), F8),
                        pltpu.VMEM((blk, LANES), F32), pltpu.VMEM((8, LANES), F32),
                        pltpu.SemaphoreType.DMA, pltpu.SemaphoreType.DMA])
    st3 = (slot_t * g).reshape(n_blocks, 1, blk)
    depth = wg.shape[0]
    return pl.pallas_call(
        kern,
        grid_spec=grid_spec,
        out_shape=jax.ShapeDtypeStruct(((TOP_K * t + blk) * g, LANES), F32),
        compiler_params=_params(("arbitrary",)),
        name="moe_experts",
    )(block_e, n_active, st3, st3, (slot_r * g).reshape(n_blocks, 1, blk), hf_tm,
      wg, bg.reshape(depth, N_EXPERTS, 1, f), wu, bu.reshape(depth, N_EXPERTS, 1, f),
      wd, bd.reshape(depth, N_EXPERTS, 1, d))


def _route(ti8, blk):
    nb, _, rows = ti8.shape
    t = nb * rows
    npair = t * TOP_K
    assert npair % blk == 0
    flat_e = ti8[:, :TOP_K, :].reshape(-1)
    experts = jnp.arange(N_EXPERTS, dtype=jnp.int32)
    counts = jnp.sum((flat_e[:, None] == experts[None, :]).astype(jnp.int32), axis=0)
    padded = (counts + blk - 1) // blk * blk
    pend = jnp.cumsum(padded)
    need = padded - counts
    pair_bits = 19
    assert npair <= 1 << pair_bits and blk <= 1 << pair_bits
    within = jnp.arange(blk, dtype=jnp.int32)[None, :]
    dummy_key = jnp.where(within < need[:, None],
                          (experts[:, None] << (pair_bits + 1)) | (1 << pair_bits) | within,
                          ((2 * N_EXPERTS) << (pair_bits + 1)) | (1 << pair_bits))
    real_key = (flat_e << (pair_bits + 1)) | jnp.arange(npair, dtype=jnp.int32)
    skey = lax.sort(jnp.concatenate([real_key, dummy_key.reshape(-1)]))
    n_blocks = npair // blk + N_EXPERTS
    real = ((skey >> pair_bits) & 1) == 0
    q = jnp.where(real, skey & ((1 << pair_bits) - 1), 0)
    tok = (q // (TOP_K * rows)) * rows + q % rows
    choice = (q // rows) % TOP_K
    slot_t = jnp.where(real, tok, 0)
    spare = npair + jnp.arange(n_blocks * blk, dtype=jnp.int32) % blk
    slot_r = jnp.where(real, choice * t + tok, spare)
    bstart = jnp.arange(n_blocks, dtype=jnp.int32) * blk
    block_e = jnp.minimum(jnp.sum((pend[None, :] <= bstart[:, None]).astype(jnp.int32), axis=1),
                          N_EXPERTS - 1)
    n_active = (pend[-1] // blk).astype(jnp.int32).reshape(1)
    return block_e, n_active, slot_t, slot_r


def _combine_kernel(x_ref, mod_ref, tg_ref, y0, y1, y2, y3, lng_ref, lnb_ref, o_ref):
    bb, tl, d = x_ref.shape
    rows = bb * tl
    sel = (lax.broadcasted_iota(jnp.int32, (8, LANES), 0) ==
           lax.broadcasted_iota(jnp.int32, (8, LANES), 1)).astype(F32)
    gt = lax.dot_general(tg_ref[0], sel, (((0,), (0,)), ((), ())), preferred_element_type=F32,
                         precision=lax.Precision.HIGHEST)
    ys = (y0, y1, y2, y3)
    terms = [gt[:, k:k + 1] * _load_token_major(ys[k], rows, d) for k in range(TOP_K)]
    ff = ((terms[0] + terms[1]) + (terms[2] + terms[3])).reshape(bb, tl, d)
    g_f = mod_ref[:, 5:6, :]
    o_ref[...] = _layer_norm(DEEPNORM_ALPHA * x_ref[...] + g_f * ff, lng_ref[...], lnb_ref[...])


def _combine_layer(x1, mod, tg8, y, lng, lnb, bb, tl):
    b, l, d = x1.shape
    nb, nt = b // bb, l // tl
    rows = bb * tl
    tb = (b * l) // rows
    ymaps = [functools.partial(lambda ib, it, k: (k * tb + ib * nt + it, 0), k=k) for k in range(TOP_K)]
    return pl.pallas_call(
        _combine_kernel,
        grid=(nb, nt),
        in_specs=[pl.BlockSpec((bb, tl, d), lambda ib, it: (ib, it, 0)),
                  pl.BlockSpec((bb, 8, d), lambda ib, it: (ib, 0, 0)),
                  pl.BlockSpec((1, 8, rows), lambda ib, it: (ib * nt + it, 0, 0))]
                 + [pl.BlockSpec((rows * (d // LANES), LANES), m) for m in ymaps]
                 + [_const_spec(lng.shape), _const_spec(lnb.shape)],
        out_specs=pl.BlockSpec((bb, tl, d), lambda ib, it: (ib, it, 0)),
        out_shape=jax.ShapeDtypeStruct((b, l, d), F32),
        compiler_params=_params(("arbitrary", "arbitrary")),
        name="moe_combine",
    )(x1, mod, tg8, y, y, y, y, lng, lnb)


def _moe_layer(x1, hf, ti8, tg8, mod, layer, moe_w, lng, lnb, blk, bb, tl):
    block_e, n_active, slot_t, slot_r = _route(ti8, blk)
    y = _moe_experts(hf, block_e, n_active, slot_t, slot_r, layer, *moe_w, blk)
    return _combine_layer(x1, mod, tg8, y, lng, lnb, bb, tl)


def _rel_bias_table(rel_bias, c):
    w = BAND_PAST + c
    m = jnp.arange(w + c - 1)
    f = rel_bias[:, jnp.clip(BAND_PAST + c - 1 - m, -REL_CLIP, REL_CLIP) + REL_CLIP].astype(F32)
    return jnp.stack([f[:, c - 1 - t:c - 1 - t + w] for t in range(c)], axis=1)


def _trunk(x, mods, s0, past_k, past_v, wts, gla_bb, att_bb, blk):
    b, l, d = x.shape
    tl = min(TOKEN_TILE, l)
    c = min(CHUNK, l)
    row = lambda v: v.reshape(1, -1)

    def mod_of(layer):
        m = mods[layer].reshape(b, 6, d)
        return jnp.concatenate([m, jnp.zeros((b, 2, d), F32)], axis=1)

    def router_w(layer):
        return wts["moe_wr"][layer].T, wts["moe_br"][layer].reshape(N_EXPERTS, 1)

    moe_w = (wts["moe_wg"], wts["moe_bg"], wts["moe_wu"], wts["moe_bu"], wts["moe_wd"], wts["moe_bd"])

    wrt, br = router_w(0)
    gw = dict(wq=wts["gla_wq"][0].astype(BF16), wk=wts["gla_wk"][0].astype(BF16),
              wv=wts["gla_wv"][0].astype(BF16), wr=wts["gla_wr"][0].astype(BF16),
              wa1=wts["gla_wa1"][0].astype(BF16), wa2=wts["gla_wa2"][0].astype(BF16),
              ba=row(wts["gla_ba"][0]), ng=row(wts["gla_norm_g"][0]),
              wo=wts["gla_wo"][0].astype(BF16), lng=row(wts["ln_g"][0, 0]), lnb=row(wts["ln_b"][0, 0]),
              wrt=wrt, br=br)
    mod0 = mod_of(0)
    s0t = jnp.swapaxes(s0, -1, -2)
    x1, hf, ti8, tg8, sfin_t = _gla_layer(x, mod0, s0t, gw, gla_bb, tl)
    x = _moe_layer(x1, hf, ti8, tg8, mod0, 0, moe_w, row(wts["ln_g"][0, 1]), row(wts["ln_b"][0, 1]),
                   blk, gla_bb, tl)
    s_fin = jnp.swapaxes(sfin_t, -1, -2)[None]

    wrt, br = router_w(1)
    aw = dict(wkv_k=wts["kv_wk"].astype(BF16), wkv_v=wts["kv_wv"].astype(BF16),
              wq=wts["att_wq"][0].astype(BF16), wo=wts["att_wo"][0].astype(BF16),
              bias=_rel_bias_table(wts["att_rel_bias"][0], c),
              lng=row(wts["ln_g"][1, 0]), lnb=row(wts["ln_b"][1, 0]), wrt=wrt, br=br)
    mod1 = mod_of(1)
    x1, hf, ti8, tg8, k_new, v_new = _attn_layer(x, mod1, past_k, past_v, aw, att_bb, tl)
    x = _moe_layer(x1, hf, ti8, tg8, mod1, 1, moe_w, row(wts["ln_g"][1, 1]), row(wts["ln_b"][1, 1]),
                   blk, att_bb, tl)
    keep = k_new.shape[1]
    hd = d // ATT_HEADS
    return (x, s_fin, k_new.reshape(b, keep, ATT_HEADS, hd), v_new.reshape(b, keep, ATT_HEADS, hd))


def kernel(x_prompt, x_sample, c_prompt, c_sample, state_gla, cache_k, cache_v, ada_w, ada_b, ln_g, ln_b, gla_wq, gla_wk, gla_wv, gla_wa1, gla_wa2, gla_ba, gla_wr, gla_norm_g, gla_wo, kv_wk, kv_wv, att_wq, att_wo, att_rel_bias, moe_wr, moe_br, moe_wg, moe_bg, moe_wu, moe_bu, moe_wd, moe_bd):
    wts = dict(ln_g=ln_g, ln_b=ln_b, gla_wq=gla_wq, gla_wk=gla_wk, gla_wv=gla_wv, gla_wa1=gla_wa1,
               gla_wa2=gla_wa2, gla_ba=gla_ba, gla_wr=gla_wr, gla_norm_g=gla_norm_g, gla_wo=gla_wo,
               kv_wk=kv_wk, kv_wv=kv_wv, att_wq=att_wq, att_wo=att_wo, att_rel_bias=att_rel_bias,
               moe_wr=moe_wr, moe_br=moe_br, moe_wg=moe_wg, moe_bg=moe_bg, moe_wu=moe_wu,
               moe_bu=moe_bu, moe_wd=moe_wd, moe_bd=moe_bd)
    b_p, l_p, d = x_prompt.shape
    b_s, l_s, _ = x_sample.shape
    assert cache_k.shape[1] == BAND_PAST
    mods = _ada_modulation(jnp.concatenate([c_prompt, c_sample], axis=0), ada_w, ada_b)
    mods_p, mods_s = mods[:, :b_p], mods[:, b_p:]

    s0_p = jnp.zeros((b_p,) + state_gla.shape[2:], F32)
    blk_p = MOE_BLOCK_LARGE if b_p * l_p * TOP_K >= 64 * MOE_BLOCK_LARGE else MOE_BLOCK_SMALL
    y_p, s_p, k_p, v_p = _trunk(x_prompt, mods_p, s0_p, None, None, wts, 1, 1, blk_p)

    pk = cache_k.reshape(b_s, BAND_PAST, d).astype(BF16)
    pv = cache_v.reshape(b_s, BAND_PAST, d).astype(BF16)
    blk_s = MOE_BLOCK_LARGE if b_s * l_s * TOP_K >= 64 * MOE_BLOCK_LARGE else MOE_BLOCK_SMALL
    y_s, s_s, k_s, v_s = _trunk(x_sample, mods_s, state_gla[0], pk, pv, wts, b_s, min(4, b_s), blk_s)
    return (y_p, y_s, s_p, k_p, v_p, s_s, k_s, v_s)
```

```python
import functools
import math

import jax
import jax.numpy as jnp
from jax import lax
from jax.experimental import pallas as pl
from jax.experimental.pallas import tpu as pltpu

F32 = jnp.float32
BF16 = jnp.bfloat16
F8 = jnp.float8_e4m3fn
F8_MAX = 448.0

CHUNK = 64
GLA_HEADS = 4
GLA_TAU = 16.0
ATT_HEADS = 16
BAND_PAST = 512
REL_CLIP = 128
N_EXPERTS = 32
TOP_K = 4
SWIGLU_LIMIT = 7.0
SWIGLU_ALPHA = 1.702
DEPTH = 2
DEEPNORM_ALPHA = (2.0 * DEPTH) ** 0.25
LN_EPS = 1e-5
NEG_INF = -1e30

V7X_VMEM_LIMIT_BYTES = 56 * 1024 * 1024
TOKEN_TILE = 512
GLA_SUB = 16
MOE_BLOCK_LARGE = 512
MOE_BLOCK_SMALL = 64
LANES = 128


def _dot(a, b):
    return jnp.dot(a, b, preferred_element_type=F32)


def _dot_nt(a, b):
    return lax.dot_general(a, b, (((1,), (1,)), ((), ())), preferred_element_type=F32)


def _dot_tn(a, b):
    return lax.dot_general(a, b, (((0,), (0,)), ((), ())), preferred_element_type=F32)


def _layer_norm(y, g, b):
    mu = jnp.mean(y, axis=-1, keepdims=True)
    yc = y - mu
    var = jnp.mean(yc * yc, axis=-1, keepdims=True)
    return yc * lax.rsqrt(var + LN_EPS) * g + b


def _store_token_major(ref, val):
    rows, d = val.shape
    g = d // LANES
    for s in range(g):
        ref[pl.ds(s, rows, stride=g), :] = val[:, s * LANES:(s + 1) * LANES]


def _load_token_major(ref, rows, d):
    g = d // LANES
    return jnp.concatenate([ref[pl.ds(s, rows, stride=g), :] for s in range(g)], axis=1)


def _const_spec(shape):
    nd = len(shape)
    return pl.BlockSpec(shape, lambda *_: (0,) * nd, pipeline_mode=pl.Buffered(1))


def _params(sem):
    return pltpu.CompilerParams(dimension_semantics=sem, vmem_limit_bytes=V7X_VMEM_LIMIT_BYTES)


def _ada_kernel(c_ref, w_ref, b_ref, o_ref):
    c = c_ref[...]
    s = c * jax.nn.sigmoid(c)
    o_ref[0] = jnp.dot(s, w_ref[0], preferred_element_type=F32,
                       precision=lax.Precision.HIGHEST) + b_ref[0]


def _ada_modulation(c, ada_w, ada_b):
    n, d = c.shape
    depth, _, d6 = ada_w.shape
    tn = d6 // 4
    return pl.pallas_call(
        _ada_kernel,
        grid=(depth, d6 // tn),
        in_specs=[pl.BlockSpec((n, d), lambda l, j: (0, 0)),
                  pl.BlockSpec((1, d, tn), lambda l, j: (l, 0, j)),
                  pl.BlockSpec((1, 1, tn), lambda l, j: (l, 0, j))],
        out_specs=pl.BlockSpec((1, n, tn), lambda l, j: (l, 0, j)),
        out_shape=jax.ShapeDtypeStruct((depth, n, d6), F32),
        compiler_params=_params(("arbitrary", "arbitrary")),
        name="ada_modulation",
    )(c, ada_w, ada_b.reshape(depth, 1, d6))


def _post_mixer(x3, mix3, g_m, sh_f, sc_f, lng, lnb, wrt, br):
    bb, tl, d = x3.shape
    rows = bb * tl
    x1 = _layer_norm(DEEPNORM_ALPHA * x3 + g_m * mix3, lng, lnb)
    hf = x1 * (1.0 + sc_f) + sh_f
    logits = lax.dot_general(wrt, hf.reshape(rows, d), (((1,), (1,)), ((), ())),
                             preferred_element_type=F32, precision=lax.Precision.HIGHEST) + br
    eidx = lax.broadcasted_iota(jnp.int32, logits.shape, 0).astype(F32)
    vals, idxs = [], []
    cur = logits
    for _ in range(TOP_K):
        m = jnp.max(cur, axis=0, keepdims=True)
        sel = jnp.min(jnp.where(cur == m, eidx, float(N_EXPERTS)), axis=0, keepdims=True)
        vals.append(m)
        idxs.append(sel)
        cur = jnp.where(eidx == sel, -jnp.inf, cur)
    ex = [jnp.exp(v - vals[0]) for v in vals]
    den = (ex[0] + ex[1]) + (ex[2] + ex[3])
    out_row = lax.broadcasted_iota(jnp.int32, (8, rows), 0)
    ti = jnp.zeros((8, rows), F32)
    tg = jnp.zeros((8, rows), F32)
    for k in range(TOP_K):
        ti = jnp.where(out_row == k, idxs[k], ti)
        tg = jnp.where(out_row == k, ex[k] / den, tg)
    return x1, hf, ti.astype(jnp.int32), tg


def _gla_kernel(x_ref, mod_ref, s0_ref, wq_ref, wk_ref, wv_ref, wr_ref, wa1_ref, wa2_ref, ba_ref,
                ng_ref, wo_ref, lng_ref, lnb_ref, wrt_ref, br_ref,
                x1_ref, hf_ref, ti_ref, tg_ref, sfin_ref,
                st_s, q_s, k_s, v_s, la_s, o_s, *, bb, tl, c):
    i = pl.program_id(1)
    d = x_ref.shape[-1]
    dk = q_s.shape[-1] // GLA_HEADS
    dv = v_s.shape[-1] // GLA_HEADS
    rows = bb * tl
    sub = min(GLA_SUB, c)

    @pl.when(i == 0)
    def _():
        st_s[...] = s0_ref[...]

    x3 = x_ref[...]
    mod = mod_ref[...]
    h = (x3 * (1.0 + mod[:, 1:2, :]) + mod[:, 0:1, :]).reshape(rows, d)
    hb = h.astype(BF16)
    q_s[...] = _dot(hb, wq_ref[...]) * (dk ** -0.5)
    k_s[...] = _dot(hb, wk_ref[...])
    v_s[...] = _dot(hb, wv_ref[...])
    a = _dot(_dot(hb, wa1_ref[...]).astype(BF16), wa2_ref[...]) + ba_ref[...]
    la_s[...] = (jnp.minimum(a, 0.0) - jnp.log1p(jnp.exp(-jnp.abs(a)))) * (1.0 / GLA_TAU)

    tril = (lax.broadcasted_iota(jnp.int32, (c, c), 0) >=
            lax.broadcasted_iota(jnp.int32, (c, c), 1)).astype(BF16)
    lane_c = lax.broadcasted_iota(jnp.int32, (sub, c), 1)
    row_c = lax.broadcasted_iota(jnp.int32, (sub, c), 0)
    krow = lax.broadcasted_iota(jnp.int32, (c, dk), 0)
    ng = ng_ref[...]
    chunks_per_seq = tl // c

    def chunk_body(n, carry):
        r0 = pl.multiple_of(n * c, c)
        bi = n // chunks_per_seq
        la = la_s[pl.ds(r0, c), :]
        la_hi = la.astype(BF16)
        la_lo = (la - la_hi.astype(F32)).astype(BF16)
        bcum = _dot(tril, la_hi) + _dot(tril, la_lo)
        for hd in range(GLA_HEADS):
            qh = q_s[pl.ds(r0, c), hd * dk:(hd + 1) * dk]
            kh = k_s[pl.ds(r0, c), hd * dk:(hd + 1) * dk]
            vh = v_s[pl.ds(r0, c), hd * dv:(hd + 1) * dv].astype(BF16)
            bh = bcum[:, hd * dk:(hd + 1) * dk]
            blocks = []
            for sb in range(c // sub):
                lo = sb * sub
                qi = qh[lo:lo + sub]
                bi_rows = bh[lo:lo + sub]
                if sb > 0:
                    ref_b = bh[lo:lo + 1]
                    qt = (qi * jnp.exp(bi_rows - ref_b)).astype(BF16)
                    kt = jnp.where(krow < lo, kh * jnp.exp(jnp.minimum(ref_b - bh, 0.0)), 0.0)
                    acc = _dot_nt(qt, kt.astype(BF16))
                else:
                    acc = jnp.zeros((sub, c), F32)
                for s in range(sub):
                    g = lo + s
                    e = jnp.exp(jnp.minimum(bi_rows - bh[g:g + 1], 0.0))
                    col = jnp.sum(qi * e * kh[g:g + 1], axis=-1, keepdims=True)
                    acc = jnp.where((lane_c == g) & (row_c >= s), col, acc)
                blocks.append(acc)
            amat = blocks[0] if len(blocks) == 1 else jnp.concatenate(blocks, axis=0)
            st = st_s[bi, hd]
            o = _dot(amat.astype(BF16), vh) + _dot_nt((qh * jnp.exp(bh)).astype(BF16), st.astype(BF16))
            ms = jnp.mean(o * o, axis=-1, keepdims=True)
            o_s[pl.ds(r0, c), hd * dv:(hd + 1) * dv] = o * lax.rsqrt(ms + LN_EPS) * ng
            bl = bh[c - 1:c]
            kd = (kh * jnp.exp(bl - bh)).astype(BF16)
            st_s[bi, hd] = st * jnp.exp(bl) + _dot_tn(vh, kd)
        return carry

    lax.fori_loop(0, rows // c, chunk_body, 0)

    r = _dot(hb, wr_ref[...])
    og = (o_s[...] * (r * jax.nn.sigmoid(r))).astype(BF16)
    mix = _dot(og, wo_ref[...]).reshape(bb, tl, d)
    x1, hf, ti, tg = _post_mixer(x3, mix, mod[:, 2:3, :], mod[:, 3:4, :], mod[:, 4:5, :],
                                 lng_ref[...], lnb_ref[...], wrt_ref[...], br_ref[...])
    x1_ref[...] = x1
    _store_token_major(hf_ref, hf.reshape(rows, d))
    ti_ref[0] = ti
    tg_ref[0] = tg

    @pl.when(i == pl.num_programs(1) - 1)
    def _():
        sfin_ref[...] = st_s[...]


def _gla_layer(x, mod, s0t, w, bb, tl):
    b, l, d = x.shape
    c = min(CHUNK, l)
    hk = w["wq"].shape[1]
    hv = w["wv"].shape[1]
    nb, nt = b // bb, l // tl
    assert bb == 1 or nt == 1
    rows = bb * tl
    dv, dk = s0t.shape[2], s0t.shape[3]
    kern = functools.partial(_gla_kernel, bb=bb, tl=tl, c=c)
    xmap = lambda ib, it: (ib, it, 0)
    bmap = lambda ib, it: (ib, 0, 0)
    smap = lambda ib, it: (ib, 0, 0, 0)
    rmap = lambda ib, it: (ib * nt + it, 0, 0)
    outs = pl.pallas_call(
        kern,
        grid=(nb, nt),
        in_specs=[pl.BlockSpec((bb, tl, d), xmap),
                  pl.BlockSpec((bb, 8, d), bmap),
                  pl.BlockSpec((bb, GLA_HEADS, dv, dk), smap),
                  _const_spec(w["wq"].shape), _const_spec(w["wk"].shape), _const_spec(w["wv"].shape),
                  _const_spec(w["wr"].shape), _const_spec(w["wa1"].shape), _const_spec(w["wa2"].shape),
                  _const_spec(w["ba"].shape), _const_spec(w["ng"].shape), _const_spec(w["wo"].shape),
                  _const_spec(w["lng"].shape), _const_spec(w["lnb"].shape),
                  _const_spec(w["wrt"].shape), _const_spec(w["br"].shape)],
        out_specs=[pl.BlockSpec((bb, tl, d), xmap),
                   pl.BlockSpec((rows * (d // LANES), LANES), lambda ib, it: (ib * nt + it, 0)),
                   pl.BlockSpec((1, 8, rows), rmap),
                   pl.BlockSpec((1, 8, rows), rmap),
                   pl.BlockSpec((bb, GLA_HEADS, dv, dk), smap)],
        out_shape=[jax.ShapeDtypeStruct((b, l, d), F32),
                   jax.ShapeDtypeStruct((b * l * (d // LANES), LANES), F32),
                   jax.ShapeDtypeStruct((nb * nt, 8, rows), jnp.int32),
                   jax.ShapeDtypeStruct((nb * nt, 8, rows), F32),
                   jax.ShapeDtypeStruct(s0t.shape, F32)],
        scratch_shapes=[pltpu.VMEM((bb, GLA_HEADS, dv, dk), F32),
                        pltpu.VMEM((rows, hk), F32), pltpu.VMEM((rows, hk), F32),
                        pltpu.VMEM((rows, hv), F32), pltpu.VMEM((rows, hk), F32),
                        pltpu.VMEM((rows, hv), F32)],
        compiler_params=_params(("arbitrary", "arbitrary")),
        name="gla_layer",
    )(x, mod, s0t, w["wq"], w["wk"], w["wv"], w["wr"], w["wa1"], w["wa2"], w["ba"], w["ng"],
      w["wo"], w["lng"], w["lnb"], w["wrt"], w["br"])
    return outs


def _attn_kernel(*refs, bb, tl, c, has_past):
    if has_past:
        (x_ref, mod_ref, pk_ref, pv_ref, wkv_k_ref, wkv_v_ref, wq_ref, wo_ref, bias_ref, lng_ref,
         lnb_ref, wrt_ref, br_ref, x1_ref, hf_ref, ti_ref, tg_ref, ko_ref, vo_ref,
         kw_s, vw_s, q_s, o_s, sc_s, m_s, e_s) = refs
    else:
        (x_ref, mod_ref, wkv_k_ref, wkv_v_ref, wq_ref, wo_ref, bias_ref, lng_ref,
         lnb_ref, wrt_ref, br_ref, x1_ref, hf_ref, ti_ref, tg_ref, ko_ref, vo_ref,
         kw_s, vw_s, q_s, o_s, sc_s, m_s, e_s) = refs
    i = pl.program_id(1)
    d = x_ref.shape[-1]
    p = BAND_PAST
    w = p + c
    rows = bb * tl
    hd2 = 2 * (d // ATT_HEADS)

    @pl.when(i == 0)
    def _():
        if has_past:
            kw_s[:, 0:p, :] = pk_ref[...]
            vw_s[:, 0:p, :] = pv_ref[...]
        else:
            kw_s[:, 0:p, :] = jnp.zeros((bb, p, d), BF16)
            vw_s[:, 0:p, :] = jnp.zeros((bb, p, d), BF16)

    x3 = x_ref[...]
    mod = mod_ref[...]
    xb = x3.reshape(rows, d).astype(BF16)
    hb = (x3 * (1.0 + mod[:, 1:2, :]) + mod[:, 0:1, :]).reshape(rows, d).astype(BF16)
    kn = _dot(xb, wkv_k_ref[...])
    vn = _dot(xb, wkv_v_ref[...])
    ko_ref[...] = kn.reshape(bb, tl, d)
    vo_ref[...] = vn.reshape(bb, tl, d)
    kw_s[:, p:p + tl, :] = kn.astype(BF16).reshape(bb, tl, d)
    vw_s[:, p:p + tl, :] = vn.astype(BF16).reshape(bb, tl, d)
    q_s[...] = (_dot(hb, wq_ref[...]) * ((d // ATT_HEADS) ** -0.5)).astype(BF16)

    lane = lax.broadcasted_iota(jnp.int32, (c, hd2), 1)
    low = lane < (hd2 // 2)
    kpos = lax.broadcasted_iota(jnp.int32, (1, w), 1)
    chunks_per_seq = tl // c

    def chunk_body(n, carry):
        r0 = pl.multiple_of(n * c, c)
        bi = n // chunks_per_seq
        ci = n - bi * chunks_per_seq
        w0 = pl.multiple_of(ci * c, c)
        if not has_past:
            valid = (kpos + (i * tl + ci * c)) >= p
        for hp in range(ATT_HEADS // 2):
            qp = q_s[pl.ds(r0, c), hp * hd2:(hp + 1) * hd2]
            kwin = kw_s[bi, pl.ds(w0, w), hp * hd2:(hp + 1) * hd2]
            for half in range(2):
                qm = jnp.where(low if half == 0 else ~low, qp, jnp.zeros_like(qp))
                s = _dot_nt(qm, kwin) + bias_ref[2 * hp + half]
                if not has_past:
                    s = jnp.where(valid, s, NEG_INF)
                sc_s[2 * hp + half] = s
        for hh in range(ATT_HEADS):
            m_s[hh] = jnp.broadcast_to(jnp.max(sc_s[hh], axis=-1, keepdims=True), (c, LANES))
        for hh in range(ATT_HEADS):
            s = sc_s[hh]
            m = m_s[hh]
            parts = [jnp.exp(s[:, j * LANES:(j + 1) * LANES] - m) for j in range(w // LANES)]
            if w % LANES:
                parts.append(jnp.exp(s[:, w - w % LANES:] - m[:, :w % LANES]))
            e_s[hh] = jnp.concatenate(parts, axis=1).astype(BF16)
        ones = jnp.ones((w, hd2), BF16)
        for hp in range(ATT_HEADS // 2):
            vext = jnp.concatenate([vw_s[bi, pl.ds(w0, w), hp * hd2:(hp + 1) * hd2], ones], axis=1)
            outs = []
            for half in range(2):
                r = _dot(e_s[2 * hp + half], vext)
                outs.append(r[:, :hd2] / r[:, hd2:])
            o_s[pl.ds(r0, c), hp * hd2:(hp + 1) * hd2] = jnp.where(low, outs[0], outs[1])
        return carry

    lax.fori_loop(0, rows // c, chunk_body, 0)

    if tl >= p:
        kw_s[:, 0:p, :] = kw_s[:, tl:tl + p, :]
        vw_s[:, 0:p, :] = vw_s[:, tl:tl + p, :]

    mix = _dot(o_s[...].astype(BF16), wo_ref[...]).reshape(bb, tl, d)
    x1, hf, ti, tg = _post_mixer(x3, mix, mod[:, 2:3, :], mod[:, 3:4, :], mod[:, 4:5, :],
                                 lng_ref[...], lnb_ref[...], wrt_ref[...], br_ref[...])
    x1_ref[...] = x1
    _store_token_major(hf_ref, hf.reshape(rows, d))
    ti_ref[0] = ti
    tg_ref[0] = tg


def _attn_layer(x, mod, past_k, past_v, w, bb, tl):
    b, l, d = x.shape
    c = min(CHUNK, l)
    has_past = past_k is not None
    nb, nt = b // bb, l // tl
    assert nt == 1 or (tl >= BAND_PAST and bb == 1)
    rows = bb * tl
    keep = min(BAND_PAST, l)
    assert keep == tl or (keep == BAND_PAST and tl == BAND_PAST)
    kern = functools.partial(_attn_kernel, bb=bb, tl=tl, c=c, has_past=has_past)
    xmap = lambda ib, it: (ib, it, 0)
    bmap = lambda ib, it: (ib, 0, 0)
    rmap = lambda ib, it: (ib * nt + it, 0, 0)
    in_specs = [pl.BlockSpec((bb, tl, d), xmap), pl.BlockSpec((bb, 8, d), bmap)]
    args = [x, mod]
    if has_past:
        in_specs += [pl.BlockSpec((bb, BAND_PAST, d), bmap)] * 2
        args += [past_k, past_v]
    names = ["wkv_k", "wkv_v", "wq", "wo", "bias", "lng", "lnb", "wrt", "br"]
    in_specs += [_const_spec(w[n].shape) for n in names]
    args += [w[n] for n in names]
    outs = pl.pallas_call(
        kern,
        grid=(nb, nt),
        in_specs=in_specs,
        out_specs=[pl.BlockSpec((bb, tl, d), xmap),
                   pl.BlockSpec((rows * (d // LANES), LANES), lambda ib, it: (ib * nt + it, 0)),
                   pl.BlockSpec((1, 8, rows), rmap),
                   pl.BlockSpec((1, 8, rows), rmap),
                   pl.BlockSpec((bb, tl, d), bmap),
                   pl.BlockSpec((bb, tl, d), bmap)],
        out_shape=[jax.ShapeDtypeStruct((b, l, d), F32),
                   jax.ShapeDtypeStruct((b * l * (d // LANES), LANES), F32),
                   jax.ShapeDtypeStruct((nb * nt, 8, rows), jnp.int32),
                   jax.ShapeDtypeStruct((nb * nt, 8, rows), F32),
                   jax.ShapeDtypeStruct((b, keep, d), F32),
                   jax.ShapeDtypeStruct((b, keep, d), F32)],
        scratch_shapes=[pltpu.VMEM((bb, BAND_PAST + tl, d), BF16),
                        pltpu.VMEM((bb, BAND_PAST + tl, d), BF16),
                        pltpu.VMEM((rows, d), BF16),
                        pltpu.VMEM((rows, d), F32),
                        pltpu.VMEM((ATT_HEADS, c, BAND_PAST + c), F32),
                        pltpu.VMEM((ATT_HEADS, c, LANES), F32),
                        pltpu.VMEM((ATT_HEADS, c, BAND_PAST + c), BF16)],
        compiler_params=_params(("arbitrary", "arbitrary")),
        name="attn_layer",
    )(*args)
    return outs


def _moe_kernel(be_ref, nact_ref, st_ref, stn_ref, sr_ref, hf_hbm, wg_ref, bg_ref, wu_ref, bu_ref,
                wd_ref, bd_ref, y_hbm, xbuf, ybuf, xb_s, hm_s, wgb, wub, wdb, xs_s, ws_s, gsem, ssem,
                *, blk, g):
    i = pl.program_id(0)
    nact = nact_ref[0]
    f = hm_s.shape[1]
    n_col = 4
    fc = f // n_col
    per = [3 * blk // 8, 3 * blk // 8, blk // 4, 0]
    row_pieces = [(0, 3 * blk // 4), (3 * blk // 4, blk)] if blk >= 256 else [(0, blk)]

    def gather_row(tok_ref, j):
        src = pl.multiple_of(tok_ref[0, 0, j], g)
        return pltpu.make_async_copy(hf_hbm.at[pl.ds(src, g)], xbuf.at[pl.ds(j * g, g)], gsem)

    def gather_all():
        return pltpu.make_async_copy(hf_hbm.at[pl.ds(0, blk * g)], xbuf, gsem)

    def scatter_all():
        return pltpu.make_async_copy(ybuf, y_hbm.at[pl.ds(y_hbm.shape[0] - blk * g, blk * g)], ssem)

    @pl.when(i == 0)
    def _():
        ybuf[...] = jnp.zeros(ybuf.shape, F32)
        scatter_all().start()

        def prime(j, carry):
            src = pl.multiple_of(st_ref[0, 0, j], g)
            pltpu.make_async_copy(hf_hbm.at[pl.ds(src, g)],
                                  xbuf.at[pl.ds(pl.multiple_of(j * g, g), g)], gsem).start()
            return carry

        lax.fori_loop(0, blk, prime, 0)

    @pl.when(i < nact)
    def _():
        e = be_ref[i]
        prev = be_ref[jnp.maximum(i - 1, 0)]

        @pl.when((i == 0) | (e != prev))
        def _():
            for k, (w_ref, w8) in enumerate(((wg_ref, wgb), (wu_ref, wub), (wd_ref, wdb))):
                w = w_ref[0, 0]
                top = jnp.max(jnp.max(jnp.abs(w), axis=0, keepdims=True), axis=1, keepdims=True)
                top = jnp.maximum(top, 1e-30)
                w8[...] = (w * (F8_MAX / top)).astype(F8)
                ws_s[k:k + 1, :] = jnp.broadcast_to(top * (1.0 / F8_MAX), (1, LANES))

        gather_all().wait()
        xparts = [xbuf[pl.ds(s, blk, stride=g), :] for s in range(g)]
        row_top = jnp.max(jnp.abs(xparts[0]), axis=-1, keepdims=True)
        for s in range(1, g):
            row_top = jnp.maximum(row_top, jnp.max(jnp.abs(xparts[s]), axis=-1, keepdims=True))
        row_top = jnp.maximum(row_top, 1e-30)
        x_scale = F8_MAX / row_top
        for s in range(g):
            xb_s[:, s * LANES:(s + 1) * LANES] = (xparts[s] * x_scale).astype(F8)
        xs_s[...] = jnp.broadcast_to(row_top * (1.0 / F8_MAX), xs_s.shape)
        h_scale = F8_MAX / ((SWIGLU_LIMIT + 1.0) * SWIGLU_LIMIT)
        for c in range(n_col):
            cols = slice(c * fc, (c + 1) * fc)
            xinv = xs_s[:, 0:1]
            a = jnp.minimum(_dot(xb_s[...], wgb[:, cols]) * (xinv * ws_s[0:1, 0:1]) + bg_ref[0, 0][:, cols],
                            SWIGLU_LIMIT)
            u = jnp.clip(_dot(xb_s[...], wub[:, cols]) * (xinv * ws_s[1:2, 0:1]) + bu_ref[0, 0][:, cols],
                         -SWIGLU_LIMIT, SWIGLU_LIMIT)
            hm_s[:, cols] = ((u + 1.0) * (a * jax.nn.sigmoid(SWIGLU_ALPHA * a)) * h_scale).astype(F8)
            for j in range(sum(per[:c]), sum(per[:c + 1])):
                gather_row(stn_ref, j).start()
        scatter_all().wait()
        for r0, r1 in row_pieces:
            rc = r1 - r0
            y = _dot(hm_s[r0:r1, :], wdb[...]) * (ws_s[2:3, 0:1] * (1.0 / h_scale)) + bd_ref[0, 0]
            for s in range(g):
                ybuf[pl.ds(r0 * g + s, rc, stride=g), :] = y[:, s * LANES:(s + 1) * LANES]
            for j in range(r0, r1):
                dst = pl.multiple_of(sr_ref[0, 0, j], g)
                pltpu.make_async_copy(ybuf.at[pl.ds(j * g, g)], y_hbm.at[pl.ds(dst, g)], ssem).start()

        @pl.when(i == nact - 1)
        def _():
            scatter_all().wait()
            gather_all().wait()


def _moe_experts(hf_tm, block_e, n_active, slot_t, slot_r, layer, wg, bg, wu, bu, wd, bd, blk):
    d = wg.shape[2]
    g = d // LANES
    t = hf_tm.shape[0] // g
    f = wg.shape[-1]
    n_blocks = block_e.shape[0]
    kern = functools.partial(_moe_kernel, blk=blk, g=g)
    wmap = lambda i, be, na: (layer, be[i], 0, 0)
    smap = lambda i, be, na: (i, 0, 0)
    nmap = lambda i, be, na: (jnp.minimum(i + 1, n_blocks - 1), 0, 0)
    grid_spec = pltpu.PrefetchScalarGridSpec(
        num_scalar_prefetch=2,
        grid=(n_blocks,),
        in_specs=[pl.BlockSpec((1, 1, blk), smap, memory_space=pltpu.SMEM),
                  pl.BlockSpec((1, 1, blk), nmap, memory_space=pltpu.SMEM),
                  pl.BlockSpec((1, 1, blk), smap, memory_space=pltpu.SMEM),
                  pl.BlockSpec(memory_space=pl.ANY),
                  pl.BlockSpec((1, 1, d, f), wmap), pl.BlockSpec((1, 1, 1, f), wmap),
                  pl.BlockSpec((1, 1, d, f), wmap), pl.BlockSpec((1, 1, 1, f), wmap),
                  pl.BlockSpec((1, 1, f, d), wmap), pl.BlockSpec((1, 1, 1, d), wmap)],
        out_specs=pl.BlockSpec(memory_space=pl.ANY),
        scratch_shapes=[pltpu.VMEM((blk * g, LANES), F32), pltpu.VMEM((blk * g, LANES), F32),
                        pltpu.VMEM((blk, d), F8), pltpu.VMEM((blk, f), F8),
                        pltpu.VMEM((d, f), F8), pltpu.VMEM((d, f), F8), pltpu.VMEM((f, d), F8),
                        pltpu.VMEM((blk, LANES), F32), pltpu.VMEM((8, LANES), F32),
                        pltpu.SemaphoreType.DMA, pltpu.SemaphoreType.DMA])
    st3 = (slot_t * g).reshape(n_blocks, 1, blk)
    depth = wg.shape[0]
    return pl.pallas_call(
        kern,
        grid_spec=grid_spec,
        out_shape=jax.ShapeDtypeStruct(((TOP_K * t + blk) * g, LANES), F32),
        compiler_params=_params(("arbitrary",)),
        name="moe_experts",
    )(block_e, n_active, st3, st3, (slot_r * g).reshape(n_blocks, 1, blk), hf_tm,
      wg, bg.reshape(depth, N_EXPERTS, 1, f), wu, bu.reshape(depth, N_EXPERTS, 1, f),
      wd, bd.reshape(depth, N_EXPERTS, 1, d))


def _route(ti8, blk):
    nb, _, rows = ti8.shape
    t = nb * rows
    npair = t * TOP_K
    assert npair % blk == 0
    flat_e = ti8[:, :TOP_K, :].reshape(-1)
    experts = jnp.arange(N_EXPERTS, dtype=jnp.int32)
    counts = jnp.sum((flat_e[:, None] == experts[None, :]).astype(jnp.int32), axis=0)
    padded = (counts + blk - 1) // blk * blk
    pend = jnp.cumsum(padded)
    need = padded - counts
    pair_bits = 19
    assert npair <= 1 << pair_bits and blk <= 1 << pair_bits
    within = jnp.arange(blk, dtype=jnp.int32)[None, :]
    dummy_key = jnp.where(within < need[:, None],
                          (experts[:, None] << (pair_bits + 1)) | (1 << pair_bits) | within,
                          ((2 * N_EXPERTS) << (pair_bits + 1)) | (1 << pair_bits))
    real_key = (flat_e << (pair_bits + 1)) | jnp.arange(npair, dtype=jnp.int32)
    skey = lax.sort(jnp.concatenate([real_key, dummy_key.reshape(-1)]))
    n_blocks = npair // blk + N_EXPERTS
    real = ((skey >> pair_bits) & 1) == 0
    q = jnp.where(real, skey & ((1 << pair_bits) - 1), 0)
    tok = (q // (TOP_K * rows)) * rows + q % rows
    choice = (q // rows) % TOP_K
    slot_t = jnp.where(real, tok, 0)
    spare = npair + jnp.arange(n_blocks * blk, dtype=jnp.int32) % blk
    slot_r = jnp.where(real, choice * t + tok, spare)
    bstart = jnp.arange(n_blocks, dtype=jnp.int32) * blk
    block_e = jnp.minimum(jnp.sum((pend[None, :] <= bstart[:, None]).astype(jnp.int32), axis=1),
                          N_EXPERTS - 1)
    n_active = (pend[-1] // blk).astype(jnp.int32).reshape(1)
    return block_e, n_active, slot_t, slot_r


def _combine_kernel(x_ref, mod_ref, tg_ref, y0, y1, y2, y3, lng_ref, lnb_ref, o_ref):
    bb, tl, d = x_ref.shape
    rows = bb * tl
    sel = (lax.broadcasted_iota(jnp.int32, (8, LANES), 0) ==
           lax.broadcasted_iota(jnp.int32, (8, LANES), 1)).astype(F32)
    gt = lax.dot_general(tg_ref[0], sel, (((0,), (0,)), ((), ())), preferred_element_type=F32,
                         precision=lax.Precision.HIGHEST)
    ys = (y0, y1, y2, y3)
    terms = [gt[:, k:k + 1] * _load_token_major(ys[k], rows, d) for k in range(TOP_K)]
    ff = ((terms[0] + terms[1]) + (terms[2] + terms[3])).reshape(bb, tl, d)
    g_f = mod_ref[:, 5:6, :]
    o_ref[...] = _layer_norm(DEEPNORM_ALPHA * x_ref[...] + g_f * ff, lng_ref[...], lnb_ref[...])


def _combine_layer(x1, mod, tg8, y, lng, lnb, bb, tl):
    b, l, d = x1.shape
    nb, nt = b // bb, l // tl
    rows = bb * tl
    tb = (b * l) // rows
    ymaps = [functools.partial(lambda ib, it, k: (k * tb + ib * nt + it, 0), k=k) for k in range(TOP_K)]
    return pl.pallas_call(
        _combine_kernel,
        grid=(nb, nt),
        in_specs=[pl.BlockSpec((bb, tl, d), lambda ib, it: (ib, it, 0)),
                  pl.BlockSpec((bb, 8, d), lambda ib, it: (ib, 0, 0)),
                  pl.BlockSpec((1, 8, rows), lambda ib, it: (ib * nt + it, 0, 0))]
                 + [pl.BlockSpec((rows * (d // LANES), LANES), m) for m in ymaps]
                 + [_const_spec(lng.shape), _const_spec(lnb.shape)],
        out_specs=pl.BlockSpec((bb, tl, d), lambda ib, it: (ib, it, 0)),
        out_shape=jax.ShapeDtypeStruct((b, l, d), F32),
        compiler_params=_params(("arbitrary", "arbitrary")),
        name="moe_combine",
    )(x1, mod, tg8, y, y, y, y, lng, lnb)


def _moe_layer(x1, hf, ti8, tg8, mod, layer, moe_w, lng, lnb, blk, bb, tl):
    block_e, n_active, slot_t, slot_r = _route(ti8, blk)
    y = _moe_experts(hf, block_e, n_active, slot_t, slot_r, layer, *moe_w, blk)
    return _combine_layer(x1, mod, tg8, y, lng, lnb, bb, tl)


def _rel_bias_table(rel_bias, c):
    w = BAND_PAST + c
    m = jnp.arange(w + c - 1)
    f = rel_bias[:, jnp.clip(BAND_PAST + c - 1 - m, -REL_CLIP, REL_CLIP) + REL_CLIP].astype(F32)
    return jnp.stack([f[:, c - 1 - t:c - 1 - t + w] for t in range(c)], axis=1)


def _trunk(x, mods, s0, past_k, past_v, wts, gla_bb, att_bb, blk):
    b, l, d = x.shape
    tl = min(TOKEN_TILE, l)
    c = min(CHUNK, l)
    row = lambda v: v.reshape(1, -1)

    def mod_of(layer):
        m = mods[layer].reshape(b, 6, d)
        return jnp.concatenate([m, jnp.zeros((b, 2, d), F32)], axis=1)

    def router_w(layer):
        return wts["moe_wr"][layer].T, wts["moe_br"][layer].reshape(N_EXPERTS, 1)

    moe_w = (wts["moe_wg"], wts["moe_bg"], wts["moe_wu"], wts["moe_bu"], wts["moe_wd"], wts["moe_bd"])

    wrt, br = router_w(0)
    gw = dict(wq=wts["gla_wq"][0].astype(BF16), wk=wts["gla_wk"][0].astype(BF16),
              wv=wts["gla_wv"][0].astype(BF16), wr=wts["gla_wr"][0].astype(BF16),
              wa1=wts["gla_wa1"][0].astype(BF16), wa2=wts["gla_wa2"][0].astype(BF16),
              ba=row(wts["gla_ba"][0]), ng=row(wts["gla_norm_g"][0]),
              wo=wts["gla_wo"][0].astype(BF16), lng=row(wts["ln_g"][0, 0]), lnb=row(wts["ln_b"][0, 0]),
              wrt=wrt, br=br)
    mod0 = mod_of(0)
    s0t = jnp.swapaxes(s0, -1, -2)
    x1, hf, ti8, tg8, sfin_t = _gla_layer(x, mod0, s0t, gw, gla_bb, tl)
    x = _moe_layer(x1, hf, ti8, tg8, mod0, 0, moe_w, row(wts["ln_g"][0, 1]), row(wts["ln_b"][0, 1]),
                   blk, gla_bb, tl)
    s_fin = jnp.swapaxes(sfin_t, -1, -2)[None]

    wrt, br = router_w(1)
    aw = dict(wkv_k=wts["kv_wk"].astype(BF16), wkv_v=wts["kv_wv"].astype(BF16),
              wq=wts["att_wq"][0].astype(BF16), wo=wts["att_wo"][0].astype(BF16),
              bias=_rel_bias_table(wts["att_rel_bias"][0], c),
              lng=row(wts["ln_g"][1, 0]), lnb=row(wts["ln_b"][1, 0]), wrt=wrt, br=br)
    mod1 = mod_of(1)
    x1, hf, ti8, tg8, k_new, v_new = _attn_layer(x, mod1, past_k, past_v, aw, att_bb, tl)
    x = _moe_layer(x1, hf, ti8, tg8, mod1, 1, moe_w, row(wts["ln_g"][1, 1]), row(wts["ln_b"][1, 1]),
                   blk, att_bb, tl)
    keep = k_new.shape[1]
    hd = d // ATT_HEADS
    return (x, s_fin, k_new.reshape(b, keep, ATT_HEADS, hd), v_new.reshape(b, keep, ATT_HEADS, hd))


def kernel(x_prompt, x_sample, c_prompt, c_sample, state_gla, cache_k, cache_v, ada_w, ada_b, ln_g, ln_b, gla_wq, gla_wk, gla_wv, gla_wa1, gla_wa2, gla_ba, gla_wr, gla_norm_g, gla_wo, kv_wk, kv_wv, att_wq, att_wo, att_rel_bias, moe_wr, moe_br, moe_wg, moe_bg, moe_wu, moe_bu, moe_wd, moe_bd):
    wts = dict(ln_g=ln_g, ln_b=ln_b, gla_wq=gla_wq, gla_wk=gla_wk, gla_wv=gla_wv, gla_wa1=gla_wa1,
               gla_wa2=gla_wa2, gla_ba=gla_ba, gla_wr=gla_wr, gla_norm_g=gla_norm_g, gla_wo=gla_wo,
               kv_wk=kv_wk, kv_wv=kv_wv, att_wq=att_wq, att_wo=att_wo, att_rel_bias=att_rel_bias,
               moe_wr=moe_wr, moe_br=moe_br, moe_wg=moe_wg, moe_bg=moe_bg, moe_wu=moe_wu,
               moe_bu=moe_bu, moe_wd=moe_wd, moe_bd=moe_bd)
    b_p, l_p, d = x_prompt.shape
    b_s, l_s, _ = x_sample.shape
    assert cache_k.shape[1] == BAND_PAST
    mods = _ada_modulation(jnp.concatenate([c_prompt, c_sample], axis=0), ada_w, ada_b)
    mods_p, mods_s = mods[:, :b_p], mods[:, b_p:]

    s0_p = jnp.zeros((b_p,) + state_gla.shape[2:], F32)
    blk_p = MOE_BLOCK_LARGE if b_p * l_p * TOP_K >= 64 * MOE_BLOCK_LARGE else MOE_BLOCK_SMALL
    y_p, s_p, k_p, v_p = _trunk(x_prompt, mods_p, s0_p, None, None, wts, 1, 1, blk_p)

    pk = cache_k.reshape(b_s, BAND_PAST, d).astype(BF16)
    pv = cache_v.reshape(b_s, BAND_PAST, d).astype(BF16)
    blk_s = MOE_BLOCK_LARGE if b_s * l_s * TOP_K >= 64 * MOE_BLOCK_LARGE else MOE_BLOCK_SMALL
    y_s, s_s, k_s, v_s = _trunk(x_sample, mods_s, state_gla[0], pk, pv, wts, b_s, min(4, b_s), blk_s)
    return (y_p, y_s, s_p, k_p, v_p, s_s, k_s, v_s)
```
